```python
import math
import jax, jax.numpy as jnp
from jax import lax
import numpy as np

D_MODEL = 2048
BATCH = 8
SEQ = 4096
DEPTH = 1
DEC_BATCH = 16
DEC_SEQ = 16
PAST_LEN = 1024

CHUNK = 64
LEFT_CHUNKS = 8
ATT_LEFT = LEFT_CHUNKS * CHUNK
BAND = (LEFT_CHUNKS + 1) * CHUNK
D_ATT = D_MODEL // 2
D_SSM = D_MODEL - D_ATT
N_HEADS = 8
HEAD_DIM = D_ATT // N_HEADS
SSM_GROUP = 16
N_SSM_GROUPS = D_SSM // SSM_GROUP
SSM_STATE = 64
REL_CLIP = 256
N_REL = 2 * REL_CLIP + 1
D_FF = ((8 * D_MODEL // 3 + 127) // 128) * 128
D_IN = 3 * D_ATT + D_SSM
EPS = 1e-6
NEG_INF = -1e30

kernel_name = "hybrid_chunk_attn_s5_macaron_step"


def _rmsnorm(x, g):
    xf = x.astype(jnp.float32)
    y = xf * lax.rsqrt(jnp.mean(xf * xf, axis=-1, keepdims=True) + EPS)
    return (y * g.astype(jnp.float32)).astype(x.dtype)


def _swiglu(x, w_gate, w_up, w_down):
    return (jax.nn.silu(x @ w_gate) * (x @ w_up)) @ w_down


def _rel_bias(rel_bias, dist):
    idx = jnp.clip(dist, -REL_CLIP, REL_CLIP) + REL_CLIP
    return rel_bias[:, idx].astype(jnp.float32)


def _attend(q, k, v, bias, mask):
    s = jnp.einsum('bqhd,bkhd->bhqk', q, k).astype(jnp.float32) * (HEAD_DIM ** -0.5) + bias[None]
    if mask is not None:
        s = jnp.where(mask, s, NEG_INF)
    p = jax.nn.softmax(s, axis=-1)
    return jnp.einsum('bhqk,bkhd->bqhd', p.astype(v.dtype), v)


def _attn_prompt(q, k, v, rel_bias):
    b, L = q.shape[:2]
    n_chunks = L // CHUNK
    qc = q.reshape(b, n_chunks, CHUNK, N_HEADS, HEAD_DIM).swapaxes(0, 1)
    pad = ((0, 0), (ATT_LEFT, 0), (0, 0), (0, 0))
    kp = jnp.pad(k, pad)
    vp = jnp.pad(v, pad)
    i = jnp.arange(CHUNK)
    r = jnp.arange(BAND)
    bias = _rel_bias(rel_bias, ATT_LEFT + i[:, None] - r[None, :])

    def one_chunk(args):
        c, q_c = args
        kb = lax.dynamic_slice_in_dim(kp, c * CHUNK, BAND, axis=1)
        vb = lax.dynamic_slice_in_dim(vp, c * CHUNK, BAND, axis=1)
        valid = r >= (LEFT_CHUNKS - c) * CHUNK
        return _attend(q_c, kb, vb, bias, valid[None, None, None, :])

    out = lax.map(one_chunk, (jnp.arange(n_chunks), qc))
    return out.swapaxes(0, 1).reshape(b, L, D_ATT)


def _attn_sample(q, k, v, rel_bias):
    b, S = q.shape[:2]
    W = k.shape[1] - S
    i = jnp.arange(S)
    r = jnp.arange(W + S)
    bias = _rel_bias(rel_bias, W + i[:, None] - r[None, :])
    return _attend(q, k, v, bias, None).reshape(b, S, D_ATT)


def _linrec(e1, e2):
    a1, b1 = e1
    a2, b2 = e2
    return a1 * a2, a2 * b1 + b2


def _ssm(u, s0, lam_re, lam_im, log_dt, b_re, b_im, c_re, c_im, d_skip):
    bsz, L, _ = u.shape
    f32 = jnp.float32
    ug = u.astype(f32).reshape(bsz, L, N_SSM_GROUPS, SSM_GROUP)
    lam = lax.complex(lam_re.astype(f32), lam_im.astype(f32))
    dt = jnp.exp(log_dt.astype(f32))[:, None]
    lam_bar = jnp.exp(lam * dt)
    bmat = lax.complex(b_re.astype(f32), b_im.astype(f32))
    b_bar = ((lam_bar - 1.0) / lam)[:, :, None] * bmat
    bu = jnp.einsum('gph,blgh->blgp', b_bar, ug.astype(jnp.complex64))
    bu = bu.at[:, 0].add(lam_bar[None] * s0)
    a = jnp.broadcast_to(lam_bar, (1, L) + lam_bar.shape)
    _, s = lax.associative_scan(_linrec, (a, bu), axis=1)
    cmat = lax.complex(c_re.astype(f32), c_im.astype(f32))
    y = jnp.einsum('ghp,blgp->blgh', cmat, s).real + d_skip.astype(f32) * ug
    last = s[:, -1]
    return y.reshape(bsz, L, D_SSM), last.real, last.imag


def _layer(x, cache_k, cache_v, s_re, s_im, p):
    (norm_ffn1, ffn1_w_gate, ffn1_w_up, ffn1_w_down, norm_mix, w_in, q_norm, k_norm, rel_bias,
     lam_re, lam_im, log_dt, b_re, b_im, c_re, c_im, d_skip, w_glu, b_glu,
     norm_att_out, norm_ssm_out, w_out, norm_ffn2, ffn2_w_gate, ffn2_w_up, ffn2_w_down, norm_final) = p
    bsz, L, _ = x.shape
    x = x + 0.5 * _swiglu(_rmsnorm(x, norm_ffn1), ffn1_w_gate, ffn1_w_up, ffn1_w_down)
    h = _rmsnorm(x, norm_mix)
    proj = h @ w_in
    q, k, v, u = jnp.split(proj, [D_ATT, 2 * D_ATT, 3 * D_ATT], axis=-1)
    q = _rmsnorm(q.reshape(bsz, L, N_HEADS, HEAD_DIM), q_norm)
    k = _rmsnorm(k.reshape(bsz, L, N_HEADS, HEAD_DIM), k_norm)
    v = v.reshape(bsz, L, N_HEADS, HEAD_DIM)
    if cache_k is None:
        att = _attn_prompt(q, k, v, rel_bias)
        rows = min(ATT_LEFT, L)
        new_k, new_v = k[:, L - rows:], v[:, L - rows:]
        s0 = jnp.zeros((bsz, N_SSM_GROUPS, SSM_STATE), jnp.complex64)
    else:
        k_all = jnp.concatenate([cache_k.astype(k.dtype), k], axis=1)
        v_all = jnp.concatenate([cache_v.astype(v.dtype), v], axis=1)
        att = _attn_sample(q, k_all, v_all, rel_bias)
        new_k, new_v = k_all[:, L:], v_all[:, L:]
        s0 = lax.complex(s_re.astype(jnp.float32), s_im.astype(jnp.float32))
    y_ssm, new_re, new_im = _ssm(u, s0, lam_re, lam_im, log_dt, b_re, b_im, c_re, c_im, d_skip)
    glu = jax.nn.gelu(y_ssm).astype(x.dtype) @ w_glu + b_glu
    ga, gb = jnp.split(glu, 2, axis=-1)
    ssm_out = ga * jax.nn.sigmoid(gb)
    mix = jnp.concatenate([_rmsnorm(att, norm_att_out), _rmsnorm(ssm_out, norm_ssm_out)], axis=-1)
    x = x + mix @ w_out
    x = x + 0.5 * _swiglu(_rmsnorm(x, norm_ffn2), ffn2_w_gate, ffn2_w_up, ffn2_w_down)
    x = _rmsnorm(x, norm_final)
    return x, new_k, new_v, new_re, new_im


def setup_inputs(seed: int = 0) -> dict:
    key = jax.random.key(seed)
    ks = iter(jax.random.split(key, 40))

    def nrm(shape, scale):
        return jax.random.normal(next(ks), shape, jnp.float32) * scale

    def gain(n):
        return 1.0 + nrm((DEPTH, n), 0.01)

    W_C = min(ATT_LEFT, PAST_LEN)
    n_idx = jnp.arange(SSM_STATE, dtype=jnp.float32)
    return {
        "x_prompt": nrm((BATCH, SEQ, D_MODEL), 1.0),
        "x_sample": nrm((DEC_BATCH, DEC_SEQ, D_MODEL), 1.0),
        "cache_attn_k": nrm((DEPTH, DEC_BATCH, W_C, N_HEADS, HEAD_DIM), 1.0),
        "cache_attn_v": nrm((DEPTH, DEC_BATCH, W_C, N_HEADS, HEAD_DIM), 1.0),
        "state_ssm_re": nrm((DEPTH, DEC_BATCH, N_SSM_GROUPS, SSM_STATE), 0.5),
        "state_ssm_im": nrm((DEPTH, DEC_BATCH, N_SSM_GROUPS, SSM_STATE), 0.5),
        "norm_ffn1": gain(D_MODEL),
        "ffn1_w_gate": nrm((DEPTH, D_MODEL, D_FF), D_MODEL ** -0.5),
        "ffn1_w_up": nrm((DEPTH, D_MODEL, D_FF), D_MODEL ** -0.5),
        "ffn1_w_down": nrm((DEPTH, D_FF, D_MODEL), D_FF ** -0.5),
        "norm_mix": gain(D_MODEL),
        "w_in": nrm((DEPTH, D_MODEL, D_IN), D_MODEL ** -0.5),
        "q_norm": gain(HEAD_DIM),
        "k_norm": gain(HEAD_DIM),
        "rel_bias": nrm((DEPTH, N_HEADS, N_REL), 0.1),
        "ssm_lambda_re": -0.5 + nrm((DEPTH, N_SSM_GROUPS, SSM_STATE), 0.01),
        "ssm_lambda_im": jnp.broadcast_to(math.pi * n_idx, (DEPTH, N_SSM_GROUPS, SSM_STATE)) + nrm((DEPTH, N_SSM_GROUPS, SSM_STATE), 0.01),
        "ssm_log_dt": jax.random.uniform(next(ks), (DEPTH, N_SSM_GROUPS), jnp.float32, math.log(1e-3), math.log(1e-1)),
        "ssm_b_re": nrm((DEPTH, N_SSM_GROUPS, SSM_STATE, SSM_GROUP), (2 * SSM_GROUP) ** -0.5),
        "ssm_b_im": nrm((DEPTH, N_SSM_GROUPS, SSM_STATE, SSM_GROUP), (2 * SSM_GROUP) ** -0.5),
        "ssm_c_re": nrm((DEPTH, N_SSM_GROUPS, SSM_GROUP, SSM_STATE), (2 * SSM_STATE) ** -0.5),
        "ssm_c_im": nrm((DEPTH, N_SSM_GROUPS, SSM_GROUP, SSM_STATE), (2 * SSM_STATE) ** -0.5),
        "ssm_d": nrm((DEPTH, N_SSM_GROUPS, SSM_GROUP), 0.5),
        "w_glu": nrm((DEPTH, D_SSM, 2 * D_SSM), D_SSM ** -0.5),
        "b_glu": nrm((DEPTH, 2 * D_SSM), 0.01),
        "norm_att_out": gain(D_ATT),
        "norm_ssm_out": gain(D_SSM),
        "w_out": nrm((DEPTH, D_MODEL, D_MODEL), D_MODEL ** -0.5),
        "norm_ffn2": gain(D_MODEL),
        "ffn2_w_gate": nrm((DEPTH, D_MODEL, D_FF), D_MODEL ** -0.5),
        "ffn2_w_up": nrm((DEPTH, D_MODEL, D_FF), D_MODEL ** -0.5),
        "ffn2_w_down": nrm((DEPTH, D_FF, D_MODEL), D_FF ** -0.5),
        "norm_final": gain(D_MODEL),
    }


def reference(x_prompt, x_sample, cache_attn_k, cache_attn_v, state_ssm_re, state_ssm_im,
              norm_ffn1, ffn1_w_gate, ffn1_w_up, ffn1_w_down, norm_mix, w_in, q_norm, k_norm, rel_bias,
              ssm_lambda_re, ssm_lambda_im, ssm_log_dt, ssm_b_re, ssm_b_im, ssm_c_re, ssm_c_im, ssm_d,
              w_glu, b_glu, norm_att_out, norm_ssm_out, w_out,
              norm_ffn2, ffn2_w_gate, ffn2_w_up, ffn2_w_down, norm_final):
    yp, ys = x_prompt, x_sample
    kp_l, vp_l, rp_l, ip_l = [], [], [], []
    kd_l, vd_l, rd_l, id_l = [], [], [], []
    for l in range(DEPTH):
        params = (norm_ffn1[l], ffn1_w_gate[l], ffn1_w_up[l], ffn1_w_down[l], norm_mix[l], w_in[l],
                  q_norm[l], k_norm[l], rel_bias[l],
                  ssm_lambda_re[l], ssm_lambda_im[l], ssm_log_dt[l], ssm_b_re[l], ssm_b_im[l],
                  ssm_c_re[l], ssm_c_im[l], ssm_d[l], w_glu[l], b_glu[l],
                  norm_att_out[l], norm_ssm_out[l], w_out[l],
                  norm_ffn2[l], ffn2_w_gate[l], ffn2_w_up[l], ffn2_w_down[l], norm_final[l])
        yp, kp, vp, rp, ip = _layer(yp, None, None, None, None, params)
        ys, kd, vd, rd, idd = _layer(ys, cache_attn_k[l], cache_attn_v[l], state_ssm_re[l], state_ssm_im[l], params)
        kp_l.append(kp); vp_l.append(vp); rp_l.append(rp); ip_l.append(ip)
        kd_l.append(kd); vd_l.append(vd); rd_l.append(rd); id_l.append(idd)
    return (yp, ys,
            jnp.stack(kp_l), jnp.stack(vp_l), jnp.stack(rp_l), jnp.stack(ip_l),
            jnp.stack(kd_l), jnp.stack(vd_l), jnp.stack(rd_l), jnp.stack(id_l))
```

```python
import functools
import math

import jax
import jax.numpy as jnp
from jax import lax
from jax.experimental import pallas as pl
from jax.experimental.pallas import tpu as pltpu

EPS = 1e-6
NEG_INF = -1e30
CHUNK = 64
LEFT_CHUNKS = 8
ATT_LEFT = LEFT_CHUNKS * CHUNK
REL_CLIP = 256
N_HEADS = 8
HEAD_DIM = 128
SSM_GROUP = 16
SSM_STATE = 64
SSM_T = 16
SSM_W = SSM_T * SSM_GROUP
ATT_QB = 256
ATT_KW = ATT_QB + ATT_LEFT
VMEM_LIMIT_V7X = 56 * 1024 * 1024

BF16 = jnp.bfloat16
F32 = jnp.float32


def _dot(a, b):
    return jnp.dot(a, b, preferred_element_type=F32)


def _rms(x, g):
    return x * lax.rsqrt(jnp.mean(x * x, axis=-1, keepdims=True) + EPS) * g


def _ffn_kernel(x_ref, g_ref, wg_ref, wu_ref, wd_ref, gf_ref, o_ref, xn_ref, *, final_norm):
    j = pl.program_id(1)

    @pl.when(j == 0)
    def _():
        xn_ref[...] = _rms(x_ref[...], g_ref[...]).astype(BF16)

    xn = xn_ref[...]
    a = _dot(xn, wg_ref[...])
    b = _dot(xn, wu_ref[...])
    h = (a * jax.nn.sigmoid(a) * b).astype(BF16)
    part = _dot(h, wd_ref[...])

    @pl.when(j == 0)
    def _():
        o_ref[...] = part

    @pl.when(j > 0)
    def _():
        o_ref[...] += part

    @pl.when(j == pl.num_programs(1) - 1)
    def _():
        y = x_ref[...] + 0.5 * o_ref[...]
        if final_norm:
            y = _rms(y, gf_ref[...])
        o_ref[...] = y


def _ffn(x, g, wg, wu, wd, gf, *, tm, tf, final_norm):
    n, d = x.shape
    fpad = wg.shape[1]
    return pl.pallas_call(
        functools.partial(_ffn_kernel, final_norm=final_norm),
        grid=(n // tm, fpad // tf),
        in_specs=[
            pl.BlockSpec((tm, d), lambda i, j: (i, 0)),
            pl.BlockSpec((1, d), lambda i, j: (0, 0)),
            pl.BlockSpec((d, tf), lambda i, j: (0, j)),
            pl.BlockSpec((d, tf), lambda i, j: (0, j)),
            pl.BlockSpec((tf, d), lambda i, j: (j, 0)),
            pl.BlockSpec((1, d), lambda i, j: (0, 0)),
        ],
        out_specs=pl.BlockSpec((tm, d), lambda i, j: (i, 0)),
        out_shape=jax.ShapeDtypeStruct((n, d), F32),
        scratch_shapes=[pltpu.VMEM((tm, d), BF16)],
        compiler_params=pltpu.CompilerParams(
            dimension_semantics=("parallel", "arbitrary"), vmem_limit_bytes=VMEM_LIMIT_V7X),
        name="ffn",
    )(x, g, wg, wu, wd, gf)


def _proj_kernel(x_ref, g_ref, w_ref, qn_ref, kn_ref, q_ref, k_ref, v_ref, u_ref, kf_ref, vf_ref,
                 *, blocks_per_seq):
    i = pl.program_id(0)
    d_att = N_HEADS * HEAD_DIM
    h = _rms(x_ref[...], g_ref[...]).astype(BF16)
    q = _dot(h, w_ref[:, 0:d_att])
    k = _dot(h, w_ref[:, d_att:2 * d_att])
    v = _dot(h, w_ref[:, 2 * d_att:3 * d_att])
    u_ref[...] = _dot(h, w_ref[:, 3 * d_att:]).astype(BF16)
    v_ref[...] = v.astype(BF16)
    qg = qn_ref[...] * (HEAD_DIM ** -0.5)
    kg = kn_ref[...]
    kn = []
    for hd in range(N_HEADS):
        sl = slice(hd * HEAD_DIM, (hd + 1) * HEAD_DIM)
        q_ref[:, sl] = _rms(q[:, sl], qg).astype(BF16)
        kh = _rms(k[:, sl], kg)
        k_ref[:, sl] = kh.astype(BF16)
        kn.append(kh)

    @pl.when(i % blocks_per_seq == blocks_per_seq - 1)
    def _():
        for hd in range(N_HEADS):
            kf_ref[:, hd * HEAD_DIM:(hd + 1) * HEAD_DIM] = kn[hd]
        vf_ref[...] = v


def _proj(x, g, w, qn, kn, *, tm, blocks_per_seq):
    n, d = x.shape
    d_att = N_HEADS * HEAD_DIM
    d_ssm = w.shape[1] - 3 * d_att
    n_seq = n // (tm * blocks_per_seq)
    tok = lambda i: (i, 0)
    const = lambda i: (0, 0)
    seq = lambda i: (i // blocks_per_seq, 0)
    return pl.pallas_call(
        functools.partial(_proj_kernel, blocks_per_seq=blocks_per_seq),
        grid=(n // tm,),
        in_specs=[
            pl.BlockSpec((tm, d), tok),
            pl.BlockSpec((1, d), const),
            pl.BlockSpec(w.shape, const, pipeline_mode=pl.Buffered(1)),
            pl.BlockSpec((1, HEAD_DIM), const),
            pl.BlockSpec((1, HEAD_DIM), const),
        ],
        out_specs=[
            pl.BlockSpec((tm, d_att), tok),
            pl.BlockSpec((tm, d_att), tok),
            pl.BlockSpec((tm, d_att), tok),
            pl.BlockSpec((tm, d_ssm), tok),
            pl.BlockSpec((tm, d_att), seq),
            pl.BlockSpec((tm, d_att), seq),
        ],
        out_shape=[
            jax.ShapeDtypeStruct((n, d_att), BF16),
            jax.ShapeDtypeStruct((n, d_att), BF16),
            jax.ShapeDtypeStruct((n, d_att), BF16),
            jax.ShapeDtypeStruct((n, d_ssm), BF16),
            jax.ShapeDtypeStruct((n_seq * tm, d_att), F32),
            jax.ShapeDtypeStruct((n_seq * tm, d_att), F32),
        ],
        compiler_params=pltpu.CompilerParams(
            dimension_semantics=("arbitrary",), vmem_limit_bytes=VMEM_LIMIT_V7X),
        name="proj",
    )(x, g, w, qn, kn)


def _softmax_pv(s, v):
    m = jnp.max(s, axis=-1, keepdims=True)
    p = jnp.exp(s - m)
    l = jnp.sum(p, axis=-1, keepdims=True)
    return _dot(p.astype(BF16), v) / l


def _qk(q, k):
    return lax.dot_general(q, k, (((1,), (1,)), ((), ())), preferred_element_type=F32)


def _attn_prompt_kernel(q_ref, k_ref, v_ref, bias_ref, o_ref, *, n_blocks):
    def block(q0, k0, kw):
        q = q_ref[pl.ds(q0, ATT_QB), :]
        k = k_ref[pl.ds(k0, kw), :]
        v = v_ref[pl.ds(k0, kw), :]
        s = _qk(q, k) + bias_ref[:, ATT_KW - kw:]
        o_ref[pl.ds(q0, ATT_QB), :] = _softmax_pv(s, v).astype(BF16)

    lead = ATT_LEFT // ATT_QB
    for i in range(min(lead, n_blocks)):
        block(i * ATT_QB, 0, (i + 1) * ATT_QB)

    def body(i, carry):
        q0 = pl.multiple_of(i * ATT_QB, ATT_QB)
        block(q0, pl.multiple_of(q0 - ATT_LEFT, ATT_QB), ATT_KW)
        return carry

    lax.fori_loop(lead, n_blocks, body, 0)


def _attn_prompt(q, k, v, bias, *, batch, seq):
    n, d_att = q.shape
    blk = pl.BlockSpec((seq, HEAD_DIM), lambda b, h: (b, h))
    return pl.pallas_call(
        functools.partial(_attn_prompt_kernel, n_blocks=seq // ATT_QB),
        grid=(batch, N_HEADS),
        in_specs=[blk, blk, blk, pl.BlockSpec((None, ATT_QB, ATT_KW), lambda b, h: (h, 0, 0))],
        out_specs=blk,
        out_shape=jax.ShapeDtypeStruct((n, d_att), BF16),
        compiler_params=pltpu.CompilerParams(
            dimension_semantics=("parallel", "parallel"), vmem_limit_bytes=VMEM_LIMIT_V7X),
        name="attn_prompt",
    )(q, k, v, bias)


def _attn_sample_kernel(q_ref, kn_ref, vn_ref, kc_ref, vc_ref, bias_ref, o_ref, *, w_cache):
    for hd in range(N_HEADS):
        sl = slice(hd * HEAD_DIM, (hd + 1) * HEAD_DIM)
        q = q_ref[:, sl]
        s1 = _qk(q, kc_ref[:, sl].astype(BF16)) + bias_ref[hd, :, :w_cache]
        s2 = _qk(q, kn_ref[:, sl].astype(BF16)) + bias_ref[hd, :, w_cache:]
        m = jnp.maximum(jnp.max(s1, axis=-1, keepdims=True), jnp.max(s2, axis=-1, keepdims=True))
        p1 = jnp.exp(s1 - m)
        p2 = jnp.exp(s2 - m)
        l = jnp.sum(p1, axis=-1, keepdims=True) + jnp.sum(p2, axis=-1, keepdims=True)
        o = _dot(p1.astype(BF16), vc_ref[:, sl].astype(BF16)) + _dot(p2.astype(BF16), vn_ref[:, sl].astype(BF16))
        o_ref[:, sl] = (o / l).astype(BF16)


def _attn_sample(q, kn, vn, kc, vc, bias, *, batch, seq):
    n, d_att = q.shape
    w_cache = kc.shape[1]
    tok = pl.BlockSpec((seq, d_att), lambda b: (b, 0))
    cache = pl.BlockSpec((None, w_cache, d_att), lambda b: (b, 0, 0))
    return pl.pallas_call(
        functools.partial(_attn_sample_kernel, w_cache=w_cache),
        grid=(batch,),
        in_specs=[tok, tok, tok, cache, cache, pl.BlockSpec(bias.shape, lambda b: (0, 0, 0))],
        out_specs=tok,
        out_shape=jax.ShapeDtypeStruct((n, d_att), BF16),
        compiler_params=pltpu.CompilerParams(
            dimension_semantics=("parallel",), vmem_limit_bytes=VMEM_LIMIT_V7X),
        name="attn_sample",
    )(q, kn, vn, kc, vc, bias)


def _ssm_kernel(u_ref, km_ref, bm_ref, cm_ref, a_ref, s0_ref, y_ref, sf_ref, sl_ref, sp_ref,
                *, rows, n_chunks, groups):
    half = 2 * SSM_STATE
    for gi in range(groups):
        sl_ref[gi] = _dot(u_ref[gi], bm_ref[gi])

    coef = []
    for gi in range(groups):
        coef.append(tuple(jnp.broadcast_to(a_ref[gi, r:r + 1, :], (rows, half)) for r in range(3)))

    def body(c, carry):
        r0 = pl.multiple_of(c * rows, rows)
        new = []
        for gi in range(groups):
            s, sw = carry[gi]
            a1, a2, a2w = coef[gi]
            sp_ref[gi, pl.ds(r0, rows), :] = s
            loc = sl_ref[gi, pl.ds(r0, rows), :]
            new.append((a1 * s + a2 * sw + loc[:, :half], a1 * sw + a2w * s + loc[:, half:]))
        return tuple(new)

    init = tuple((s0_ref[gi, 0], s0_ref[gi, 1]) for gi in range(groups))
    last = lax.fori_loop(0, n_chunks, body, init)
    for gi in range(groups):
        sf_ref[gi] = last[gi][0]
        y = _dot(u_ref[gi], km_ref[gi]) + _dot(sp_ref[gi].astype(BF16), cm_ref[gi])
        y_ref[gi] = jax.nn.gelu(y).astype(BF16)


def _ssm(ug, km, bm, cm, acoef, s0, *, rows, groups):
    n_g, n_rows, _ = ug.shape
    n_chunks = n_rows // rows
    half = 2 * SSM_STATE
    g3 = lambda g: (g, 0, 0)
    return pl.pallas_call(
        functools.partial(_ssm_kernel, rows=rows, n_chunks=n_chunks, groups=groups),
        grid=(n_g // groups,),
        in_specs=[
            pl.BlockSpec((groups, n_rows, SSM_W), g3),
            pl.BlockSpec((groups, SSM_W, SSM_W), g3),
            pl.BlockSpec((groups, SSM_W, 2 * half), g3),
            pl.BlockSpec((groups, half, SSM_W), g3),
            pl.BlockSpec((groups, 3, half), g3),
            pl.BlockSpec((groups, 2, rows, half), lambda g: (g, 0, 0, 0)),
        ],
        out_specs=[
            pl.BlockSpec((groups, n_rows, SSM_W), g3),
            pl.BlockSpec((groups, rows, half), g3),
        ],
        out_shape=[
            jax.ShapeDtypeStruct((n_g, n_rows, SSM_W), BF16),
            jax.ShapeDtypeStruct((n_g, rows, half), F32),
        ],
        scratch_shapes=[
            pltpu.VMEM((groups, n_rows, 2 * half), F32),
            pltpu.VMEM((groups, n_rows, half), F32),
        ],
        compiler_params=pltpu.CompilerParams(
            dimension_semantics=("parallel",), vmem_limit_bytes=VMEM_LIMIT_V7X),
        name="ssm",
    )(ug, km, bm, cm, acoef, s0)


def _mix_kernel(att_ref, yg_ref, x_ref, wglu_ref, bglu_ref, ga_ref, gs_ref, wout_ref, o_ref):
    d_ssm = yg_ref.shape[1]
    d_att = att_ref.shape[1]
    glu = _dot(yg_ref[...], wglu_ref[...]) + bglu_ref[...]
    ssm_out = glu[:, :d_ssm] * jax.nn.sigmoid(glu[:, d_ssm:])
    mix_s = _rms(ssm_out, gs_ref[...]).astype(BF16)
    mix_a = _rms(att_ref[...].astype(F32), ga_ref[...]).astype(BF16)
    o_ref[...] = x_ref[...] + _dot(mix_a, wout_ref[:d_att, :]) + _dot(mix_s, wout_ref[d_att:, :])


def _mix(att, yg, x, wglu, bglu, ga, gs, wout, *, tm):
    n, d = x.shape
    tok = lambda i: (i, 0)
    const = lambda i: (0, 0)
    once = lambda a: pl.BlockSpec(a.shape, const, pipeline_mode=pl.Buffered(1))
    return pl.pallas_call(
        _mix_kernel,
        grid=(n // tm,),
        in_specs=[
            pl.BlockSpec((tm, att.shape[1]), tok),
            pl.BlockSpec((tm, yg.shape[1]), tok),
            pl.BlockSpec((tm, d), tok),
            once(wglu), once(bglu), once(ga), once(gs), once(wout),
        ],
        out_specs=pl.BlockSpec((tm, d), tok),
        out_shape=jax.ShapeDtypeStruct((n, d), F32),
        compiler_params=pltpu.CompilerParams(
            dimension_semantics=("parallel",), vmem_limit_bytes=VMEM_LIMIT_V7X),
        name="mix",
    )(att, yg, x, wglu, bglu, ga, gs, wout)


def _ssm_matrices(lam_re, lam_im, log_dt, b_re, b_im, c_re, c_im, d_skip):
    hp = lax.Precision.HIGHEST
    n_g = lam_re.shape[0]
    dt = jnp.exp(log_dt)[:, None]
    n = jnp.arange(SSM_T + 1, dtype=F32)[:, None, None]
    mag = jnp.exp(lam_re * dt * n)
    ang = lam_im * dt * n
    pw_re, pw_im = mag * jnp.cos(ang), mag * jnp.sin(ang)
    x, y = pw_re[1] - 1.0, pw_im[1]
    den = lam_re * lam_re + lam_im * lam_im
    z_re, z_im = (x * lam_re + y * lam_im) / den, (y * lam_re - x * lam_im) / den
    bb_re = z_re[..., None] * b_re - z_im[..., None] * b_im
    bb_im = z_re[..., None] * b_im + z_im[..., None] * b_re
    pb_re = pw_re[..., None] * bb_re - pw_im[..., None] * bb_im
    pb_im = pw_re[..., None] * bb_im + pw_im[..., None] * bb_re
    kd = (jnp.einsum('gop,dgpi->dgoi', c_re, pb_re[:SSM_T], precision=hp)
          - jnp.einsum('gop,dgpi->dgoi', c_im, pb_im[:SSM_T], precision=hp))
    t = jnp.arange(SSM_T)
    lag = t[None, :] - t[:, None]
    kt = jnp.where((lag >= 0)[:, :, None, None, None], kd[jnp.clip(lag, 0, SSM_T - 1)], 0.0)
    kt = kt.transpose(2, 0, 4, 1, 3)
    eye_t = jnp.eye(SSM_T, dtype=F32)[None, :, None, :, None]
    eye_h = jnp.eye(SSM_GROUP, dtype=F32)[None, None, :, None, :]
    kt = kt + eye_t * eye_h * d_skip[:, None, :, None, None]
    km = kt.reshape(n_g, SSM_W, SSM_W)
    inj_re = pb_re[:SSM_T][::-1].transpose(1, 0, 3, 2).reshape(n_g, SSM_W, SSM_STATE)
    inj_im = pb_im[:SSM_T][::-1].transpose(1, 0, 3, 2).reshape(n_g, SSM_W, SSM_STATE)
    bm = jnp.concatenate([inj_re, inj_im, inj_im, inj_re], axis=-1)
    cp_re = c_re[None] * pw_re[1:, :, None, :] - c_im[None] * pw_im[1:, :, None, :]
    cp_im = c_re[None] * pw_im[1:, :, None, :] + c_im[None] * pw_re[1:, :, None, :]
    out_re = cp_re.transpose(1, 3, 0, 2).reshape(n_g, SSM_STATE, SSM_W)
    out_im = -cp_im.transpose(1, 3, 0, 2).reshape(n_g, SSM_STATE, SSM_W)
    cm = jnp.concatenate([out_re, out_im], axis=1)
    ar, ai = pw_re[SSM_T], pw_im[SSM_T]
    acoef = jnp.stack([jnp.concatenate([ar, ar], -1), jnp.concatenate([-ai, ai], -1),
                       jnp.concatenate([ai, -ai], -1)], axis=1)
    return km.astype(BF16), bm.astype(BF16), cm.astype(BF16), acoef


def _prompt_bias(rel_bias):
    r = jnp.arange(ATT_QB)[:, None]
    j = jnp.arange(ATT_KW)[None, :]
    dist = r + ATT_LEFT - j
    qc, kc = r // CHUNK, j // CHUNK
    band = (kc >= qc) & (kc <= qc + LEFT_CHUNKS)
    bias = rel_bias[:, jnp.clip(dist, -REL_CLIP, REL_CLIP) + REL_CLIP]
    return jnp.where(band[None], bias, NEG_INF).astype(F32)


def _sample_bias(rel_bias, w_cache, seq):
    i = jnp.arange(seq)[:, None]
    r = jnp.arange(w_cache + seq)[None, :]
    return rel_bias[:, jnp.clip(w_cache + i - r, -REL_CLIP, REL_CLIP) + REL_CLIP].astype(F32)


def _to_groups(u, batch, seq):
    n_g = u.shape[1] // SSM_GROUP
    u = u.reshape(batch, seq // SSM_T, SSM_T, n_g, SSM_GROUP)
    return u.transpose(3, 1, 0, 2, 4).reshape(n_g, (seq // SSM_T) * batch, SSM_W)


def _from_groups(y, batch, seq):
    n_g = y.shape[0]
    y = y.reshape(n_g, seq // SSM_T, batch, SSM_T, SSM_GROUP)
    return y.transpose(2, 1, 3, 0, 4).reshape(batch * seq, n_g * SSM_GROUP)


def _state_rows(s_re, s_im):
    s = jnp.concatenate([s_re, s_im], -1).transpose(1, 0, 2)
    sw = jnp.concatenate([s_im, s_re], -1).transpose(1, 0, 2)
    return jnp.stack([s, sw], axis=1)


def _pad_cols(w, mult):
    pad = (-w.shape[1]) % mult
    return jnp.pad(w, ((0, 0), (0, pad)))


def _stream(x, p, *, batch, seq, tm, ssm_groups, cache=None):
    bps = max(seq // tm, 1)
    x1 = _ffn(x, p['g_ffn1'], p['ffn1_wg'], p['ffn1_wu'], p['ffn1_wd'], p['g_final'],
              tm=tm, tf=p['tf'], final_norm=False)
    q, k, v, u, kf, vf = _proj(x1, p['g_mix'], p['w_in'], p['q_norm'], p['k_norm'], tm=tm, blocks_per_seq=bps)
    n_g = u.shape[1] // SSM_GROUP
    if cache is None:
        att = _attn_prompt(q, k, v, p['bias_prompt'], batch=batch, seq=seq)
        s0 = jnp.zeros((n_g, 2, batch, 2 * SSM_STATE), F32)
    else:
        ck, cv, s_re, s_im = cache
        w_cache = ck.shape[1]
        att = _attn_sample(q, kf, vf, ck.reshape(batch, w_cache, -1), cv.reshape(batch, w_cache, -1),
                           _sample_bias(p['rel_bias'], w_cache, seq), batch=batch, seq=seq)
        s0 = _state_rows(s_re, s_im)
    yg, sf = _ssm(_to_groups(u, batch, seq), p['km'], p['bm'], p['cm'], p['acoef'], s0,
                  rows=batch, groups=ssm_groups)
    x2 = _mix(att, _from_groups(yg, batch, seq), x1, p['w_glu'], p['b_glu'], p['g_att'], p['g_ssm'],
              p['w_out'], tm=tm)
    y = _ffn(x2, p['g_ffn2'], p['ffn2_wg'], p['ffn2_wu'], p['ffn2_wd'], p['g_final'],
             tm=tm, tf=p['tf'], final_norm=True)
    sf = sf.transpose(1, 0, 2)
    return y, kf, vf, sf[..., :SSM_STATE], sf[..., SSM_STATE:]


def kernel(x_prompt, x_sample, cache_attn_k, cache_attn_v, state_ssm_re, state_ssm_im, norm_ffn1, ffn1_w_gate, ffn1_w_up, ffn1_w_down, norm_mix, w_in, q_norm, k_norm, rel_bias, ssm_lambda_re, ssm_lambda_im, ssm_log_dt, ssm_b_re, ssm_b_im, ssm_c_re, ssm_c_im, ssm_d, w_glu, b_glu, norm_att_out, norm_ssm_out, w_out, norm_ffn2, ffn2_w_gate, ffn2_w_up, ffn2_w_down, norm_final):
    depth = norm_ffn1.shape[0]
    bp, lp, d = x_prompt.shape
    bs, ls, _ = x_sample.shape
    tf = 512
    yp = x_prompt.reshape(bp * lp, d)
    ys = x_sample.reshape(bs * ls, d)
    outs = [[] for _ in range(8)]
    for l in range(depth):
        km, bm, cm, acoef = _ssm_matrices(ssm_lambda_re[l], ssm_lambda_im[l], ssm_log_dt[l], ssm_b_re[l],
                                          ssm_b_im[l], ssm_c_re[l], ssm_c_im[l], ssm_d[l])
        row = lambda a: a[l][None, :]
        p = dict(
            tf=tf,
            g_ffn1=row(norm_ffn1), g_mix=row(norm_mix), g_att=row(norm_att_out), g_ssm=row(norm_ssm_out),
            g_ffn2=row(norm_ffn2), g_final=row(norm_final), q_norm=row(q_norm), k_norm=row(k_norm),
            ffn1_wg=_pad_cols(ffn1_w_gate[l], tf).astype(BF16), ffn1_wu=_pad_cols(ffn1_w_up[l], tf).astype(BF16),
            ffn1_wd=_pad_cols(ffn1_w_down[l].T, tf).T.astype(BF16),
            ffn2_wg=_pad_cols(ffn2_w_gate[l], tf).astype(BF16), ffn2_wu=_pad_cols(ffn2_w_up[l], tf).astype(BF16),
            ffn2_wd=_pad_cols(ffn2_w_down[l].T, tf).T.astype(BF16),
            w_in=w_in[l].astype(BF16), w_glu=w_glu[l].astype(BF16), b_glu=row(b_glu), w_out=w_out[l].astype(BF16),
            rel_bias=rel_bias[l], bias_prompt=_prompt_bias(rel_bias[l]),
            km=km, bm=bm, cm=cm, acoef=acoef,
        )
        yp, kp, vp, rp, ip = _stream(yp, p, batch=bp, seq=lp, tm=512, ssm_groups=4)
        ys, kd, vd, rd, idd = _stream(ys, p, batch=bs, seq=ls, tm=bs * ls, ssm_groups=8,
                                      cache=(cache_attn_k[l], cache_attn_v[l], state_ssm_re[l], state_ssm_im[l]))
        rows = min(ATT_LEFT, lp)
        kp = kp.reshape(bp, -1, N_HEADS, HEAD_DIM)[:, -rows:]
        vp = vp.reshape(bp, -1, N_HEADS, HEAD_DIM)[:, -rows:]
        kd = jnp.concatenate([cache_attn_k[l], kd.reshape(bs, ls, N_HEADS, HEAD_DIM)], axis=1)[:, ls:]
        vd = jnp.concatenate([cache_attn_v[l], vd.reshape(bs, ls, N_HEADS, HEAD_DIM)], axis=1)[:, ls:]
        for lst, val in zip(outs, (kp, vp, rp, ip, kd, vd, rd, idd)):
            lst.append(val)
    return (yp.reshape(bp, lp, d), ys.reshape(bs, ls, d)) + tuple(jnp.stack(o) for o in outs)
```

```python
import functools

import jax
import jax.numpy as jnp
from jax import lax
from jax.experimental import pallas as pl
from jax.experimental.pallas import tpu as pltpu

EPS = 1e-6
NEG_INF = -1e30
CHUNK = 64
LEFT_CHUNKS = 8
ATT_LEFT = LEFT_CHUNKS * CHUNK
REL_CLIP = 256
N_HEADS = 8
HEAD_DIM = 128
SSM_GROUP = 16
SSM_STATE = 64
LANES = 128
SSM_T = 16
SSM_W = SSM_T * SSM_GROUP
GROUPS_PER_TILE = LANES // SSM_GROUP
ATT_QB = 256
ATT_KW = ATT_QB + ATT_LEFT
REGROUP_PAD = 8
VMEM_LIMIT_V7X = 56 * 1024 * 1024

BF16 = jnp.bfloat16
F32 = jnp.float32


def _dot(a, b):
    return jnp.dot(a, b, preferred_element_type=F32)


def _rms(x, g):
    return x * lax.rsqrt(jnp.mean(x * x, axis=-1, keepdims=True) + EPS) * g


def _ffn_kernel(x_ref, g_ref, wg_ref, wu_ref, wd_ref, gf_ref, o_ref, xn_ref, *, final_norm):
    j = pl.program_id(1)

    @pl.when(j == 0)
    def _():
        x = x_ref[...]
        xn_ref[...] = _rms(x, g_ref[...]).astype(BF16)
        o_ref[...] = x

    xn = xn_ref[...]
    a = _dot(xn, wg_ref[...])
    b = _dot(xn, wu_ref[...])
    h = (a * jax.nn.sigmoid(a) * b).astype(BF16)
    o_ref[...] += _dot(h, wd_ref[...])

    if final_norm:
        @pl.when(j == pl.num_programs(1) - 1)
        def _():
            o_ref[...] = _rms(o_ref[...], gf_ref[...])


def _ffn(x, g, wg, wu, wd_half, gf, *, tm, tf, final_norm):
    n, d = x.shape
    fpad = wg.shape[1]
    return pl.pallas_call(
        functools.partial(_ffn_kernel, final_norm=final_norm),
        grid=(n // tm, fpad // tf),
        in_specs=[
            pl.BlockSpec((tm, d), lambda i, j: (i, 0)),
            pl.BlockSpec((1, d), lambda i, j: (0, 0)),
            pl.BlockSpec((d, tf), lambda i, j: (0, j)),
            pl.BlockSpec((d, tf), lambda i, j: (0, j)),
            pl.BlockSpec((tf, d), lambda i, j: (j, 0)),
            pl.BlockSpec((1, d), lambda i, j: (0, 0)),
        ],
        out_specs=pl.BlockSpec((tm, d), lambda i, j: (i, 0)),
        out_shape=jax.ShapeDtypeStruct((n, d), F32),
        scratch_shapes=[pltpu.VMEM((tm, d), BF16)],
        compiler_params=pltpu.CompilerParams(
            dimension_semantics=("parallel", "arbitrary"), vmem_limit_bytes=VMEM_LIMIT_V7X),
        name="ffn",
    )(x, g, wg, wu, wd_half, gf)


def _granule_transpose(vs):
    gran = lax.broadcasted_iota(jnp.int32, vs[0].shape, 1) // SSM_GROUP
    for s in (4, 2, 1):
        upper = (gran & s) != 0
        new = list(vs)
        for a in range(GROUPS_PER_TILE):
            if a & s == 0:
                lo, hi = vs[a], vs[a + s]
                new[a] = jnp.where(upper, pltpu.roll(hi, s * SSM_GROUP, 1), lo)
                new[a + s] = jnp.where(upper, hi, pltpu.roll(lo, LANES - s * SSM_GROUP, 1))
        vs = new
    return vs


def _regroup_geometry(nb, tb):
    return tb + REGROUP_PAD, tb // SSM_T


def _tokens_to_groups(slab_ref, ug_ref, *, nb, tb):
    pitch, ncl = _regroup_geometry(nb, tb)
    for j in range(slab_ref.shape[0]):
        xs = [jnp.concatenate([slab_ref[j, pl.ds(c * SSM_T + t, nb, stride=pitch), :] for c in range(ncl)], axis=0)
              for t in range(SSM_T)]
        lo = _granule_transpose(xs[:GROUPS_PER_TILE])
        hi = _granule_transpose(xs[GROUPS_PER_TILE:])
        for gl in range(GROUPS_PER_TILE):
            ug_ref[j * GROUPS_PER_TILE + gl] = jnp.concatenate([lo[gl], hi[gl]], axis=1).astype(ug_ref.dtype)


def _groups_to_tokens(yg_ref, slab_ref, *, nb, tb):
    pitch, ncl = _regroup_geometry(nb, tb)
    for j in range(slab_ref.shape[0]):
        ds = [yg_ref[j * GROUPS_PER_TILE + gl].astype(F32) for gl in range(GROUPS_PER_TILE)]
        lo = _granule_transpose([d[:, :LANES] for d in ds])
        hi = _granule_transpose([d[:, LANES:] for d in ds])
        for t, x in enumerate(lo + hi):
            for c in range(ncl):
                slab_ref[j, pl.ds(c * SSM_T + t, nb, stride=pitch), :] = x[c * nb:(c + 1) * nb]


def _proj_kernel(x_ref, g_ref, w_ref, qn_ref, kn_ref, q_ref, k_ref, v_ref, ug_ref, kf_ref, vf_ref, slab_ref,
                 *, first_tail_block):
    i = pl.program_id(0)
    nb, tb, d = x_ref.shape
    d_att = N_HEADS * HEAD_DIM
    pitch, _ = _regroup_geometry(nb, tb)
    h = _rms(x_ref[...].reshape(nb * tb, d), g_ref[...]).astype(BF16)
    q = _dot(h, w_ref[:, 0:d_att])
    k = _dot(h, w_ref[:, d_att:2 * d_att])
    v = _dot(h, w_ref[:, 2 * d_att:3 * d_att])
    u = _dot(h, w_ref[:, 3 * d_att:])
    for j in range(slab_ref.shape[0]):
        for b in range(nb):
            slab_ref[j, b * pitch:b * pitch + tb, :] = u[b * tb:(b + 1) * tb, j * LANES:(j + 1) * LANES]
    _tokens_to_groups(slab_ref, ug_ref, nb=nb, tb=tb)
    qg = qn_ref[...] * (HEAD_DIM ** -0.5)
    kg = kn_ref[...]
    qs, ks = [], []
    for hd in range(N_HEADS):
        sl = slice(hd * HEAD_DIM, (hd + 1) * HEAD_DIM)
        qs.append(_rms(q[:, sl], qg))
        ks.append(_rms(k[:, sl], kg))
    qn = jnp.concatenate(qs, axis=1)
    kn = jnp.concatenate(ks, axis=1)
    for b in range(nb):
        rows = slice(b * tb, (b + 1) * tb)
        q_ref[b] = qn[rows].astype(BF16)
        k_ref[b] = kn[rows].astype(BF16)
        v_ref[b] = v[rows].astype(BF16)

    @pl.when(i >= first_tail_block)
    def _():
        for b in range(nb):
            rows = slice(b * tb, (b + 1) * tb)
            kf_ref[b] = kn[rows]
            vf_ref[b] = v[rows]


def _proj(x, g, w, qn, kn, *, tb):
    nb, seq, d = x.shape
    d_att = N_HEADS * HEAD_DIM
    d_ssm = w.shape[1] - 3 * d_att
    n_g = d_ssm // SSM_GROUP
    n_blocks = seq // tb
    tail = min(ATT_LEFT, seq)
    first_tail_block = n_blocks - tail // tb
    pitch, ncl = _regroup_geometry(nb, tb)
    tok = lambda i: (0, i, 0)
    const = lambda i: (0, 0)
    tail_map = lambda i: (0, jnp.maximum(i - first_tail_block, 0), 0)
    return pl.pallas_call(
        functools.partial(_proj_kernel, first_tail_block=first_tail_block),
        grid=(n_blocks,),
        in_specs=[
            pl.BlockSpec((nb, tb, d), tok),
            pl.BlockSpec((1, d), const),
            pl.BlockSpec(w.shape, const, pipeline_mode=pl.Buffered(1)),
            pl.BlockSpec((1, HEAD_DIM), const),
            pl.BlockSpec((1, HEAD_DIM), const),
        ],
        out_specs=[
            pl.BlockSpec((nb, tb, d_att), tok),
            pl.BlockSpec((nb, tb, d_att), tok),
            pl.BlockSpec((nb, tb, d_att), tok),
            pl.BlockSpec((n_g, ncl * nb, SSM_W), tok),
            pl.BlockSpec((nb, tb, d_att), tail_map),
            pl.BlockSpec((nb, tb, d_att), tail_map),
        ],
        out_shape=[
            jax.ShapeDtypeStruct((nb, seq, d_att), BF16),
            jax.ShapeDtypeStruct((nb, seq, d_att), BF16),
            jax.ShapeDtypeStruct((nb, seq, d_att), BF16),
            jax.ShapeDtypeStruct((n_g, (seq // SSM_T) * nb, SSM_W), BF16),
            jax.ShapeDtypeStruct((nb, tail, d_att), F32),
            jax.ShapeDtypeStruct((nb, tail, d_att), F32),
        ],
        scratch_shapes=[pltpu.VMEM((d_ssm // LANES, nb * pitch, LANES), F32)],
        compiler_params=pltpu.CompilerParams(
            dimension_semantics=("arbitrary",), vmem_limit_bytes=VMEM_LIMIT_V7X),
        name="proj",
    )(x, g, w, qn, kn)


def _softmax_pv(s, v):
    m = jnp.max(s, axis=-1, keepdims=True)
    p = jnp.exp(s - m)
    l = jnp.sum(p, axis=-1, keepdims=True)
    return _dot(p.astype(BF16), v) / l


def _qk(q, k):
    return lax.dot_general(q, k, (((1,), (1,)), ((), ())), preferred_element_type=F32)


def _attn_prompt_kernel(q_ref, k_ref, v_ref, bias_ref, o_ref, *, n_blocks):
    def block(q0, k0, kw):
        q = q_ref[pl.ds(q0, ATT_QB), :]
        k = k_ref[pl.ds(k0, kw), :]
        v = v_ref[pl.ds(k0, kw), :]
        s = _qk(q, k) + bias_ref[:, ATT_KW - kw:]
        o_ref[pl.ds(q0, ATT_QB), :] = _softmax_pv(s, v).astype(BF16)

    lead = ATT_LEFT // ATT_QB
    for i in range(min(lead, n_blocks)):
        block(i * ATT_QB, 0, (i + 1) * ATT_QB)

    def body(i, carry):
        q0 = pl.multiple_of(i * ATT_QB, ATT_QB)
        block(q0, pl.multiple_of(q0 - ATT_LEFT, ATT_QB), ATT_KW)
        return carry

    lax.fori_loop(lead, n_blocks, body, 0)


def _attn_prompt(q, k, v, bias):
    batch, seq, d_att = q.shape
    blk = pl.BlockSpec((None, seq, HEAD_DIM), lambda b, h: (b, 0, h))
    return pl.pallas_call(
        functools.partial(_attn_prompt_kernel, n_blocks=seq // ATT_QB),
        grid=(batch, N_HEADS),
        in_specs=[blk, blk, blk, pl.BlockSpec((None, ATT_QB, ATT_KW), lambda b, h: (h, 0, 0))],
        out_specs=blk,
        out_shape=jax.ShapeDtypeStruct((batch, seq, d_att), BF16),
        compiler_params=pltpu.CompilerParams(
            dimension_semantics=("parallel", "parallel"), vmem_limit_bytes=VMEM_LIMIT_V7X),
        name="attn_prompt",
    )(q, k, v, bias)


def _attn_sample_kernel(q_ref, kn_ref, vn_ref, kc_ref, vc_ref, bias_ref, o_ref, *, w_cache):
    for hd in range(N_HEADS):
        sl = slice(hd * HEAD_DIM, (hd + 1) * HEAD_DIM)
        q = q_ref[:, sl]
        s1 = _qk(q, kc_ref[:, sl].astype(BF16)) + bias_ref[hd, :, :w_cache]
        s2 = _qk(q, kn_ref[:, sl].astype(BF16)) + bias_ref[hd, :, w_cache:]
        m = jnp.maximum(jnp.max(s1, axis=-1, keepdims=True), jnp.max(s2, axis=-1, keepdims=True))
        p1 = jnp.exp(s1 - m)
        p2 = jnp.exp(s2 - m)
        l = jnp.sum(p1, axis=-1, keepdims=True) + jnp.sum(p2, axis=-1, keepdims=True)
        o = _dot(p1.astype(BF16), vc_ref[:, sl].astype(BF16)) + _dot(p2.astype(BF16), vn_ref[:, sl].astype(BF16))
        o_ref[:, sl] = (o / l).astype(BF16)


def _attn_sample(q, kn, vn, kc, vc, bias):
    batch, seq, d_att = q.shape
    w_cache = kc.shape[1]
    tok = pl.BlockSpec((None, seq, d_att), lambda b: (b, 0, 0))
    cache = pl.BlockSpec((None, w_cache, d_att), lambda b: (b, 0, 0))
    return pl.pallas_call(
        functools.partial(_attn_sample_kernel, w_cache=w_cache),
        grid=(batch,),
        in_specs=[tok, tok, tok, cache, cache, pl.BlockSpec(bias.shape, lambda b: (0, 0, 0))],
        out_specs=tok,
        out_shape=jax.ShapeDtypeStruct((batch, seq, d_att), BF16),
        compiler_params=pltpu.CompilerParams(
            dimension_semantics=("parallel",), vmem_limit_bytes=VMEM_LIMIT_V7X),
        name="attn_sample",
    )(q, kn, vn, kc, vc, bias)


def _ssm_kernel(u_ref, km_ref, bm_ref, cm_ref, a_ref, s0_ref, y_ref, sf_ref, sl_ref, sp_ref,
                *, rows, n_chunks, groups):
    half = 2 * SSM_STATE
    for gi in range(groups):
        sl_ref[gi] = _dot(u_ref[gi], bm_ref[gi])

    coef = []
    for gi in range(groups):
        coef.append(tuple(jnp.broadcast_to(a_ref[gi, r:r + 1, :], (rows, half)) for r in range(3)))

    def body(c, carry):
        r0 = pl.multiple_of(c * rows, rows)
        new = []
        for gi in range(groups):
            s, sw = carry[gi]
            a1, a2, a2w = coef[gi]
            sp_ref[gi, pl.ds(r0, rows), :] = s
            loc = sl_ref[gi, pl.ds(r0, rows), :]
            new.append((a1 * s + a2 * sw + loc[:, :half], a1 * sw + a2w * s + loc[:, half:]))
        return tuple(new)

    init = tuple((s0_ref[gi, 0], s0_ref[gi, 1]) for gi in range(groups))
    last = lax.fori_loop(0, n_chunks, body, init)
    for gi in range(groups):
        sf_ref[gi] = last[gi][0]
        y = _dot(u_ref[gi], km_ref[gi]) + _dot(sp_ref[gi].astype(BF16), cm_ref[gi])
        y_ref[gi] = jax.nn.gelu(y).astype(BF16)


def _ssm(ug, km, bm, cm, acoef, s0, *, rows, groups):
    n_g, n_rows, _ = ug.shape
    n_chunks = n_rows // rows
    half = 2 * SSM_STATE
    g3 = lambda g: (g, 0, 0)
    return pl.pallas_call(
        functools.partial(_ssm_kernel, rows=rows, n_chunks=n_chunks, groups=groups),
        grid=(n_g // groups,),
        in_specs=[
            pl.BlockSpec((groups, n_rows, SSM_W), g3),
            pl.BlockSpec((groups, SSM_W, SSM_W), g3),
            pl.BlockSpec((groups, SSM_W, 2 * half), g3),
            pl.BlockSpec((groups, half, SSM_W), g3),
            pl.BlockSpec((groups, 3, half), g3),
            pl.BlockSpec((groups, 2, rows, half), lambda g: (g, 0, 0, 0)),
        ],
        out_specs=[
            pl.BlockSpec((groups, n_rows, SSM_W), g3),
            pl.BlockSpec((groups, rows, half), g3),
        ],
        out_shape=[
            jax.ShapeDtypeStruct((n_g, n_rows, SSM_W), BF16),
            jax.ShapeDtypeStruct((n_g, rows, half), F32),
        ],
        scratch_shapes=[
            pltpu.VMEM((groups, n_rows, 2 * half), F32),
            pltpu.VMEM((groups, n_rows, half), F32),
        ],
        compiler_params=pltpu.CompilerParams(
            dimension_semantics=("parallel",), vmem_limit_bytes=VMEM_LIMIT_V7X),
        name="ssm",
    )(ug, km, bm, cm, acoef, s0)


def _mix_kernel(att_ref, yg_ref, x_ref, wglu_ref, bglu_ref, ga_ref, gs_ref, wout_ref, o_ref, slab_ref, y_ref):
    nb, tb, d = x_ref.shape
    d_att = att_ref.shape[2]
    d_ssm = y_ref.shape[1]
    pitch, _ = _regroup_geometry(nb, tb)
    _groups_to_tokens(yg_ref, slab_ref, nb=nb, tb=tb)
    for j in range(slab_ref.shape[0]):
        for b in range(nb):
            y_ref[b * tb:(b + 1) * tb, j * LANES:(j + 1) * LANES] = slab_ref[j, b * pitch:b * pitch + tb, :].astype(BF16)
    glu = _dot(y_ref[...], wglu_ref[...]) + bglu_ref[...]
    ssm_out = glu[:, :d_ssm] * jax.nn.sigmoid(glu[:, d_ssm:])
    mix_s = _rms(ssm_out, gs_ref[...]).astype(BF16)
    att = att_ref[...].reshape(nb * tb, d_att).astype(F32)
    mix_a = _rms(att, ga_ref[...]).astype(BF16)
    o = x_ref[...].reshape(nb * tb, d) + _dot(mix_a, wout_ref[:d_att, :]) + _dot(mix_s, wout_ref[d_att:, :])
    o_ref[...] = o.reshape(nb, tb, d)


def _mix(att, yg, x, wglu, bglu, ga, gs, wout, *, tb):
    nb, seq, d = x.shape
    d_att = att.shape[2]
    n_g = yg.shape[0]
    d_ssm = n_g * SSM_GROUP
    pitch, ncl = _regroup_geometry(nb, tb)
    tok = lambda i: (0, i, 0)
    const = lambda i: (0, 0)
    once = lambda a: pl.BlockSpec(a.shape, const, pipeline_mode=pl.Buffered(1))
    return pl.pallas_call(
        _mix_kernel,
        grid=(seq // tb,),
        in_specs=[
            pl.BlockSpec((nb, tb, d_att), tok),
            pl.BlockSpec((n_g, ncl * nb, SSM_W), tok),
            pl.BlockSpec((nb, tb, d), tok),
            once(wglu), once(bglu), once(ga), once(gs), once(wout),
        ],
        out_specs=pl.BlockSpec((nb, tb, d), tok),
        out_shape=jax.ShapeDtypeStruct((nb, seq, d), F32),
        scratch_shapes=[pltpu.VMEM((d_ssm // LANES, nb * pitch, LANES), F32),
                        pltpu.VMEM((nb * tb, d_ssm), BF16)],
        compiler_params=pltpu.CompilerParams(
            dimension_semantics=("parallel",), vmem_limit_bytes=VMEM_LIMIT_V7X),
        name="mix",
    )(att, yg, x, wglu, bglu, ga, gs, wout)


def _ssm_matrices(lam_re, lam_im, log_dt, b_re, b_im, c_re, c_im, d_skip):
    hp = lax.Precision.HIGHEST
    n_g = lam_re.shape[0]
    dt = jnp.exp(log_dt)[:, None]
    n = jnp.arange(SSM_T + 1, dtype=F32)[:, None, None]
    mag = jnp.exp(lam_re * dt * n)
    ang = lam_im * dt * n
    pw_re, pw_im = mag * jnp.cos(ang), mag * jnp.sin(ang)
    x, y = pw_re[1] - 1.0, pw_im[1]
    den = lam_re * lam_re + lam_im * lam_im
    z_re, z_im = (x * lam_re + y * lam_im) / den, (y * lam_re - x * lam_im) / den
    bb_re = z_re[..., None] * b_re - z_im[..., None] * b_im
    bb_im = z_re[..., None] * b_im + z_im[..., None] * b_re
    pb_re = pw_re[..., None] * bb_re - pw_im[..., None] * bb_im
    pb_im = pw_re[..., None] * bb_im + pw_im[..., None] * bb_re
    kd = (jnp.einsum('gop,dgpi->dgoi', c_re, pb_re[:SSM_T], precision=hp)
          - jnp.einsum('gop,dgpi->dgoi', c_im, pb_im[:SSM_T], precision=hp))
    kt = jnp.stack([jnp.concatenate([jnp.zeros_like(kd[:t_in]), kd[:SSM_T - t_in]], axis=0)
                    for t_in in range(SSM_T)], axis=0)
    kt = kt.transpose(2, 0, 4, 1, 3)
    eye_t = jnp.eye(SSM_T, dtype=F32)[None, :, None, :, None]
    eye_h = jnp.eye(SSM_GROUP, dtype=F32)[None, None, :, None, :]
    kt = kt + eye_t * eye_h * d_skip[:, None, :, None, None]
    km = kt.reshape(n_g, SSM_W, SSM_W)
    inj_re = pb_re[:SSM_T][::-1].transpose(1, 0, 3, 2).reshape(n_g, SSM_W, SSM_STATE)
    inj_im = pb_im[:SSM_T][::-1].transpose(1, 0, 3, 2).reshape(n_g, SSM_W, SSM_STATE)
    bm = jnp.concatenate([inj_re, inj_im, inj_im, inj_re], axis=-1)
    cp_re = c_re[None] * pw_re[1:, :, None, :] - c_im[None] * pw_im[1:, :, None, :]
    cp_im = c_re[None] * pw_im[1:, :, None, :] + c_im[None] * pw_re[1:, :, None, :]
    out_re = cp_re.transpose(1, 3, 0, 2).reshape(n_g, SSM_STATE, SSM_W)
    out_im = -cp_im.transpose(1, 3, 0, 2).reshape(n_g, SSM_STATE, SSM_W)
    cm = jnp.concatenate([out_re, out_im], axis=1)
    ar, ai = pw_re[SSM_T], pw_im[SSM_T]
    acoef = jnp.stack([jnp.concatenate([ar, ar], -1), jnp.concatenate([-ai, ai], -1),
                       jnp.concatenate([ai, -ai], -1)], axis=1)
    return km.astype(BF16), bm.astype(BF16), cm.astype(BF16), acoef


def _bias_table(rel_bias):
    n_heads = rel_bias.shape[0]
    ext = ATT_QB + ATT_KW
    n_edge = ATT_LEFT - REL_CLIP + 1
    assert ATT_KW - n_edge == 2 * REL_CLIP - 1 and ATT_LEFT >= REL_CLIP
    far = rel_bias[:, 2 * REL_CLIP:]
    row = jnp.concatenate([jnp.broadcast_to(far, (n_heads, n_edge)), rel_bias[:, 1:2 * REL_CLIP][:, ::-1],
                           jnp.broadcast_to(far, (n_heads, ATT_QB))], axis=1)
    skew = jnp.tile(row, (1, ATT_QB))[:, :ATT_QB * (ext - 1)].reshape(n_heads, ATT_QB, ext - 1)
    return skew[:, :, :ATT_KW].astype(F32)


def _band_mask():
    r = jnp.arange(ATT_QB)[:, None] // CHUNK
    j = jnp.arange(ATT_KW)[None, :] // CHUNK
    return (j >= r) & (j <= r + LEFT_CHUNKS)


def _state_rows(s_re, s_im):
    s = jnp.concatenate([s_re, s_im], -1).transpose(1, 0, 2)
    sw = jnp.concatenate([s_im, s_re], -1).transpose(1, 0, 2)
    return jnp.stack([s, sw], axis=1)


def _pad_cols(w, mult):
    pad = (-w.shape[1]) % mult
    return jnp.pad(w, ((0, 0), (0, pad)))


def _stream(x, p, *, tm, tb, ssm_groups, cache=None):
    batch, seq, d = x.shape
    ffn = functools.partial(_ffn, tm=tm, tf=p['tf'])
    x1 = ffn(x.reshape(batch * seq, d), p['g_ffn1'], p['ffn1_wg'], p['ffn1_wu'], p['ffn1_wd'], p['g_final'],
             final_norm=False).reshape(batch, seq, d)
    q, k, v, ug, kf, vf = _proj(x1, p['g_mix'], p['w_in'], p['q_norm'], p['k_norm'], tb=tb)
    n_g = ug.shape[0]
    if cache is None:
        att = _attn_prompt(q, k, v, p['bias_prompt'])
        s0 = jnp.zeros((n_g, 2, batch, 2 * SSM_STATE), F32)
    else:
        ck, cv, s_re, s_im = cache
        w_cache = ck.shape[1]
        assert w_cache == ATT_LEFT and seq <= ATT_QB
        att = _attn_sample(q, kf, vf, ck.reshape(batch, w_cache, -1), cv.reshape(batch, w_cache, -1),
                           p['bias_table'][:, :seq, :w_cache + seq])
        s0 = _state_rows(s_re, s_im)
    yg, sf = _ssm(ug, p['km'], p['bm'], p['cm'], p['acoef'], s0, rows=batch, groups=ssm_groups)
    x2 = _mix(att, yg, x1, p['w_glu'], p['b_glu'], p['g_att'], p['g_ssm'], p['w_out'], tb=tb)
    y = ffn(x2.reshape(batch * seq, d), p['g_ffn2'], p['ffn2_wg'], p['ffn2_wu'], p['ffn2_wd'], p['g_final'],
            final_norm=True).reshape(batch, seq, d)
    sf = sf.transpose(1, 0, 2)
    return y, kf, vf, sf[..., :SSM_STATE], sf[..., SSM_STATE:]


def kernel(x_prompt, x_sample, cache_attn_k, cache_attn_v, state_ssm_re, state_ssm_im, norm_ffn1, ffn1_w_gate, ffn1_w_up, ffn1_w_down, norm_mix, w_in, q_norm, k_norm, rel_bias, ssm_lambda_re, ssm_lambda_im, ssm_log_dt, ssm_b_re, ssm_b_im, ssm_c_re, ssm_c_im, ssm_d, w_glu, b_glu, norm_att_out, norm_ssm_out, w_out, norm_ffn2, ffn2_w_gate, ffn2_w_up, ffn2_w_down, norm_final):
    depth = norm_ffn1.shape[0]
    bs, ls, _ = x_sample.shape
    tf = 512
    yp, ys = x_prompt, x_sample
    outs = [[] for _ in range(8)]
    for l in range(depth):
        km, bm, cm, acoef = _ssm_matrices(ssm_lambda_re[l], ssm_lambda_im[l], ssm_log_dt[l], ssm_b_re[l],
                                          ssm_b_im[l], ssm_c_re[l], ssm_c_im[l], ssm_d[l])
        row = lambda a: a[l][None, :]
        half_down = lambda w: (0.5 * _pad_cols(w.T, tf).T).astype(BF16)
        bias_table = _bias_table(rel_bias[l])
        p = dict(
            tf=tf,
            g_ffn1=row(norm_ffn1), g_mix=row(norm_mix), g_att=row(norm_att_out), g_ssm=row(norm_ssm_out),
            g_ffn2=row(norm_ffn2), g_final=row(norm_final), q_norm=row(q_norm), k_norm=row(k_norm),
            ffn1_wg=_pad_cols(ffn1_w_gate[l], tf).astype(BF16), ffn1_wu=_pad_cols(ffn1_w_up[l], tf).astype(BF16),
            ffn1_wd=half_down(ffn1_w_down[l]),
            ffn2_wg=_pad_cols(ffn2_w_gate[l], tf).astype(BF16), ffn2_wu=_pad_cols(ffn2_w_up[l], tf).astype(BF16),
            ffn2_wd=half_down(ffn2_w_down[l]),
            w_in=w_in[l].astype(BF16), w_glu=w_glu[l].astype(BF16), b_glu=row(b_glu), w_out=w_out[l].astype(BF16),
            bias_table=bias_table, bias_prompt=jnp.where(_band_mask()[None], bias_table, NEG_INF),
            km=km, bm=bm, cm=cm, acoef=acoef,
        )
        yp, kp, vp, rp, ip = _stream(yp, p, tm=512, tb=CHUNK, ssm_groups=4)
        ys, kd, vd, rd, idd = _stream(ys, p, tm=bs * ls, tb=ls, ssm_groups=8,
                                      cache=(cache_attn_k[l], cache_attn_v[l], state_ssm_re[l], state_ssm_im[l]))
        kp = kp.reshape(kp.shape[0], -1, N_HEADS, HEAD_DIM)
        vp = vp.reshape(vp.shape[0], -1, N_HEADS, HEAD_DIM)
        kd = jnp.concatenate([cache_attn_k[l], kd.reshape(bs, ls, N_HEADS, HEAD_DIM)], axis=1)[:, ls:]
        vd = jnp.concatenate([cache_attn_v[l], vd.reshape(bs, ls, N_HEADS, HEAD_DIM)], axis=1)[:, ls:]
        for lst, val in zip(outs, (kp, vp, rp, ip, kd, vd, rd, idd)):
            lst.append(val)
    return (yp, ys) + tuple(jnp.stack(o) for o in outs)
```

```python
import functools

import jax
import jax.numpy as jnp
from jax import lax
from jax.experimental import pallas as pl
from jax.experimental.pallas import tpu as pltpu

EPS = 1e-6
NEG_INF = -1e30
CHUNK = 64
LEFT_CHUNKS = 8
ATT_LEFT = LEFT_CHUNKS * CHUNK
REL_CLIP = 256
N_HEADS = 8
HEAD_DIM = 128
SSM_GROUP = 16
SSM_STATE = 64
LANES = 128
SSM_T = 16
SSM_W = SSM_T * SSM_GROUP
GROUPS_PER_TILE = LANES // SSM_GROUP
ATT_QB = 256
ATT_KW = ATT_QB + ATT_LEFT
REGROUP_PAD = 8
VMEM_LIMIT_V7X = 56 * 1024 * 1024

BF16 = jnp.bfloat16
F32 = jnp.float32


def _dot(a, b):
    return jnp.dot(a, b, preferred_element_type=F32)


def _rms(x, g):
    return x * lax.rsqrt(jnp.mean(x * x, axis=-1, keepdims=True) + EPS) * g


def _ffn_kernel(x_ref, g_ref, wg_ref, wu_ref, wd_ref, gf_ref, o_ref, xn_ref, *, final_norm):
    j = pl.program_id(1)

    @pl.when(j == 0)
    def _():
        x = x_ref[...]
        xn_ref[...] = _rms(x, g_ref[...]).astype(BF16)
        o_ref[...] = x

    xn = xn_ref[...]
    a = _dot(xn, wg_ref[...])
    b = _dot(xn, wu_ref[...])
    h = (a * jax.nn.sigmoid(a) * b).astype(BF16)
    o_ref[...] += _dot(h, wd_ref[...])

    if final_norm:
        @pl.when(j == pl.num_programs(1) - 1)
        def _():
            o_ref[...] = _rms(o_ref[...], gf_ref[...])


def _ffn(x, g, wg, wu, wd_half, gf, *, tm, tf, final_norm):
    n, d = x.shape
    fpad = wg.shape[1]
    return pl.pallas_call(
        functools.partial(_ffn_kernel, final_norm=final_norm),
        grid=(n // tm, fpad // tf),
        in_specs=[
            pl.BlockSpec((tm, d), lambda i, j: (i, 0)),
            pl.BlockSpec((1, d), lambda i, j: (0, 0)),
            pl.BlockSpec((d, tf), lambda i, j: (0, j)),
            pl.BlockSpec((d, tf), lambda i, j: (0, j)),
            pl.BlockSpec((tf, d), lambda i, j: (j, 0)),
            pl.BlockSpec((1, d), lambda i, j: (0, 0)),
        ],
        out_specs=pl.BlockSpec((tm, d), lambda i, j: (i, 0)),
        out_shape=jax.ShapeDtypeStruct((n, d), F32),
        scratch_shapes=[pltpu.VMEM((tm, d), BF16)],
        compiler_params=pltpu.CompilerParams(
            dimension_semantics=("parallel", "arbitrary"), vmem_limit_bytes=VMEM_LIMIT_V7X),
        name="ffn",
    )(x, g, wg, wu, wd_half, gf)


def _granule_transpose(vs):
    gran = lax.broadcasted_iota(jnp.int32, vs[0].shape, 1) // SSM_GROUP
    for s in (4, 2, 1):
        upper = (gran & s) != 0
        new = list(vs)
        for a in range(GROUPS_PER_TILE):
            if a & s == 0:
                lo, hi = vs[a], vs[a + s]
                new[a] = jnp.where(upper, pltpu.roll(hi, s * SSM_GROUP, 1), lo)
                new[a + s] = jnp.where(upper, hi, pltpu.roll(lo, LANES - s * SSM_GROUP, 1))
        vs = new
    return vs


def _regroup_geometry(nb, tb):
    return tb + REGROUP_PAD, tb // SSM_T


def _tokens_to_groups(slab_ref, ug_ref, *, nb, tb):
    pitch, ncl = _regroup_geometry(nb, tb)
    for j in range(slab_ref.shape[0]):
        xs = [jnp.concatenate([slab_ref[j, pl.ds(c * SSM_T + t, nb, stride=pitch), :] for c in range(ncl)], axis=0)
              for t in range(SSM_T)]
        lo = _granule_transpose(xs[:GROUPS_PER_TILE])
        hi = _granule_transpose(xs[GROUPS_PER_TILE:])
        for gl in range(GROUPS_PER_TILE):
            ug_ref[j * GROUPS_PER_TILE + gl] = jnp.concatenate([lo[gl], hi[gl]], axis=1).astype(ug_ref.dtype)


def _groups_to_tokens(yg_ref, slab_ref, *, nb, tb):
    pitch, ncl = _regroup_geometry(nb, tb)
    for j in range(slab_ref.shape[0]):
        ds = [yg_ref[j * GROUPS_PER_TILE + gl].astype(F32) for gl in range(GROUPS_PER_TILE)]
        lo = _granule_transpose([d[:, :LANES] for d in ds])
        hi = _granule_transpose([d[:, LANES:] for d in ds])
        for t, x in enumerate(lo + hi):
            for c in range(ncl):
                slab_ref[j, pl.ds(c * SSM_T + t, nb, stride=pitch), :] = x[c * nb:(c + 1) * nb]


def _proj_kernel(x_ref, g_ref, w_ref, qn_ref, kn_ref, q_ref, k_ref, v_ref, ug_ref, kf_ref, vf_ref, slab_ref,
                 *, first_tail_block):
    i = pl.program_id(0)
    nb, tb, d = x_ref.shape
    d_att = N_HEADS * HEAD_DIM
    pitch, _ = _regroup_geometry(nb, tb)
    h = _rms(x_ref[...].reshape(nb * tb, d), g_ref[...]).astype(BF16)
    q = _dot(h, w_ref[:, 0:d_att])
    k = _dot(h, w_ref[:, d_att:2 * d_att])
    v = _dot(h, w_ref[:, 2 * d_att:3 * d_att])
    u = _dot(h, w_ref[:, 3 * d_att:])
    for j in range(slab_ref.shape[0]):
        for b in range(nb):
            slab_ref[j, b * pitch:b * pitch + tb, :] = u[b * tb:(b + 1) * tb, j * LANES:(j + 1) * LANES]
    _tokens_to_groups(slab_ref, ug_ref, nb=nb, tb=tb)
    qg = qn_ref[...] * (HEAD_DIM ** -0.5)
    kg = kn_ref[...]
    qs, ks = [], []
    for hd in range(N_HEADS):
        sl = slice(hd * HEAD_DIM, (hd + 1) * HEAD_DIM)
        qs.append(_rms(q[:, sl], qg))
        ks.append(_rms(k[:, sl], kg))
    qn = jnp.concatenate(qs, axis=1)
    kn = jnp.concatenate(ks, axis=1)
    for b in range(nb):
        rows = slice(b * tb, (b + 1) * tb)
        q_ref[b] = qn[rows].astype(BF16)
        k_ref[b] = kn[rows].astype(BF16)
        v_ref[b] = v[rows].astype(BF16)

    @pl.when(i >= first_tail_block)
    def _():
        for b in range(nb):
            rows = slice(b * tb, (b + 1) * tb)
            for hd in range(N_HEADS):
                head_rows = pl.ds(hd, tb, stride=N_HEADS)
                kf_ref[b, head_rows, :] = ks[hd][rows]
                vf_ref[b, head_rows, :] = v[rows, hd * HEAD_DIM:(hd + 1) * HEAD_DIM]


def _proj(x, g, w, qn, kn, *, tb):
    nb, seq, d = x.shape
    d_att = N_HEADS * HEAD_DIM
    d_ssm = w.shape[1] - 3 * d_att
    n_g = d_ssm // SSM_GROUP
    n_blocks = seq // tb
    tail = min(ATT_LEFT, seq)
    first_tail_block = n_blocks - tail // tb
    pitch, ncl = _regroup_geometry(nb, tb)
    tok = lambda i: (0, i, 0)
    const = lambda i: (0, 0)
    tail_map = lambda i: (0, jnp.maximum(i - first_tail_block, 0), 0)
    return pl.pallas_call(
        functools.partial(_proj_kernel, first_tail_block=first_tail_block),
        grid=(n_blocks,),
        in_specs=[
            pl.BlockSpec((nb, tb, d), tok),
            pl.BlockSpec((1, d), const),
            pl.BlockSpec(w.shape, const, pipeline_mode=pl.Buffered(1)),
            pl.BlockSpec((1, HEAD_DIM), const),
            pl.BlockSpec((1, HEAD_DIM), const),
        ],
        out_specs=[
            pl.BlockSpec((nb, tb, d_att), tok),
            pl.BlockSpec((nb, tb, d_att), tok),
            pl.BlockSpec((nb, tb, d_att), tok),
            pl.BlockSpec((n_g, ncl * nb, SSM_W), tok),
            pl.BlockSpec((nb, tb * N_HEADS, HEAD_DIM), tail_map),
            pl.BlockSpec((nb, tb * N_HEADS, HEAD_DIM), tail_map),
        ],
        out_shape=[
            jax.ShapeDtypeStruct((nb, seq, d_att), BF16),
            jax.ShapeDtypeStruct((nb, seq, d_att), BF16),
            jax.ShapeDtypeStruct((nb, seq, d_att), BF16),
            jax.ShapeDtypeStruct((n_g, (seq // SSM_T) * nb, SSM_W), BF16),
            jax.ShapeDtypeStruct((nb, tail * N_HEADS, HEAD_DIM), F32),
            jax.ShapeDtypeStruct((nb, tail * N_HEADS, HEAD_DIM), F32),
        ],
        scratch_shapes=[pltpu.VMEM((d_ssm // LANES, nb * pitch, LANES), F32)],
        compiler_params=pltpu.CompilerParams(
            dimension_semantics=("arbitrary",), vmem_limit_bytes=VMEM_LIMIT_V7X),
        name="proj",
    )(x, g, w, qn, kn)


def _softmax_pv(s, v):
    m = jnp.max(s, axis=-1, keepdims=True)
    p = jnp.exp(s - m)
    l = jnp.sum(p, axis=-1, keepdims=True)
    return _dot(p.astype(BF16), v) / l


def _qk(q, k):
    return lax.dot_general(q, k, (((1,), (1,)), ((), ())), preferred_element_type=F32)


def _attn_prompt_kernel(q_ref, k_ref, v_ref, bias_ref, o_ref, *, n_blocks):
    def block(q0, k0, kw):
        q = q_ref[pl.ds(q0, ATT_QB), :]
        k = k_ref[pl.ds(k0, kw), :]
        v = v_ref[pl.ds(k0, kw), :]
        s = _qk(q, k) + bias_ref[:, ATT_KW - kw:]
        o_ref[pl.ds(q0, ATT_QB), :] = _softmax_pv(s, v).astype(BF16)

    lead = ATT_LEFT // ATT_QB
    for i in range(min(lead, n_blocks)):
        block(i * ATT_QB, 0, (i + 1) * ATT_QB)

    def body(i, carry):
        q0 = pl.multiple_of(i * ATT_QB, ATT_QB)
        block(q0, pl.multiple_of(q0 - ATT_LEFT, ATT_QB), ATT_KW)
        return carry

    lax.fori_loop(lead, n_blocks, body, 0, unroll=2)


def _attn_prompt(q, k, v, bias):
    batch, seq, d_att = q.shape
    blk = pl.BlockSpec((None, seq, HEAD_DIM), lambda b, h: (b, 0, h))
    return pl.pallas_call(
        functools.partial(_attn_prompt_kernel, n_blocks=seq // ATT_QB),
        grid=(batch, N_HEADS),
        in_specs=[blk, blk, blk, pl.BlockSpec((None, ATT_QB, ATT_KW), lambda b, h: (h, 0, 0))],
        out_specs=blk,
        out_shape=jax.ShapeDtypeStruct((batch, seq, d_att), BF16),
        compiler_params=pltpu.CompilerParams(
            dimension_semantics=("parallel", "parallel"), vmem_limit_bytes=VMEM_LIMIT_V7X),
        name="attn_prompt",
    )(q, k, v, bias)


def _attn_sample_kernel(q_ref, kn_ref, vn_ref, kc_ref, vc_ref, bias_ref, o_ref, ko_ref, vo_ref, *, w_cache, seq):
    def head_rows(ref, n, hd):
        return ref[pl.ds(hd, n, stride=N_HEADS), :].astype(BF16)

    for hd in range(N_HEADS):
        sl = slice(hd * HEAD_DIM, (hd + 1) * HEAD_DIM)
        q = q_ref[:, sl]
        s1 = _qk(q, head_rows(kc_ref, w_cache, hd)) + bias_ref[hd, :, :w_cache]
        s2 = _qk(q, head_rows(kn_ref, seq, hd)) + bias_ref[hd, :, w_cache:]
        m = jnp.maximum(jnp.max(s1, axis=-1, keepdims=True), jnp.max(s2, axis=-1, keepdims=True))
        p1 = jnp.exp(s1 - m)
        p2 = jnp.exp(s2 - m)
        l = jnp.sum(p1, axis=-1, keepdims=True) + jnp.sum(p2, axis=-1, keepdims=True)
        o = (_dot(p1.astype(BF16), head_rows(vc_ref, w_cache, hd))
             + _dot(p2.astype(BF16), head_rows(vn_ref, seq, hd)))
        o_ref[:, sl] = (o / l).astype(BF16)

    keep = (w_cache - seq) * N_HEADS
    for new_ref, old_ref, out_ref in ((kn_ref, kc_ref, ko_ref), (vn_ref, vc_ref, vo_ref)):
        out_ref[:keep, :] = old_ref[seq * N_HEADS:, :]
        out_ref[keep:, :] = new_ref[...]


def _attn_sample(q, kn, vn, kc, vc, bias):
    batch, seq, d_att = q.shape
    w_cache = kc.shape[1] // N_HEADS
    assert seq <= w_cache
    tok = pl.BlockSpec((None, seq, d_att), lambda b: (b, 0, 0))
    new = pl.BlockSpec((None, seq * N_HEADS, HEAD_DIM), lambda b: (b, 0, 0))
    cache = pl.BlockSpec((None, w_cache * N_HEADS, HEAD_DIM), lambda b: (b, 0, 0))
    return pl.pallas_call(
        functools.partial(_attn_sample_kernel, w_cache=w_cache, seq=seq),
        grid=(batch,),
        in_specs=[tok, new, new, cache, cache, pl.BlockSpec(bias.shape, lambda b: (0, 0, 0))],
        out_specs=[tok, cache, cache],
        out_shape=[jax.ShapeDtypeStruct((batch, seq, d_att), BF16),
                   jax.ShapeDtypeStruct(kc.shape, F32), jax.ShapeDtypeStruct(vc.shape, F32)],
        compiler_params=pltpu.CompilerParams(
            dimension_semantics=("parallel",), vmem_limit_bytes=VMEM_LIMIT_V7X),
        name="attn_sample",
    )(q, kn, vn, kc, vc, bias)


def _ssm_kernel(u_ref, km_ref, bm_ref, cm_ref, a_ref, s0_ref, y_ref, sf_ref, sl_ref, sp_ref,
                *, rows, n_chunks, groups):
    half = 2 * SSM_STATE
    for gi in range(groups):
        sl_ref[gi] = _dot(u_ref[gi], bm_ref[gi])

    coef = []
    for gi in range(groups):
        coef.append(tuple(jnp.broadcast_to(a_ref[gi, r:r + 1, :], (rows, half)) for r in range(3)))

    def body(c, carry):
        r0 = pl.multiple_of(c * rows, rows)
        new = []
        for gi in range(groups):
            s, sw = carry[gi]
            a1, a2, a2w = coef[gi]
            sp_ref[gi, pl.ds(r0, rows), :] = s
            loc = sl_ref[gi, pl.ds(r0, rows), :]
            new.append((a1 * s + a2 * sw + loc[:, :half], a1 * sw + a2w * s + loc[:, half:]))
        return tuple(new)

    init = tuple((s0_ref[gi, 0], s0_ref[gi, 1]) for gi in range(groups))
    last = lax.fori_loop(0, n_chunks, body, init)
    for gi in range(groups):
        sf_ref[gi] = last[gi][0]
        y = _dot(u_ref[gi], km_ref[gi]) + _dot(sp_ref[gi].astype(BF16), cm_ref[gi])
        y_ref[gi] = jax.nn.gelu(y).astype(BF16)


def _ssm(ug, km, bm, cm, acoef, s0, *, rows, groups):
    n_g, n_rows, _ = ug.shape
    n_chunks = n_rows // rows
    half = 2 * SSM_STATE
    g3 = lambda g: (g, 0, 0)
    return pl.pallas_call(
        functools.partial(_ssm_kernel, rows=rows, n_chunks=n_chunks, groups=groups),
        grid=(n_g // groups,),
        in_specs=[
            pl.BlockSpec((groups, n_rows, SSM_W), g3),
            pl.BlockSpec((groups, SSM_W, SSM_W), g3),
            pl.BlockSpec((groups, SSM_W, 2 * half), g3),
            pl.BlockSpec((groups, half, SSM_W), g3),
            pl.BlockSpec((groups, 3, half), g3),
            pl.BlockSpec((groups, 2, rows, half), lambda g: (g, 0, 0, 0)),
        ],
        out_specs=[
            pl.BlockSpec((groups, n_rows, SSM_W), g3),
            pl.BlockSpec((groups, rows, half), g3),
        ],
        out_shape=[
            jax.ShapeDtypeStruct((n_g, n_rows, SSM_W), BF16),
            jax.ShapeDtypeStruct((n_g, rows, half), F32),
        ],
        scratch_shapes=[
            pltpu.VMEM((groups, n_rows, 2 * half), F32),
            pltpu.VMEM((groups, n_rows, half), F32),
        ],
        compiler_params=pltpu.CompilerParams(
            dimension_semantics=("parallel",), vmem_limit_bytes=VMEM_LIMIT_V7X),
        name="ssm",
    )(ug, km, bm, cm, acoef, s0)


def _mix_kernel(att_ref, yg_ref, x_ref, wglu_ref, bglu_ref, ga_ref, gs_ref, wout_ref, o_ref, slab_ref, y_ref):
    nb, tb, d = x_ref.shape
    d_att = att_ref.shape[2]
    d_ssm = y_ref.shape[1]
    pitch, _ = _regroup_geometry(nb, tb)
    _groups_to_tokens(yg_ref, slab_ref, nb=nb, tb=tb)
    for j in range(slab_ref.shape[0]):
        for b in range(nb):
            y_ref[b * tb:(b + 1) * tb, j * LANES:(j + 1) * LANES] = slab_ref[j, b * pitch:b * pitch + tb, :].astype(BF16)
    glu = _dot(y_ref[...], wglu_ref[...]) + bglu_ref[...]
    ssm_out = glu[:, :d_ssm] * jax.nn.sigmoid(glu[:, d_ssm:])
    mix_s = _rms(ssm_out, gs_ref[...]).astype(BF16)
    att = att_ref[...].reshape(nb * tb, d_att).astype(F32)
    mix_a = _rms(att, ga_ref[...]).astype(BF16)
    o = x_ref[...].reshape(nb * tb, d) + _dot(mix_a, wout_ref[:d_att, :]) + _dot(mix_s, wout_ref[d_att:, :])
    o_ref[...] = o.reshape(nb, tb, d)


def _mix(att, yg, x, wglu, bglu, ga, gs, wout, *, tb):
    nb, seq, d = x.shape
    d_att = att.shape[2]
    n_g = yg.shape[0]
    d_ssm = n_g * SSM_GROUP
    pitch, ncl = _regroup_geometry(nb, tb)
    tok = lambda i: (0, i, 0)
    const = lambda i: (0, 0)
    once = lambda a: pl.BlockSpec(a.shape, const, pipeline_mode=pl.Buffered(1))
    return pl.pallas_call(
        _mix_kernel,
        grid=(seq // tb,),
        in_specs=[
            pl.BlockSpec((nb, tb, d_att), tok),
            pl.BlockSpec((n_g, ncl * nb, SSM_W), tok),
            pl.BlockSpec((nb, tb, d), tok),
            once(wglu), once(bglu), once(ga), once(gs), once(wout),
        ],
        out_specs=pl.BlockSpec((nb, tb, d), tok),
        out_shape=jax.ShapeDtypeStruct((nb, seq, d), F32),
        scratch_shapes=[pltpu.VMEM((d_ssm // LANES, nb * pitch, LANES), F32),
                        pltpu.VMEM((nb * tb, d_ssm), BF16)],
        compiler_params=pltpu.CompilerParams(
            dimension_semantics=("parallel",), vmem_limit_bytes=VMEM_LIMIT_V7X),
        name="mix",
    )(att, yg, x, wglu, bglu, ga, gs, wout)


def _ssm_matrices(lam_re, lam_im, log_dt, b_re, b_im, c_re, c_im, d_skip):
    hp = lax.Precision.HIGHEST
    n_g = lam_re.shape[0]
    dt = jnp.exp(log_dt)[:, None]
    n = jnp.arange(SSM_T + 1, dtype=F32)[:, None, None]
    mag = jnp.exp(lam_re * dt * n)
    ang = lam_im * dt * n
    pw_re, pw_im = mag * jnp.cos(ang), mag * jnp.sin(ang)
    x, y = pw_re[1] - 1.0, pw_im[1]
    den = lam_re * lam_re + lam_im * lam_im
    z_re, z_im = (x * lam_re + y * lam_im) / den, (y * lam_re - x * lam_im) / den
    bb_re = z_re[..., None] * b_re - z_im[..., None] * b_im
    bb_im = z_re[..., None] * b_im + z_im[..., None] * b_re
    pb_re = pw_re[..., None] * bb_re - pw_im[..., None] * bb_im
    pb_im = pw_re[..., None] * bb_im + pw_im[..., None] * bb_re
    kd = (jnp.einsum('gop,dgpi->goid', c_re, pb_re[:SSM_T], precision=hp)
          - jnp.einsum('gop,dgpi->goid', c_im, pb_im[:SSM_T], precision=hp))
    lagrow = jnp.pad(kd, ((0, 0), (0, 0), (0, 0), (SSM_T, 1)))
    skew = jnp.tile(lagrow, (1, 1, 1, SSM_T))[..., :2 * SSM_T * SSM_T]
    kt = skew.reshape(n_g, SSM_GROUP, SSM_GROUP, SSM_T, 2 * SSM_T)[..., SSM_T:]
    kt = kt.transpose(0, 3, 2, 4, 1)
    eye_t = jnp.eye(SSM_T, dtype=F32)[None, :, None, :, None]
    eye_h = jnp.eye(SSM_GROUP, dtype=F32)[None, None, :, None, :]
    kt = kt + eye_t * eye_h * d_skip[:, None, :, None, None]
    km = kt.reshape(n_g, SSM_W, SSM_W)
    inj_re = pb_re[:SSM_T][::-1].transpose(1, 0, 3, 2).reshape(n_g, SSM_W, SSM_STATE)
    inj_im = pb_im[:SSM_T][::-1].transpose(1, 0, 3, 2).reshape(n_g, SSM_W, SSM_STATE)
    bm = jnp.concatenate([inj_re, inj_im, inj_im, inj_re], axis=-1)
    cp_re = c_re[None] * pw_re[1:, :, None, :] - c_im[None] * pw_im[1:, :, None, :]
    cp_im = c_re[None] * pw_im[1:, :, None, :] + c_im[None] * pw_re[1:, :, None, :]
    out_re = cp_re.transpose(1, 3, 0, 2).reshape(n_g, SSM_STATE, SSM_W)
    out_im = -cp_im.transpose(1, 3, 0, 2).reshape(n_g, SSM_STATE, SSM_W)
    cm = jnp.concatenate([out_re, out_im], axis=1)
    ar, ai = pw_re[SSM_T], pw_im[SSM_T]
    acoef = jnp.stack([jnp.concatenate([ar, ar], -1), jnp.concatenate([-ai, ai], -1),
                       jnp.concatenate([ai, -ai], -1)], axis=1)
    return km.astype(BF16), bm.astype(BF16), cm.astype(BF16), acoef


def _bias_table(rel_bias):
    n_heads = rel_bias.shape[0]
    ext = ATT_QB + ATT_KW
    n_edge = ATT_LEFT - REL_CLIP + 1
    assert ATT_KW - n_edge == 2 * REL_CLIP - 1 and ATT_LEFT >= REL_CLIP
    far = rel_bias[:, 2 * REL_CLIP:]
    row = jnp.concatenate([jnp.broadcast_to(far, (n_heads, n_edge)), rel_bias[:, 1:2 * REL_CLIP][:, ::-1],
                           jnp.broadcast_to(far, (n_heads, ATT_QB))], axis=1)
    skew = jnp.tile(row, (1, ATT_QB))[:, :ATT_QB * (ext - 1)].reshape(n_heads, ATT_QB, ext - 1)
    return skew[:, :, :ATT_KW].astype(F32)


def _band_mask():
    r = jnp.arange(ATT_QB)[:, None] // CHUNK
    j = jnp.arange(ATT_KW)[None, :] // CHUNK
    return (j >= r) & (j <= r + LEFT_CHUNKS)


def _state_rows(s_re, s_im):
    s = jnp.concatenate([s_re, s_im], -1).transpose(1, 0, 2)
    sw = jnp.concatenate([s_im, s_re], -1).transpose(1, 0, 2)
    return jnp.stack([s, sw], axis=1)


def _pad_bf16(w, axis, mult):
    pads = [(0, 0), (0, 0)]
    pads[axis] = (0, (-w.shape[axis]) % mult)
    return jnp.pad(w.astype(BF16), pads)


def _stream(x, p, *, tm, tb, ssm_groups, cache=None):
    batch, seq, d = x.shape
    ffn = functools.partial(_ffn, tm=tm, tf=p['tf'])
    x1 = ffn(x.reshape(batch * seq, d), p['g_ffn1'], p['ffn1_wg'], p['ffn1_wu'], p['ffn1_wd'], p['g_final'],
             final_norm=False).reshape(batch, seq, d)
    q, k, v, ug, kf, vf = _proj(x1, p['g_mix'], p['w_in'], p['q_norm'], p['k_norm'], tb=tb)
    n_g = ug.shape[0]
    if cache is None:
        att = _attn_prompt(q, k, v, p['bias_prompt'])
        s0 = jnp.zeros((n_g, 2, batch, 2 * SSM_STATE), F32)
    else:
        ck, cv, s_re, s_im = cache
        w_cache = ck.shape[1]
        assert w_cache == ATT_LEFT and seq <= ATT_QB
        att, kf, vf = _attn_sample(q, kf, vf, ck.reshape(batch, w_cache * N_HEADS, HEAD_DIM),
                                   cv.reshape(batch, w_cache * N_HEADS, HEAD_DIM),
                                   p['bias_table'][:, :seq, :w_cache + seq])
        s0 = _state_rows(s_re, s_im)
    yg, sf = _ssm(ug, p['km'], p['bm'], p['cm'], p['acoef'], s0, rows=batch, groups=ssm_groups)
    x2 = _mix(att, yg, x1, p['w_glu'], p['b_glu'], p['g_att'], p['g_ssm'], p['w_out'], tb=tb)
    y = ffn(x2.reshape(batch * seq, d), p['g_ffn2'], p['ffn2_wg'], p['ffn2_wu'], p['ffn2_wd'], p['g_final'],
            final_norm=True).reshape(batch, seq, d)
    sf = sf.transpose(1, 0, 2)
    return y, kf, vf, sf[..., :SSM_STATE], sf[..., SSM_STATE:]


def kernel(x_prompt, x_sample, cache_attn_k, cache_attn_v, state_ssm_re, state_ssm_im, norm_ffn1, ffn1_w_gate, ffn1_w_up, ffn1_w_down, norm_mix, w_in, q_norm, k_norm, rel_bias, ssm_lambda_re, ssm_lambda_im, ssm_log_dt, ssm_b_re, ssm_b_im, ssm_c_re, ssm_c_im, ssm_d, w_glu, b_glu, norm_att_out, norm_ssm_out, w_out, norm_ffn2, ffn2_w_gate, ffn2_w_up, ffn2_w_down, norm_final):
    depth = norm_ffn1.shape[0]
    bs, ls, _ = x_sample.shape
    tf = 256
    yp, ys = x_prompt, x_sample
    outs = [[] for _ in range(8)]
    for l in range(depth):
        km, bm, cm, acoef = _ssm_matrices(ssm_lambda_re[l], ssm_lambda_im[l], ssm_log_dt[l], ssm_b_re[l],
                                          ssm_b_im[l], ssm_c_re[l], ssm_c_im[l], ssm_d[l])
        row = lambda a: a[l][None, :]
        half_down = lambda w: _pad_bf16(0.5 * w, 0, tf)
        bias_table = _bias_table(rel_bias[l])
        p = dict(
            tf=tf,
            g_ffn1=row(norm_ffn1), g_mix=row(norm_mix), g_att=row(norm_att_out), g_ssm=row(norm_ssm_out),
            g_ffn2=row(norm_ffn2), g_final=row(norm_final), q_norm=row(q_norm), k_norm=row(k_norm),
            ffn1_wg=_pad_bf16(ffn1_w_gate[l], 1, tf), ffn1_wu=_pad_bf16(ffn1_w_up[l], 1, tf),
            ffn1_wd=half_down(ffn1_w_down[l]),
            ffn2_wg=_pad_bf16(ffn2_w_gate[l], 1, tf), ffn2_wu=_pad_bf16(ffn2_w_up[l], 1, tf),
            ffn2_wd=half_down(ffn2_w_down[l]),
            w_in=w_in[l].astype(BF16), w_glu=w_glu[l].astype(BF16), b_glu=row(b_glu), w_out=w_out[l].astype(BF16),
            bias_table=bias_table, bias_prompt=jnp.where(_band_mask()[None], bias_table, NEG_INF),
            km=km, bm=bm, cm=cm, acoef=acoef,
        )
        yp, kp, vp, rp, ip = _stream(yp, p, tm=1024, tb=CHUNK, ssm_groups=4)
        ys, kd, vd, rd, idd = _stream(ys, p, tm=bs * ls, tb=ls, ssm_groups=8,
                                      cache=(cache_attn_k[l], cache_attn_v[l], state_ssm_re[l], state_ssm_im[l]))
        as_cache = lambda a: a.reshape(a.shape[0], -1, N_HEADS, HEAD_DIM)
        for lst, val in zip(outs, (as_cache(kp), as_cache(vp), rp, ip, as_cache(kd), as_cache(vd), rd, idd)):
            lst.append(val)
    return (yp, ys) + tuple(jnp.stack(o) for o in outs)
```

```python
import functools

import jax
import jax.numpy as jnp
from jax import lax
from jax.experimental import pallas as pl
from jax.experimental.pallas import tpu as pltpu

EPS = 1e-6
NEG_INF = -1e30
CHUNK = 64
LEFT_CHUNKS = 8
ATT_LEFT = LEFT_CHUNKS * CHUNK
REL_CLIP = 256
N_HEADS = 8
HEAD_DIM = 128
SSM_GROUP = 16
SSM_STATE = 64
LANES = 128
SSM_T = 16
SSM_W = SSM_T * SSM_GROUP
GROUPS_PER_TILE = LANES // SSM_GROUP
ATT_QB = 256
ATT_KW = ATT_QB + ATT_LEFT
REGROUP_PAD = 8
FF_PAD = 512
VMEM_LIMIT_V7X = 56 * 1024 * 1024

BF16 = jnp.bfloat16
F32 = jnp.float32


def _dot(a, b):
    return jnp.dot(a, b, preferred_element_type=F32)


def _rms(x, g):
    return x * lax.rsqrt(jnp.mean(x * x, axis=-1, keepdims=True) + EPS) * g


def _ffn_kernel(x_ref, g_ref, wg_ref, wu_ref, wd_ref, gf_ref, o_ref, xn_ref, *, final_norm):
    j = pl.program_id(1)

    @pl.when(j == 0)
    def _():
        x = x_ref[...]
        xn_ref[...] = _rms(x, g_ref[...]).astype(BF16)
        o_ref[...] = x

    xn = xn_ref[...]
    a = _dot(xn, wg_ref[...])
    b = _dot(xn, wu_ref[...])
    h = (a * jax.nn.sigmoid(a) * b).astype(BF16)
    o_ref[...] += _dot(h, wd_ref[...])

    if final_norm:
        @pl.when(j == pl.num_programs(1) - 1)
        def _():
            o_ref[...] = _rms(o_ref[...], gf_ref[...])


def _ffn(x, g, wg, wu, wd_half, gf, *, tm, tf, final_norm):
    n, d = x.shape
    fpad = wg.shape[1]
    return pl.pallas_call(
        functools.partial(_ffn_kernel, final_norm=final_norm),
        grid=(n // tm, fpad // tf),
        in_specs=[
            pl.BlockSpec((tm, d), lambda i, j: (i, 0)),
            pl.BlockSpec((1, d), lambda i, j: (0, 0)),
            pl.BlockSpec((d, tf), lambda i, j: (0, j)),
            pl.BlockSpec((d, tf), lambda i, j: (0, j)),
            pl.BlockSpec((tf, d), lambda i, j: (j, 0)),
            pl.BlockSpec((1, d), lambda i, j: (0, 0)),
        ],
        out_specs=pl.BlockSpec((tm, d), lambda i, j: (i, 0)),
        out_shape=jax.ShapeDtypeStruct((n, d), F32),
        scratch_shapes=[pltpu.VMEM((tm, d), BF16)],
        compiler_params=pltpu.CompilerParams(
            dimension_semantics=("parallel", "arbitrary"), vmem_limit_bytes=VMEM_LIMIT_V7X),
        name="ffn",
    )(x, g, wg, wu, wd_half, gf)


def _granule_transpose(vs):
    gran = lax.broadcasted_iota(jnp.int32, vs[0].shape, 1) // SSM_GROUP
    for s in (4, 2, 1):
        upper = (gran & s) != 0
        new = list(vs)
        for a in range(GROUPS_PER_TILE):
            if a & s == 0:
                lo, hi = vs[a], vs[a + s]
                new[a] = jnp.where(upper, pltpu.roll(hi, s * SSM_GROUP, 1), lo)
                new[a + s] = jnp.where(upper, hi, pltpu.roll(lo, LANES - s * SSM_GROUP, 1))
        vs = new
    return vs


def _regroup_geometry(nb, tb):
    return tb + REGROUP_PAD, tb // SSM_T


def _tokens_to_groups(slab_ref, ug_ref, *, nb, tb):
    pitch, ncl = _regroup_geometry(nb, tb)
    for j in range(slab_ref.shape[0]):
        xs = [jnp.concatenate([slab_ref[j, pl.ds(c * SSM_T + t, nb, stride=pitch), :] for c in range(ncl)], axis=0)
              for t in range(SSM_T)]
        lo = _granule_transpose(xs[:GROUPS_PER_TILE])
        hi = _granule_transpose(xs[GROUPS_PER_TILE:])
        for gl in range(GROUPS_PER_TILE):
            ug_ref[j * GROUPS_PER_TILE + gl] = jnp.concatenate([lo[gl], hi[gl]], axis=1).astype(ug_ref.dtype)


def _groups_to_tokens(yg_ref, slab_ref, *, nb, tb):
    pitch, ncl = _regroup_geometry(nb, tb)
    for j in range(slab_ref.shape[0]):
        ds = [yg_ref[j * GROUPS_PER_TILE + gl].astype(F32) for gl in range(GROUPS_PER_TILE)]
        lo = _granule_transpose([d[:, :LANES] for d in ds])
        hi = _granule_transpose([d[:, LANES:] for d in ds])
        for t, x in enumerate(lo + hi):
            for c in range(ncl):
                slab_ref[j, pl.ds(c * SSM_T + t, nb, stride=pitch), :] = x[c * nb:(c + 1) * nb]


def _proj_kernel(x_ref, g_ref, w_ref, qn_ref, kn_ref, q_ref, k_ref, v_ref, ug_ref, kf_ref, vf_ref, slab_ref,
                 *, first_tail_block):
    i = pl.program_id(0)
    nb, tb, d = x_ref.shape
    d_att = N_HEADS * HEAD_DIM
    pitch, _ = _regroup_geometry(nb, tb)
    h = _rms(x_ref[...].reshape(nb * tb, d), g_ref[...]).astype(BF16)
    q = _dot(h, w_ref[:, 0:d_att])
    k = _dot(h, w_ref[:, d_att:2 * d_att])
    v = _dot(h, w_ref[:, 2 * d_att:3 * d_att])
    u = _dot(h, w_ref[:, 3 * d_att:])
    for j in range(slab_ref.shape[0]):
        for b in range(nb):
            slab_ref[j, b * pitch:b * pitch + tb, :] = u[b * tb:(b + 1) * tb, j * LANES:(j + 1) * LANES]
    _tokens_to_groups(slab_ref, ug_ref, nb=nb, tb=tb)
    qg = qn_ref[...] * (HEAD_DIM ** -0.5)
    kg = kn_ref[...]
    qs, ks = [], []
    for hd in range(N_HEADS):
        sl = slice(hd * HEAD_DIM, (hd + 1) * HEAD_DIM)
        qs.append(_rms(q[:, sl], qg))
        ks.append(_rms(k[:, sl], kg))
    qn = jnp.concatenate(qs, axis=1)
    kn = jnp.concatenate(ks, axis=1)
    for b in range(nb):
        rows = slice(b * tb, (b + 1) * tb)
        q_ref[b] = qn[rows].astype(BF16)
        k_ref[b] = kn[rows].astype(BF16)
        v_ref[b] = v[rows].astype(BF16)

    @pl.when(i >= first_tail_block)
    def _():
        for b in range(nb):
            rows = slice(b * tb, (b + 1) * tb)
            for hd in range(N_HEADS):
                head_rows = pl.ds(hd, tb, stride=N_HEADS)
                kf_ref[b, head_rows, :] = ks[hd][rows]
                vf_ref[b, head_rows, :] = v[rows, hd * HEAD_DIM:(hd + 1) * HEAD_DIM]


def _proj(x, g, w, qn, kn, *, tb):
    nb, seq, d = x.shape
    d_att = N_HEADS * HEAD_DIM
    d_ssm = w.shape[1] - 3 * d_att
    n_g = d_ssm // SSM_GROUP
    n_blocks = seq // tb
    tail = min(ATT_LEFT, seq)
    first_tail_block = n_blocks - tail // tb
    pitch, ncl = _regroup_geometry(nb, tb)
    tok = lambda i: (0, i, 0)
    const = lambda i: (0, 0)
    tail_map = lambda i: (0, jnp.maximum(i - first_tail_block, 0), 0)
    return pl.pallas_call(
        functools.partial(_proj_kernel, first_tail_block=first_tail_block),
        grid=(n_blocks,),
        in_specs=[
            pl.BlockSpec((nb, tb, d), tok),
            pl.BlockSpec((1, d), const),
            pl.BlockSpec(w.shape, const, pipeline_mode=pl.Buffered(1)),
            pl.BlockSpec((1, HEAD_DIM), const),
            pl.BlockSpec((1, HEAD_DIM), const),
        ],
        out_specs=[
            pl.BlockSpec((nb, tb, d_att), tok),
            pl.BlockSpec((nb, tb, d_att), tok),
            pl.BlockSpec((nb, tb, d_att), tok),
            pl.BlockSpec((n_g, ncl * nb, SSM_W), tok),
            pl.BlockSpec((nb, tb * N_HEADS, HEAD_DIM), tail_map),
            pl.BlockSpec((nb, tb * N_HEADS, HEAD_DIM), tail_map),
        ],
        out_shape=[
            jax.ShapeDtypeStruct((nb, seq, d_att), BF16),
            jax.ShapeDtypeStruct((nb, seq, d_att), BF16),
            jax.ShapeDtypeStruct((nb, seq, d_att), BF16),
            jax.ShapeDtypeStruct((n_g, (seq // SSM_T) * nb, SSM_W), BF16),
            jax.ShapeDtypeStruct((nb, tail * N_HEADS, HEAD_DIM), F32),
            jax.ShapeDtypeStruct((nb, tail * N_HEADS, HEAD_DIM), F32),
        ],
        scratch_shapes=[pltpu.VMEM((d_ssm // LANES, nb * pitch, LANES), F32)],
        compiler_params=pltpu.CompilerParams(
            dimension_semantics=("arbitrary",), vmem_limit_bytes=VMEM_LIMIT_V7X),
        name="proj",
    )(x, g, w, qn, kn)


def _softmax_pv(s, v):
    m = jnp.max(s, axis=-1, keepdims=True)
    p = jnp.exp(s - m)
    l = jnp.sum(p, axis=-1, keepdims=True)
    return _dot(p.astype(BF16), v) / l


def _qk(q, k):
    return lax.dot_general(q, k, (((1,), (1,)), ((), ())), preferred_element_type=F32)


def _attn_prompt_kernel(q_ref, k_ref, v_ref, bias_ref, o_ref, s_ref, p_ref, linv_ref, *, n_blocks):
    lead = ATT_LEFT // ATT_QB
    for i in range(min(lead, n_blocks)):
        kw = (i + 1) * ATT_QB
        rows = slice(i * ATT_QB, (i + 1) * ATT_QB)
        s = _qk(q_ref[rows, :], k_ref[0:kw, :]) + bias_ref[:, ATT_KW - kw:]
        o_ref[rows, :] = _softmax_pv(s, v_ref[0:kw, :]).astype(BF16)
    n_full = n_blocks - lead
    if n_full <= 0:
        return

    def start(i):
        return pl.multiple_of(i * ATT_QB, ATT_QB)

    def scores(i, slot):
        i = jnp.minimum(i, n_blocks - 1)
        s_ref[slot] = _qk(q_ref[pl.ds(start(i), ATT_QB), :], k_ref[pl.ds(start(i) - ATT_LEFT, ATT_KW), :])

    def softmax(slot):
        s = s_ref[slot] + bias_ref[...]
        p = jnp.exp(s - jnp.max(s, axis=-1, keepdims=True))
        p_ref[slot] = p.astype(BF16)
        linv_ref[slot] = jnp.broadcast_to(1.0 / jnp.sum(p, axis=-1, keepdims=True), linv_ref.shape[1:])

    def output(i, slot):
        v = v_ref[pl.ds(start(i) - ATT_LEFT, ATT_KW), :]
        o_ref[pl.ds(start(i), ATT_QB), :] = (_dot(p_ref[slot], v) * linv_ref[slot]).astype(BF16)

    def step(i, slot):
        scores(i + 1, 1 - slot)
        output(i - 1, 1 - slot)
        softmax(slot)

    scores(lead, 0)
    scores(lead + 1, 1)
    softmax(0)
    n_pairs = (n_full - 1) // 2

    def pair(r, carry):
        i = lead + 1 + 2 * r
        step(i, 1)
        step(i + 1, 0)
        return carry

    lax.fori_loop(0, n_pairs, pair, 0)
    last_slot = 0
    if (n_full - 1) % 2:
        step(n_blocks - 1, 1)
        last_slot = 1
    output(n_blocks - 1, last_slot)


def _attn_prompt(q, k, v, bias):
    batch, seq, d_att = q.shape
    blk = pl.BlockSpec((None, seq, HEAD_DIM), lambda b, h: (b, 0, h))
    return pl.pallas_call(
        functools.partial(_attn_prompt_kernel, n_blocks=seq // ATT_QB),
        grid=(batch, N_HEADS),
        in_specs=[blk, blk, blk, pl.BlockSpec((None, ATT_QB, ATT_KW), lambda b, h: (h, 0, 0))],
        out_specs=blk,
        out_shape=jax.ShapeDtypeStruct((batch, seq, d_att), BF16),
        scratch_shapes=[pltpu.VMEM((2, ATT_QB, ATT_KW), F32), pltpu.VMEM((2, ATT_QB, ATT_KW), BF16),
                        pltpu.VMEM((2, ATT_QB, HEAD_DIM), F32)],
        compiler_params=pltpu.CompilerParams(
            dimension_semantics=("parallel", "parallel"), vmem_limit_bytes=VMEM_LIMIT_V7X),
        name="attn_prompt",
    )(q, k, v, bias)


def _attn_sample_kernel(q_ref, kn_ref, vn_ref, kc_ref, vc_ref, bias_ref, o_ref, ko_ref, vo_ref, *, w_cache, seq):
    def head_rows(ref, n, hd):
        return ref[pl.ds(hd, n, stride=N_HEADS), :].astype(BF16)

    for hd in range(N_HEADS):
        sl = slice(hd * HEAD_DIM, (hd + 1) * HEAD_DIM)
        q = q_ref[:, sl]
        s1 = _qk(q, head_rows(kc_ref, w_cache, hd)) + bias_ref[hd, :, :w_cache]
        s2 = _qk(q, head_rows(kn_ref, seq, hd)) + bias_ref[hd, :, w_cache:]
        m = jnp.maximum(jnp.max(s1, axis=-1, keepdims=True), jnp.max(s2, axis=-1, keepdims=True))
        p1 = jnp.exp(s1 - m)
        p2 = jnp.exp(s2 - m)
        l = jnp.sum(p1, axis=-1, keepdims=True) + jnp.sum(p2, axis=-1, keepdims=True)
        o = (_dot(p1.astype(BF16), head_rows(vc_ref, w_cache, hd))
             + _dot(p2.astype(BF16), head_rows(vn_ref, seq, hd)))
        o_ref[:, sl] = (o / l).astype(BF16)

    keep = (w_cache - seq) * N_HEADS
    for new_ref, old_ref, out_ref in ((kn_ref, kc_ref, ko_ref), (vn_ref, vc_ref, vo_ref)):
        out_ref[:keep, :] = old_ref[seq * N_HEADS:, :]
        out_ref[keep:, :] = new_ref[...]


def _attn_sample(q, kn, vn, kc, vc, bias):
    batch, seq, d_att = q.shape
    w_cache = kc.shape[1] // N_HEADS
    assert seq <= w_cache
    tok = pl.BlockSpec((None, seq, d_att), lambda b: (b, 0, 0))
    new = pl.BlockSpec((None, seq * N_HEADS, HEAD_DIM), lambda b: (b, 0, 0))
    cache = pl.BlockSpec((None, w_cache * N_HEADS, HEAD_DIM), lambda b: (b, 0, 0))
    return pl.pallas_call(
        functools.partial(_attn_sample_kernel, w_cache=w_cache, seq=seq),
        grid=(batch,),
        in_specs=[tok, new, new, cache, cache, pl.BlockSpec(bias.shape, lambda b: (0, 0, 0))],
        out_specs=[tok, cache, cache],
        out_shape=[jax.ShapeDtypeStruct((batch, seq, d_att), BF16),
                   jax.ShapeDtypeStruct(kc.shape, F32), jax.ShapeDtypeStruct(vc.shape, F32)],
        compiler_params=pltpu.CompilerParams(
            dimension_semantics=("parallel",), vmem_limit_bytes=VMEM_LIMIT_V7X),
        name="attn_sample",
    )(q, kn, vn, kc, vc, bias)


def _ssm_kernel(u_ref, lag_ref, bm_ref, cm_ref, a_ref, s0_ref, y_ref, sf_ref, sl_ref, sp_ref, km_ref,
                *, rows, n_chunks, groups):
    half = 2 * SSM_STATE
    lane = lax.broadcasted_iota(jnp.int32, (SSM_GROUP, SSM_W), 1)
    for gi in range(groups):
        lag = lag_ref[gi]
        for t_in in range(SSM_T):
            shifted = lag if t_in == 0 else pltpu.roll(lag, t_in * SSM_GROUP, 1)
            km_ref[gi, t_in * SSM_GROUP:(t_in + 1) * SSM_GROUP, :] = jnp.where(
                lane >= t_in * SSM_GROUP, shifted, 0.0).astype(BF16)
        sl_ref[gi] = _dot(u_ref[gi], bm_ref[gi])

    coef = []
    for gi in range(groups):
        coef.append(tuple(jnp.broadcast_to(a_ref[gi, r:r + 1, :], (rows, half)) for r in range(3)))

    def body(c, carry):
        r0 = pl.multiple_of(c * rows, rows)
        new = []
        for gi in range(groups):
            s, sw = carry[gi]
            a1, a2, a2w = coef[gi]
            sp_ref[gi, pl.ds(r0, rows), :] = s
            loc = sl_ref[gi, pl.ds(r0, rows), :]
            new.append((a1 * s + a2 * sw + loc[:, :half], a1 * sw + a2w * s + loc[:, half:]))
        return tuple(new)

    init = tuple((s0_ref[gi, 0], s0_ref[gi, 1]) for gi in range(groups))
    last = lax.fori_loop(0, n_chunks, body, init)
    for gi in range(groups):
        sf_ref[gi] = last[gi][0]
        y = _dot(u_ref[gi], km_ref[gi]) + _dot(sp_ref[gi].astype(BF16), cm_ref[gi])
        y_ref[gi] = jax.nn.gelu(y).astype(BF16)


def _ssm(ug, lagk, bm, cm, acoef, s0, *, rows, groups):
    n_g, n_rows, _ = ug.shape
    n_chunks = n_rows // rows
    half = 2 * SSM_STATE
    g3 = lambda g: (g, 0, 0)
    return pl.pallas_call(
        functools.partial(_ssm_kernel, rows=rows, n_chunks=n_chunks, groups=groups),
        grid=(n_g // groups,),
        in_specs=[
            pl.BlockSpec((groups, n_rows, SSM_W), g3),
            pl.BlockSpec((groups, SSM_GROUP, SSM_W), g3),
            pl.BlockSpec((groups, SSM_W, 2 * half), g3),
            pl.BlockSpec((groups, half, SSM_W), g3),
            pl.BlockSpec((groups, 3, half), g3),
            pl.BlockSpec((groups, 2, rows, half), lambda g: (g, 0, 0, 0)),
        ],
        out_specs=[
            pl.BlockSpec((groups, n_rows, SSM_W), g3),
            pl.BlockSpec((groups, rows, half), g3),
        ],
        out_shape=[
            jax.ShapeDtypeStruct((n_g, n_rows, SSM_W), BF16),
            jax.ShapeDtypeStruct((n_g, rows, half), F32),
        ],
        scratch_shapes=[
            pltpu.VMEM((groups, n_rows, 2 * half), F32),
            pltpu.VMEM((groups, n_rows, half), F32),
            pltpu.VMEM((groups, SSM_W, SSM_W), BF16),
        ],
        compiler_params=pltpu.CompilerParams(
            dimension_semantics=("parallel",), vmem_limit_bytes=VMEM_LIMIT_V7X),
        name="ssm",
    )(ug, lagk, bm, cm, acoef, s0)


def _mix_kernel(att_ref, yg_ref, x_ref, wglu_ref, bglu_ref, ga_ref, gs_ref, wout_ref, o_ref, slab_ref, y_ref):
    nb, tb, d = x_ref.shape
    d_att = att_ref.shape[2]
    d_ssm = y_ref.shape[1]
    pitch, _ = _regroup_geometry(nb, tb)
    _groups_to_tokens(yg_ref, slab_ref, nb=nb, tb=tb)
    for j in range(slab_ref.shape[0]):
        for b in range(nb):
            y_ref[b * tb:(b + 1) * tb, j * LANES:(j + 1) * LANES] = slab_ref[j, b * pitch:b * pitch + tb, :].astype(BF16)
    glu = _dot(y_ref[...], wglu_ref[...]) + bglu_ref[...]
    ssm_out = glu[:, :d_ssm] * jax.nn.sigmoid(glu[:, d_ssm:])
    mix_s = _rms(ssm_out, gs_ref[...]).astype(BF16)
    att = att_ref[...].reshape(nb * tb, d_att).astype(F32)
    mix_a = _rms(att, ga_ref[...]).astype(BF16)
    o = x_ref[...].reshape(nb * tb, d) + _dot(mix_a, wout_ref[:d_att, :]) + _dot(mix_s, wout_ref[d_att:, :])
    o_ref[...] = o.reshape(nb, tb, d)


def _mix(att, yg, x, wglu, bglu, ga, gs, wout, *, tb):
    nb, seq, d = x.shape
    d_att = att.shape[2]
    n_g = yg.shape[0]
    d_ssm = n_g * SSM_GROUP
    pitch, ncl = _regroup_geometry(nb, tb)
    tok = lambda i: (0, i, 0)
    const = lambda i: (0, 0)
    once = lambda a: pl.BlockSpec(a.shape, const, pipeline_mode=pl.Buffered(1))
    return pl.pallas_call(
        _mix_kernel,
        grid=(seq // tb,),
        in_specs=[
            pl.BlockSpec((nb, tb, d_att), tok),
            pl.BlockSpec((n_g, ncl * nb, SSM_W), tok),
            pl.BlockSpec((nb, tb, d), tok),
            once(wglu), once(bglu), once(ga), once(gs), once(wout),
        ],
        out_specs=pl.BlockSpec((nb, tb, d), tok),
        out_shape=jax.ShapeDtypeStruct((nb, seq, d), F32),
        scratch_shapes=[pltpu.VMEM((d_ssm // LANES, nb * pitch, LANES), F32),
                        pltpu.VMEM((nb * tb, d_ssm), BF16)],
        compiler_params=pltpu.CompilerParams(
            dimension_semantics=("parallel",), vmem_limit_bytes=VMEM_LIMIT_V7X),
        name="mix",
    )(att, yg, x, wglu, bglu, ga, gs, wout)


def _ssm_matrices(lam_re, lam_im, log_dt, b_re, b_im, c_re, c_im, d_skip):
    hp = lax.Precision.HIGHEST
    n_g = lam_re.shape[0]
    dt = jnp.exp(log_dt)[:, None]
    n = jnp.arange(SSM_T + 1, dtype=F32)[:, None, None]
    mag = jnp.exp(lam_re * dt * n)
    ang = lam_im * dt * n
    pw_re, pw_im = mag * jnp.cos(ang), mag * jnp.sin(ang)
    x, y = pw_re[1] - 1.0, pw_im[1]
    den = lam_re * lam_re + lam_im * lam_im
    z_re, z_im = (x * lam_re + y * lam_im) / den, (y * lam_re - x * lam_im) / den
    bb_re = z_re[..., None] * b_re - z_im[..., None] * b_im
    bb_im = z_re[..., None] * b_im + z_im[..., None] * b_re
    pb_re = pw_re[..., None] * bb_re - pw_im[..., None] * bb_im
    pb_im = pw_re[..., None] * bb_im + pw_im[..., None] * bb_re
    lagk = (jnp.einsum('gop,dgpi->gido', c_re, pb_re[:SSM_T], precision=hp)
            - jnp.einsum('gop,dgpi->gido', c_im, pb_im[:SSM_T], precision=hp))
    skip = jnp.eye(SSM_GROUP, dtype=F32)[None] * d_skip[:, :, None]
    lagk = lagk.at[:, :, 0, :].add(skip).reshape(n_g, SSM_GROUP, SSM_W)
    inj_re = pb_re[:SSM_T][::-1].transpose(1, 0, 3, 2).reshape(n_g, SSM_W, SSM_STATE)
    inj_im = pb_im[:SSM_T][::-1].transpose(1, 0, 3, 2).reshape(n_g, SSM_W, SSM_STATE)
    bm = jnp.concatenate([inj_re, inj_im, inj_im, inj_re], axis=-1)
    cp_re = c_re[None] * pw_re[1:, :, None, :] - c_im[None] * pw_im[1:, :, None, :]
    cp_im = c_re[None] * pw_im[1:, :, None, :] + c_im[None] * pw_re[1:, :, None, :]
    out_re = cp_re.transpose(1, 3, 0, 2).reshape(n_g, SSM_STATE, SSM_W)
    out_im = -cp_im.transpose(1, 3, 0, 2).reshape(n_g, SSM_STATE, SSM_W)
    cm = jnp.concatenate([out_re, out_im], axis=1)
    ar, ai = pw_re[SSM_T], pw_im[SSM_T]
    acoef = jnp.stack([jnp.concatenate([ar, ar], -1), jnp.concatenate([-ai, ai], -1),
                       jnp.concatenate([ai, -ai], -1)], axis=1)
    return lagk, bm.astype(BF16), cm.astype(BF16), acoef


def _bias_table(rel_bias):
    n_heads = rel_bias.shape[0]
    ext = ATT_QB + ATT_KW
    n_edge = ATT_LEFT - REL_CLIP + 1
    assert ATT_KW - n_edge == 2 * REL_CLIP - 1 and ATT_LEFT >= REL_CLIP
    far = rel_bias[:, 2 * REL_CLIP:]
    row = jnp.concatenate([jnp.broadcast_to(far, (n_heads, n_edge)), rel_bias[:, 1:2 * REL_CLIP][:, ::-1],
                           jnp.broadcast_to(far, (n_heads, ATT_QB))], axis=1)
    skew = jnp.tile(row, (1, ATT_QB))[:, :ATT_QB * (ext - 1)].reshape(n_heads, ATT_QB, ext - 1)
    return skew[:, :, :ATT_KW].astype(F32)


def _band_mask():
    r = jnp.arange(ATT_QB)[:, None] // CHUNK
    j = jnp.arange(ATT_KW)[None, :] // CHUNK
    return (j >= r) & (j <= r + LEFT_CHUNKS)


def _state_rows(s_re, s_im):
    s = jnp.concatenate([s_re, s_im], -1).transpose(1, 0, 2)
    sw = jnp.concatenate([s_im, s_re], -1).transpose(1, 0, 2)
    return jnp.stack([s, sw], axis=1)


def _cast_pad_kernel(*refs, scale):
    n = len(refs) // 2
    for w_ref, o_ref in zip(refs[:n], refs[n:]):
        r, c = w_ref.shape
        w = w_ref[...]
        o_ref[:r, :c] = (w if scale == 1.0 else w * scale).astype(BF16)
        if o_ref.shape[0] > r:
            o_ref[r:, :] = jnp.zeros((o_ref.shape[0] - r, o_ref.shape[1]), BF16)
        if o_ref.shape[1] > c:
            o_ref[:, c:] = jnp.zeros((o_ref.shape[0], o_ref.shape[1] - c), BF16)


def _cast_pad(ws, *, axis, mult, block, scale=1.0):
    r, c = ws[0].shape
    padded = -(-ws[0].shape[axis] // mult) * mult
    if axis == 1:
        in_spec, out_spec = pl.BlockSpec((block, c), lambda i: (i, 0)), pl.BlockSpec((block, padded), lambda i: (i, 0))
        out_shape, steps = jax.ShapeDtypeStruct((r, padded), BF16), r // block
    else:
        in_spec, out_spec = pl.BlockSpec((r, block), lambda i: (0, i)), pl.BlockSpec((padded, block), lambda i: (0, i))
        out_shape, steps = jax.ShapeDtypeStruct((padded, c), BF16), c // block
    return pl.pallas_call(
        functools.partial(_cast_pad_kernel, scale=scale),
        grid=(steps,),
        in_specs=[in_spec] * len(ws),
        out_specs=[out_spec] * len(ws),
        out_shape=[out_shape] * len(ws),
        compiler_params=pltpu.CompilerParams(
            dimension_semantics=("parallel",), vmem_limit_bytes=VMEM_LIMIT_V7X),
        name="cast_pad",
    )(*ws)


def _stream(x, p, *, tm, tf, tb, ssm_groups, cache=None):
    batch, seq, d = x.shape
    ffn = functools.partial(_ffn, tm=tm, tf=tf)
    x1 = ffn(x.reshape(batch * seq, d), p['g_ffn1'], p['ffn1_wg'], p['ffn1_wu'], p['ffn1_wd'], p['g_final'],
             final_norm=False).reshape(batch, seq, d)
    q, k, v, ug, kf, vf = _proj(x1, p['g_mix'], p['w_in'], p['q_norm'], p['k_norm'], tb=tb)
    n_g = ug.shape[0]
    if cache is None:
        att = _attn_prompt(q, k, v, p['bias_prompt'])
        s0 = jnp.zeros((n_g, 2, batch, 2 * SSM_STATE), F32)
    else:
        ck, cv, s_re, s_im = cache
        w_cache = ck.shape[1]
        assert w_cache == ATT_LEFT and seq <= ATT_QB
        att, kf, vf = _attn_sample(q, kf, vf, ck.reshape(batch, w_cache * N_HEADS, HEAD_DIM),
                                   cv.reshape(batch, w_cache * N_HEADS, HEAD_DIM),
                                   p['bias_table'][:, :seq, :w_cache + seq])
        s0 = _state_rows(s_re, s_im)
    yg, sf = _ssm(ug, p['lagk'], p['bm'], p['cm'], p['acoef'], s0, rows=batch, groups=ssm_groups)
    x2 = _mix(att, yg, x1, p['w_glu'], p['b_glu'], p['g_att'], p['g_ssm'], p['w_out'], tb=tb)
    y = ffn(x2.reshape(batch * seq, d), p['g_ffn2'], p['ffn2_wg'], p['ffn2_wu'], p['ffn2_wd'], p['g_final'],
            final_norm=True).reshape(batch, seq, d)
    sf = sf.transpose(1, 0, 2)
    return y, kf, vf, sf[..., :SSM_STATE], sf[..., SSM_STATE:]


def kernel(x_prompt, x_sample, cache_attn_k, cache_attn_v, state_ssm_re, state_ssm_im, norm_ffn1, ffn1_w_gate, ffn1_w_up, ffn1_w_down, norm_mix, w_in, q_norm, k_norm, rel_bias, ssm_lambda_re, ssm_lambda_im, ssm_log_dt, ssm_b_re, ssm_b_im, ssm_c_re, ssm_c_im, ssm_d, w_glu, b_glu, norm_att_out, norm_ssm_out, w_out, norm_ffn2, ffn2_w_gate, ffn2_w_up, ffn2_w_down, norm_final):
    depth = norm_ffn1.shape[0]
    bs, ls, _ = x_sample.shape
    yp, ys = x_prompt, x_sample
    outs = [[] for _ in range(8)]
    for l in range(depth):
        lagk, bm, cm, acoef = _ssm_matrices(ssm_lambda_re[l], ssm_lambda_im[l], ssm_log_dt[l], ssm_b_re[l],
                                          ssm_b_im[l], ssm_c_re[l], ssm_c_im[l], ssm_d[l])
        row = lambda a: a[l][None, :]
        bias_table = _bias_table(rel_bias[l])
        wg1, wu1, wg2, wu2 = _cast_pad([ffn1_w_gate[l], ffn1_w_up[l], ffn2_w_gate[l], ffn2_w_up[l]],
                                       axis=1, mult=FF_PAD, block=128)
        wd1, wd2 = _cast_pad([ffn1_w_down[l], ffn2_w_down[l]], axis=0, mult=FF_PAD, block=256, scale=0.5)
        p = dict(
            g_ffn1=row(norm_ffn1), g_mix=row(norm_mix), g_att=row(norm_att_out), g_ssm=row(norm_ssm_out),
            g_ffn2=row(norm_ffn2), g_final=row(norm_final), q_norm=row(q_norm), k_norm=row(k_norm),
            ffn1_wg=wg1, ffn1_wu=wu1, ffn1_wd=wd1, ffn2_wg=wg2, ffn2_wu=wu2, ffn2_wd=wd2,
            w_in=w_in[l].astype(BF16), w_glu=w_glu[l].astype(BF16), b_glu=row(b_glu), w_out=w_out[l].astype(BF16),
            bias_table=bias_table, bias_prompt=jnp.where(_band_mask()[None], bias_table, NEG_INF),
            lagk=lagk, bm=bm, cm=cm, acoef=acoef,
        )
        yp, kp, vp, rp, ip = _stream(yp, p, tm=1024, tf=256, tb=CHUNK, ssm_groups=4)
        ys, kd, vd, rd, idd = _stream(ys, p, tm=bs * ls, tf=FF_PAD, tb=ls, ssm_groups=8,
                                      cache=(cache_attn_k[l], cache_attn_v[l], state_ssm_re[l], state_ssm_im[l]))
        as_cache = lambda a: a.reshape(a.shape[0], -1, N_HEADS, HEAD_DIM)
        for lst, val in zip(outs, (as_cache(kp), as_cache(vp), rp, ip, as_cache(kd), as_cache(vd), rd, idd)):
            lst.append(val)
    return (yp, ys) + tuple(jnp.stack(o) for o in outs)
```

```python
import functools

import jax
import jax.numpy as jnp
from jax import lax
from jax.experimental import pallas as pl
from jax.experimental.pallas import tpu as pltpu

EPS = 1e-6
NEG_INF = -1e30
CHUNK = 64
LEFT_CHUNKS = 8
ATT_LEFT = LEFT_CHUNKS * CHUNK
REL_CLIP = 256
N_HEADS = 8
HEAD_DIM = 128
SSM_GROUP = 16
SSM_STATE = 64
LANES = 128
SSM_T = 16
SSM_W = SSM_T * SSM_GROUP
GROUPS_PER_TILE = LANES // SSM_GROUP
ATT_QB = 256
ATT_KW = ATT_QB + ATT_LEFT
REGROUP_PAD = 8
FF_PAD = 512
VMEM_LIMIT_V7X = 60 * 1024 * 1024

BF16 = jnp.bfloat16
F32 = jnp.float32


def _dot(a, b):
    return jnp.dot(a, b, preferred_element_type=F32)


def _rms(x, g):
    return x * lax.rsqrt(jnp.mean(x * x, axis=-1, keepdims=True) + EPS) * g


def _ffn_kernel(x_ref, g_ref, wg_ref, wu_ref, wd_ref, gf_ref, o_ref, xn_ref, *, final_norm):
    j = pl.program_id(1)

    @pl.when(j == 0)
    def _():
        x = x_ref[...]
        xn_ref[...] = _rms(x, g_ref[...]).astype(BF16)
        o_ref[...] = x

    xn = xn_ref[...]
    a = _dot(xn, wg_ref[...])
    b = _dot(xn, wu_ref[...])
    h = (a * jax.nn.sigmoid(a) * b).astype(BF16)
    o_ref[...] += _dot(h, wd_ref[...])

    if final_norm:
        @pl.when(j == pl.num_programs(1) - 1)
        def _():
            o_ref[...] = _rms(o_ref[...], gf_ref[...])


def _ffn(x, g, wg, wu, wd_half, gf, *, tm, tf, final_norm):
    n, d = x.shape
    fpad = wg.shape[1]
    return pl.pallas_call(
        functools.partial(_ffn_kernel, final_norm=final_norm),
        grid=(n // tm, fpad // tf),
        in_specs=[
            pl.BlockSpec((tm, d), lambda i, j: (i, 0)),
            pl.BlockSpec((1, d), lambda i, j: (0, 0)),
            pl.BlockSpec((d, tf), lambda i, j: (0, j)),
            pl.BlockSpec((d, tf), lambda i, j: (0, j)),
            pl.BlockSpec((tf, d), lambda i, j: (j, 0)),
            pl.BlockSpec((1, d), lambda i, j: (0, 0)),
        ],
        out_specs=pl.BlockSpec((tm, d), lambda i, j: (i, 0)),
        out_shape=jax.ShapeDtypeStruct((n, d), F32),
        scratch_shapes=[pltpu.VMEM((tm, d), BF16)],
        compiler_params=pltpu.CompilerParams(
            dimension_semantics=("parallel", "arbitrary"), vmem_limit_bytes=VMEM_LIMIT_V7X),
        name="ffn",
    )(x, g, wg, wu, wd_half, gf)


def _granule_transpose(vs):
    gran = lax.broadcasted_iota(jnp.int32, vs[0].shape, 1) // SSM_GROUP
    for s in (4, 2, 1):
        upper = (gran & s) != 0
        new = list(vs)
        for a in range(GROUPS_PER_TILE):
            if a & s == 0:
                lo, hi = vs[a], vs[a + s]
                new[a] = jnp.where(upper, pltpu.roll(hi, s * SSM_GROUP, 1), lo)
                new[a + s] = jnp.where(upper, hi, pltpu.roll(lo, LANES - s * SSM_GROUP, 1))
        vs = new
    return vs


def _regroup_geometry(nb, tb):
    return tb + REGROUP_PAD, tb // SSM_T


def _tokens_to_groups(slab_ref, ug_ref, *, nb, tb):
    pitch, ncl = _regroup_geometry(nb, tb)
    for j in range(slab_ref.shape[0]):
        xs = [jnp.concatenate([slab_ref[j, pl.ds(c * SSM_T + t, nb, stride=pitch), :] for c in range(ncl)], axis=0)
              for t in range(SSM_T)]
        lo = _granule_transpose(xs[:GROUPS_PER_TILE])
        hi = _granule_transpose(xs[GROUPS_PER_TILE:])
        for gl in range(GROUPS_PER_TILE):
            ug_ref[j * GROUPS_PER_TILE + gl] = jnp.concatenate([lo[gl], hi[gl]], axis=1).astype(ug_ref.dtype)


def _groups_to_tokens(yg_ref, slab_ref, *, nb, tb):
    pitch, ncl = _regroup_geometry(nb, tb)
    for j in range(slab_ref.shape[0]):
        ds = [yg_ref[j * GROUPS_PER_TILE + gl].astype(F32) for gl in range(GROUPS_PER_TILE)]
        lo = _granule_transpose([d[:, :LANES] for d in ds])
        hi = _granule_transpose([d[:, LANES:] for d in ds])
        for t, x in enumerate(lo + hi):
            for c in range(ncl):
                slab_ref[j, pl.ds(c * SSM_T + t, nb, stride=pitch), :] = x[c * nb:(c + 1) * nb]


def _proj_kernel(x_ref, g_ref, w_ref, qn_ref, kn_ref, q_ref, k_ref, v_ref, ug_ref, kf_ref, vf_ref, slab_ref,
                 *, first_tail_block):
    i = pl.program_id(0)
    nb, tb, d = x_ref.shape
    d_att = N_HEADS * HEAD_DIM
    pitch, _ = _regroup_geometry(nb, tb)
    h = _rms(x_ref[...].reshape(nb * tb, d), g_ref[...]).astype(BF16)
    u = _dot(h, w_ref[:, 3 * d_att:])
    for j in range(slab_ref.shape[0]):
        for b in range(nb):
            slab_ref[j, b * pitch:b * pitch + tb, :] = u[b * tb:(b + 1) * tb, j * LANES:(j + 1) * LANES]
    _tokens_to_groups(slab_ref, ug_ref, nb=nb, tb=tb)

    def head_norm(y, gain):
        return [_rms(y[:, hd * HEAD_DIM:(hd + 1) * HEAD_DIM], gain) for hd in range(N_HEADS)]

    def store_tokens(ref, y):
        for b in range(nb):
            ref[b] = y[b * tb:(b + 1) * tb].astype(ref.dtype)

    qs = head_norm(_dot(h, w_ref[:, 0:d_att]), qn_ref[...] * (HEAD_DIM ** -0.5))
    store_tokens(q_ref, jnp.concatenate(qs, axis=1))
    ks = head_norm(_dot(h, w_ref[:, d_att:2 * d_att]), kn_ref[...])
    store_tokens(k_ref, jnp.concatenate(ks, axis=1))
    v = _dot(h, w_ref[:, 2 * d_att:3 * d_att])
    store_tokens(v_ref, v)

    @pl.when(i >= first_tail_block)
    def _():
        for b in range(nb):
            rows = slice(b * tb, (b + 1) * tb)
            for hd in range(N_HEADS):
                head_rows = pl.ds(hd, tb, stride=N_HEADS)
                kf_ref[b, head_rows, :] = ks[hd][rows]
                vf_ref[b, head_rows, :] = v[rows, hd * HEAD_DIM:(hd + 1) * HEAD_DIM]


def _proj(x, g, w, qn, kn, *, tb):
    nb, seq, d = x.shape
    d_att = N_HEADS * HEAD_DIM
    d_ssm = w.shape[1] - 3 * d_att
    n_g = d_ssm // SSM_GROUP
    n_blocks = seq // tb
    tail = min(ATT_LEFT, seq)
    first_tail_block = n_blocks - tail // tb
    pitch, ncl = _regroup_geometry(nb, tb)
    tok = lambda i: (0, i, 0)
    const = lambda i: (0, 0)
    tail_map = lambda i: (0, jnp.maximum(i - first_tail_block, 0), 0)
    return pl.pallas_call(
        functools.partial(_proj_kernel, first_tail_block=first_tail_block),
        grid=(n_blocks,),
        in_specs=[
            pl.BlockSpec((nb, tb, d), tok),
            pl.BlockSpec((1, d), const),
            pl.BlockSpec(w.shape, const, pipeline_mode=pl.Buffered(1)),
            pl.BlockSpec((1, HEAD_DIM), const),
            pl.BlockSpec((1, HEAD_DIM), const),
        ],
        out_specs=[
            pl.BlockSpec((nb, tb, d_att), tok),
            pl.BlockSpec((nb, tb, d_att), tok),
            pl.BlockSpec((nb, tb, d_att), tok),
            pl.BlockSpec((n_g, ncl * nb, SSM_W), tok),
            pl.BlockSpec((nb, tb * N_HEADS, HEAD_DIM), tail_map),
            pl.BlockSpec((nb, tb * N_HEADS, HEAD_DIM), tail_map),
        ],
        out_shape=[
            jax.ShapeDtypeStruct((nb, seq, d_att), BF16),
            jax.ShapeDtypeStruct((nb, seq, d_att), BF16),
            jax.ShapeDtypeStruct((nb, seq, d_att), BF16),
            jax.ShapeDtypeStruct((n_g, (seq // SSM_T) * nb, SSM_W), BF16),
            jax.ShapeDtypeStruct((nb, tail * N_HEADS, HEAD_DIM), F32),
            jax.ShapeDtypeStruct((nb, tail * N_HEADS, HEAD_DIM), F32),
        ],
        scratch_shapes=[pltpu.VMEM((d_ssm // LANES, nb * pitch, LANES), F32)],
        compiler_params=pltpu.CompilerParams(
            dimension_semantics=("arbitrary",), vmem_limit_bytes=VMEM_LIMIT_V7X),
        name="proj",
    )(x, g, w, qn, kn)


def _softmax_pv(s, v):
    m = jnp.max(s, axis=-1, keepdims=True)
    p = jnp.exp(s - m)
    l = jnp.sum(p, axis=-1, keepdims=True)
    return _dot(p.astype(BF16), v) / l


def _qk(q, k):
    return lax.dot_general(q, k, (((1,), (1,)), ((), ())), preferred_element_type=F32)


def _attn_prompt_kernel(q_ref, k_ref, v_ref, bias_ref, o_ref, s_ref, p_ref, linv_ref, *, n_blocks):
    lead = ATT_LEFT // ATT_QB
    for i in range(min(lead, n_blocks)):
        kw = (i + 1) * ATT_QB
        rows = slice(i * ATT_QB, (i + 1) * ATT_QB)
        s = _qk(q_ref[rows, :], k_ref[0:kw, :]) + bias_ref[:, ATT_KW - kw:]
        o_ref[rows, :] = _softmax_pv(s, v_ref[0:kw, :]).astype(BF16)
    n_full = n_blocks - lead
    if n_full <= 0:
        return

    def start(i):
        return pl.multiple_of(i * ATT_QB, ATT_QB)

    def scores(i, slot):
        i = jnp.minimum(i, n_blocks - 1)
        s_ref[slot] = _qk(q_ref[pl.ds(start(i), ATT_QB), :], k_ref[pl.ds(start(i) - ATT_LEFT, ATT_KW), :])

    def softmax(slot):
        s = s_ref[slot] + bias_ref[...]
        p = jnp.exp(s - jnp.max(s, axis=-1, keepdims=True))
        p_ref[slot] = p.astype(BF16)
        linv_ref[slot] = jnp.broadcast_to(1.0 / jnp.sum(p, axis=-1, keepdims=True), linv_ref.shape[1:])

    def output(i, slot):
        v = v_ref[pl.ds(start(i) - ATT_LEFT, ATT_KW), :]
        o_ref[pl.ds(start(i), ATT_QB), :] = (_dot(p_ref[slot], v) * linv_ref[slot]).astype(BF16)

    def step(i, slot):
        scores(i + 1, 1 - slot)
        output(i - 1, 1 - slot)
        softmax(slot)

    scores(lead, 0)
    scores(lead + 1, 1)
    softmax(0)
    n_pairs = (n_full - 1) // 2

    def pair(r, carry):
        i = lead + 1 + 2 * r
        step(i, 1)
        step(i + 1, 0)
        return carry

    lax.fori_loop(0, n_pairs, pair, 0)
    last_slot = 0
    if (n_full - 1) % 2:
        step(n_blocks - 1, 1)
        last_slot = 1
    output(n_blocks - 1, last_slot)


def _attn_prompt(q, k, v, bias):
    batch, seq, d_att = q.shape
    blk = pl.BlockSpec((None, seq, HEAD_DIM), lambda b, h: (b, 0, h))
    return pl.pallas_call(
        functools.partial(_attn_prompt_kernel, n_blocks=seq // ATT_QB),
        grid=(batch, N_HEADS),
        in_specs=[blk, blk, blk, pl.BlockSpec((None, ATT_QB, ATT_KW), lambda b, h: (h, 0, 0))],
        out_specs=blk,
        out_shape=jax.ShapeDtypeStruct((batch, seq, d_att), BF16),
        scratch_shapes=[pltpu.VMEM((2, ATT_QB, ATT_KW), F32), pltpu.VMEM((2, ATT_QB, ATT_KW), BF16),
                        pltpu.VMEM((2, ATT_QB, HEAD_DIM), F32)],
        compiler_params=pltpu.CompilerParams(
            dimension_semantics=("parallel", "parallel"), vmem_limit_bytes=VMEM_LIMIT_V7X),
        name="attn_prompt",
    )(q, k, v, bias)


def _attn_sample_kernel(q_ref, kn_ref, vn_ref, kc_ref, vc_ref, bias_ref, o_ref, ko_ref, vo_ref, *, w_cache, seq):
    def head_rows(ref, n, hd):
        return ref[pl.ds(hd, n, stride=N_HEADS), :].astype(BF16)

    for hd in range(N_HEADS):
        sl = slice(hd * HEAD_DIM, (hd + 1) * HEAD_DIM)
        q = q_ref[:, sl]
        s1 = _qk(q, head_rows(kc_ref, w_cache, hd)) + bias_ref[hd, :, :w_cache]
        s2 = _qk(q, head_rows(kn_ref, seq, hd)) + bias_ref[hd, :, w_cache:]
        m = jnp.maximum(jnp.max(s1, axis=-1, keepdims=True), jnp.max(s2, axis=-1, keepdims=True))
        p1 = jnp.exp(s1 - m)
        p2 = jnp.exp(s2 - m)
        l = jnp.sum(p1, axis=-1, keepdims=True) + jnp.sum(p2, axis=-1, keepdims=True)
        o = (_dot(p1.astype(BF16), head_rows(vc_ref, w_cache, hd))
             + _dot(p2.astype(BF16), head_rows(vn_ref, seq, hd)))
        o_ref[:, sl] = (o / l).astype(BF16)

    keep = (w_cache - seq) * N_HEADS
    for new_ref, old_ref, out_ref in ((kn_ref, kc_ref, ko_ref), (vn_ref, vc_ref, vo_ref)):
        out_ref[:keep, :] = old_ref[seq * N_HEADS:, :]
        out_ref[keep:, :] = new_ref[...]


def _attn_sample(q, kn, vn, kc, vc, bias):
    batch, seq, d_att = q.shape
    w_cache = kc.shape[1] // N_HEADS
    assert seq <= w_cache
    tok = pl.BlockSpec((None, seq, d_att), lambda b: (b, 0, 0))
    new = pl.BlockSpec((None, seq * N_HEADS, HEAD_DIM), lambda b: (b, 0, 0))
    cache = pl.BlockSpec((None, w_cache * N_HEADS, HEAD_DIM), lambda b: (b, 0, 0))
    return pl.pallas_call(
        functools.partial(_attn_sample_kernel, w_cache=w_cache, seq=seq),
        grid=(batch,),
        in_specs=[tok, new, new, cache, cache, pl.BlockSpec(bias.shape, lambda b: (0, 0, 0))],
        out_specs=[tok, cache, cache],
        out_shape=[jax.ShapeDtypeStruct((batch, seq, d_att), BF16),
                   jax.ShapeDtypeStruct(kc.shape, F32), jax.ShapeDtypeStruct(vc.shape, F32)],
        compiler_params=pltpu.CompilerParams(
            dimension_semantics=("parallel",), vmem_limit_bytes=VMEM_LIMIT_V7X),
        name="attn_sample",
    )(q, kn, vn, kc, vc, bias)


def _ssm_kernel(u_ref, lag_ref, bm_ref, cm_ref, a_ref, s0_ref, y_ref, sf_ref, sl_ref, sp_ref, km_ref,
                *, rows, n_chunks, groups):
    half = 2 * SSM_STATE
    lane = lax.broadcasted_iota(jnp.int32, (SSM_GROUP, SSM_W), 1)
    for gi in range(groups):
        lag = lag_ref[gi]
        for t_in in range(SSM_T):
            shifted = lag if t_in == 0 else pltpu.roll(lag, t_in * SSM_GROUP, 1)
            km_ref[gi, t_in * SSM_GROUP:(t_in + 1) * SSM_GROUP, :] = jnp.where(
                lane >= t_in * SSM_GROUP, shifted, 0.0).astype(BF16)
        sl_ref[gi] = _dot(u_ref[gi], bm_ref[gi])

    coef = []
    for gi in range(groups):
        coef.append(tuple(jnp.broadcast_to(a_ref[gi, r:r + 1, :], (rows, half)) for r in range(3)))

    def body(c, carry):
        r0 = pl.multiple_of(c * rows, rows)
        new = []
        for gi in range(groups):
            s, sw = carry[gi]
            a1, a2, a2w = coef[gi]
            sp_ref[gi, pl.ds(r0, rows), :] = s
            loc = sl_ref[gi, pl.ds(r0, rows), :]
            new.append((a1 * s + a2 * sw + loc[:, :half], a1 * sw + a2w * s + loc[:, half:]))
        return tuple(new)

    init = tuple((s0_ref[gi, 0], s0_ref[gi, 1]) for gi in range(groups))
    last = lax.fori_loop(0, n_chunks, body, init)
    for gi in range(groups):
        sf_ref[gi] = last[gi][0]
        y = _dot(u_ref[gi], km_ref[gi]) + _dot(sp_ref[gi].astype(BF16), cm_ref[gi])
        y_ref[gi] = jax.nn.gelu(y).astype(BF16)


def _ssm(ug, lagk, bm, cm, acoef, s0, *, rows, groups):
    n_g, n_rows, _ = ug.shape
    n_chunks = n_rows // rows
    half = 2 * SSM_STATE
    g3 = lambda g: (g, 0, 0)
    return pl.pallas_call(
        functools.partial(_ssm_kernel, rows=rows, n_chunks=n_chunks, groups=groups),
        grid=(n_g // groups,),
        in_specs=[
            pl.BlockSpec((groups, n_rows, SSM_W), g3),
            pl.BlockSpec((groups, SSM_GROUP, SSM_W), g3),
            pl.BlockSpec((groups, SSM_W, 2 * half), g3),
            pl.BlockSpec((groups, half, SSM_W), g3),
            pl.BlockSpec((groups, 3, half), g3),
            pl.BlockSpec((groups, 2, rows, half), lambda g: (g, 0, 0, 0)),
        ],
        out_specs=[
            pl.BlockSpec((groups, n_rows, SSM_W), g3),
            pl.BlockSpec((groups, rows, half), g3),
        ],
        out_shape=[
            jax.ShapeDtypeStruct((n_g, n_rows, SSM_W), BF16),
            jax.ShapeDtypeStruct((n_g, rows, half), F32),
        ],
        scratch_shapes=[
            pltpu.VMEM((groups, n_rows, 2 * half), F32),
            pltpu.VMEM((groups, n_rows, half), F32),
            pltpu.VMEM((groups, SSM_W, SSM_W), BF16),
        ],
        compiler_params=pltpu.CompilerParams(
            dimension_semantics=("parallel",), vmem_limit_bytes=VMEM_LIMIT_V7X),
        name="ssm",
    )(ug, lagk, bm, cm, acoef, s0)


def _mix_kernel(att_ref, yg_ref, x_ref, wglu_ref, bglu_ref, ga_ref, gs_ref, wout_ref, o_ref, slab_ref, y_ref):
    nb, tb, d = x_ref.shape
    d_att = att_ref.shape[2]
    d_ssm = y_ref.shape[1]
    pitch, _ = _regroup_geometry(nb, tb)
    att = att_ref[...].reshape(nb * tb, d_att).astype(F32)
    mix_a = _rms(att, ga_ref[...]).astype(BF16)
    o = x_ref[...].reshape(nb * tb, d) + _dot(mix_a, wout_ref[:d_att, :])
    _groups_to_tokens(yg_ref, slab_ref, nb=nb, tb=tb)
    for j in range(slab_ref.shape[0]):
        for b in range(nb):
            y_ref[b * tb:(b + 1) * tb, j * LANES:(j + 1) * LANES] = slab_ref[j, b * pitch:b * pitch + tb, :].astype(BF16)
    glu = _dot(y_ref[...], wglu_ref[...]) + bglu_ref[...]
    ssm_out = glu[:, :d_ssm] * jax.nn.sigmoid(glu[:, d_ssm:])
    mix_s = _rms(ssm_out, gs_ref[...]).astype(BF16)
    o = o + _dot(mix_s, wout_ref[d_att:, :])
    o_ref[...] = o.reshape(nb, tb, d)


def _mix(att, yg, x, wglu, bglu, ga, gs, wout, *, tb):
    nb, seq, d = x.shape
    d_att = att.shape[2]
    n_g = yg.shape[0]
    d_ssm = n_g * SSM_GROUP
    pitch, ncl = _regroup_geometry(nb, tb)
    tok = lambda i: (0, i, 0)
    const = lambda i: (0, 0)
    once = lambda a: pl.BlockSpec(a.shape, const, pipeline_mode=pl.Buffered(1))
    return pl.pallas_call(
        _mix_kernel,
        grid=(seq // tb,),
        in_specs=[
            pl.BlockSpec((nb, tb, d_att), tok),
            pl.BlockSpec((n_g, ncl * nb, SSM_W), tok),
            pl.BlockSpec((nb, tb, d), tok),
            once(wglu), once(bglu), once(ga), once(gs), once(wout),
        ],
        out_specs=pl.BlockSpec((nb, tb, d), tok),
        out_shape=jax.ShapeDtypeStruct((nb, seq, d), F32),
        scratch_shapes=[pltpu.VMEM((d_ssm // LANES, nb * pitch, LANES), F32),
                        pltpu.VMEM((nb * tb, d_ssm), BF16)],
        compiler_params=pltpu.CompilerParams(
            dimension_semantics=("parallel",), vmem_limit_bytes=VMEM_LIMIT_V7X),
        name="mix",
    )(att, yg, x, wglu, bglu, ga, gs, wout)


def _ssm_matrices(lam_re, lam_im, log_dt, b_re, b_im, c_re, c_im, d_skip):
    hp = lax.Precision.HIGHEST
    n_g = lam_re.shape[0]
    dt = jnp.exp(log_dt)[:, None]
    n = jnp.arange(SSM_T + 1, dtype=F32)[:, None, None]
    mag = jnp.exp(lam_re * dt * n)
    ang = lam_im * dt * n
    pw_re, pw_im = mag * jnp.cos(ang), mag * jnp.sin(ang)
    x, y = pw_re[1] - 1.0, pw_im[1]
    den = lam_re * lam_re + lam_im * lam_im
    z_re, z_im = (x * lam_re + y * lam_im) / den, (y * lam_re - x * lam_im) / den
    bb_re = z_re[..., None] * b_re - z_im[..., None] * b_im
    bb_im = z_re[..., None] * b_im + z_im[..., None] * b_re
    pb_re = pw_re[..., None] * bb_re - pw_im[..., None] * bb_im
    pb_im = pw_re[..., None] * bb_im + pw_im[..., None] * bb_re
    lagk = (jnp.einsum('gop,dgpi->gido', c_re, pb_re[:SSM_T], precision=hp)
            - jnp.einsum('gop,dgpi->gido', c_im, pb_im[:SSM_T], precision=hp))
    skip = jnp.eye(SSM_GROUP, dtype=F32)[None] * d_skip[:, :, None]
    lagk = lagk.at[:, :, 0, :].add(skip).reshape(n_g, SSM_GROUP, SSM_W)
    inj_re = pb_re[:SSM_T][::-1].transpose(1, 0, 3, 2).reshape(n_g, SSM_W, SSM_STATE)
    inj_im = pb_im[:SSM_T][::-1].transpose(1, 0, 3, 2).reshape(n_g, SSM_W, SSM_STATE)
    bm = jnp.concatenate([inj_re, inj_im, inj_im, inj_re], axis=-1)
    cp_re = c_re[None] * pw_re[1:, :, None, :] - c_im[None] * pw_im[1:, :, None, :]
    cp_im = c_re[None] * pw_im[1:, :, None, :] + c_im[None] * pw_re[1:, :, None, :]
    out_re = cp_re.transpose(1, 3, 0, 2).reshape(n_g, SSM_STATE, SSM_W)
    out_im = -cp_im.transpose(1, 3, 0, 2).reshape(n_g, SSM_STATE, SSM_W)
    cm = jnp.concatenate([out_re, out_im], axis=1)
    ar, ai = pw_re[SSM_T], pw_im[SSM_T]
    acoef = jnp.stack([jnp.concatenate([ar, ar], -1), jnp.concatenate([-ai, ai], -1),
                       jnp.concatenate([ai, -ai], -1)], axis=1)
    return lagk, bm.astype(BF16), cm.astype(BF16), acoef


def _bias_table(rel_bias):
    n_heads = rel_bias.shape[0]
    ext = ATT_QB + ATT_KW
    n_edge = ATT_LEFT - REL_CLIP + 1
    assert ATT_KW - n_edge == 2 * REL_CLIP - 1 and ATT_LEFT >= REL_CLIP
    far = rel_bias[:, 2 * REL_CLIP:]
    row = jnp.concatenate([jnp.broadcast_to(far, (n_heads, n_edge)), rel_bias[:, 1:2 * REL_CLIP][:, ::-1],
                           jnp.broadcast_to(far, (n_heads, ATT_QB))], axis=1)
    skew = jnp.tile(row, (1, ATT_QB))[:, :ATT_QB * (ext - 1)].reshape(n_heads, ATT_QB, ext - 1)
    return skew[:, :, :ATT_KW].astype(F32)


def _band_mask():
    r = jnp.arange(ATT_QB)[:, None] // CHUNK
    j = jnp.arange(ATT_KW)[None, :] // CHUNK
    return (j >= r) & (j <= r + LEFT_CHUNKS)


def _state_rows(s_re, s_im):
    s = jnp.concatenate([s_re, s_im], -1).transpose(1, 0, 2)
    sw = jnp.concatenate([s_im, s_re], -1).transpose(1, 0, 2)
    return jnp.stack([s, sw], axis=1)


def _cast_pad_kernel(*refs, scale):
    n = len(refs) // 2
    for w_ref, o_ref in zip(refs[:n], refs[n:]):
        r, c = w_ref.shape
        w = w_ref[...]
        o_ref[:r, :c] = (w if scale == 1.0 else w * scale).astype(BF16)
        if o_ref.shape[0] > r:
            o_ref[r:, :] = jnp.zeros((o_ref.shape[0] - r, o_ref.shape[1]), BF16)
        if o_ref.shape[1] > c:
            o_ref[:, c:] = jnp.zeros((o_ref.shape[0], o_ref.shape[1] - c), BF16)


def _cast_pad(ws, *, axis, mult, block, scale=1.0):
    r, c = ws[0].shape
    padded = -(-ws[0].shape[axis] // mult) * mult
    if axis == 1:
        in_spec, out_spec = pl.BlockSpec((block, c), lambda i: (i, 0)), pl.BlockSpec((block, padded), lambda i: (i, 0))
        out_shape, steps = jax.ShapeDtypeStruct((r, padded), BF16), r // block
    else:
        in_spec, out_spec = pl.BlockSpec((r, block), lambda i: (0, i)), pl.BlockSpec((padded, block), lambda i: (0, i))
        out_shape, steps = jax.ShapeDtypeStruct((padded, c), BF16), c // block
    return pl.pallas_call(
        functools.partial(_cast_pad_kernel, scale=scale),
        grid=(steps,),
        in_specs=[in_spec] * len(ws),
        out_specs=[out_spec] * len(ws),
        out_shape=[out_shape] * len(ws),
        compiler_params=pltpu.CompilerParams(
            dimension_semantics=("parallel",), vmem_limit_bytes=VMEM_LIMIT_V7X),
        name="cast_pad",
    )(*ws)


def _stream(x, p, *, tm, tf, tb, ssm_groups, cache=None):
    batch, seq, d = x.shape
    ffn = functools.partial(_ffn, tm=tm, tf=tf)
    x1 = ffn(x.reshape(batch * seq, d), p['g_ffn1'], p['ffn1_wg'], p['ffn1_wu'], p['ffn1_wd'], p['g_final'],
             final_norm=False).reshape(batch, seq, d)
    q, k, v, ug, kf, vf = _proj(x1, p['g_mix'], p['w_in'], p['q_norm'], p['k_norm'], tb=tb)
    n_g = ug.shape[0]
    if cache is None:
        att = _attn_prompt(q, k, v, p['bias_prompt'])
        s0 = jnp.zeros((n_g, 2, batch, 2 * SSM_STATE), F32)
    else:
        ck, cv, s_re, s_im = cache
        w_cache = ck.shape[1]
        assert w_cache == ATT_LEFT and seq <= ATT_QB
        att, kf, vf = _attn_sample(q, kf, vf, ck.reshape(batch, w_cache * N_HEADS, HEAD_DIM),
                                   cv.reshape(batch, w_cache * N_HEADS, HEAD_DIM),
                                   p['bias_table'][:, :seq, :w_cache + seq])
        s0 = _state_rows(s_re, s_im)
    yg, sf = _ssm(ug, p['lagk'], p['bm'], p['cm'], p['acoef'], s0, rows=batch, groups=ssm_groups)
    x2 = _mix(att, yg, x1, p['w_glu'], p['b_glu'], p['g_att'], p['g_ssm'], p['w_out'], tb=tb)
    y = ffn(x2.reshape(batch * seq, d), p['g_ffn2'], p['ffn2_wg'], p['ffn2_wu'], p['ffn2_wd'], p['g_final'],
            final_norm=True).reshape(batch, seq, d)
    sf = sf.transpose(1, 0, 2)
    return y, kf, vf, sf[..., :SSM_STATE], sf[..., SSM_STATE:]


def kernel(x_prompt, x_sample, cache_attn_k, cache_attn_v, state_ssm_re, state_ssm_im, norm_ffn1, ffn1_w_gate, ffn1_w_up, ffn1_w_down, norm_mix, w_in, q_norm, k_norm, rel_bias, ssm_lambda_re, ssm_lambda_im, ssm_log_dt, ssm_b_re, ssm_b_im, ssm_c_re, ssm_c_im, ssm_d, w_glu, b_glu, norm_att_out, norm_ssm_out, w_out, norm_ffn2, ffn2_w_gate, ffn2_w_up, ffn2_w_down, norm_final):
    depth = norm_ffn1.shape[0]
    bs, ls, _ = x_sample.shape
    yp, ys = x_prompt, x_sample
    outs = [[] for _ in range(8)]
    for l in range(depth):
        lagk, bm, cm, acoef = _ssm_matrices(ssm_lambda_re[l], ssm_lambda_im[l], ssm_log_dt[l], ssm_b_re[l],
                                          ssm_b_im[l], ssm_c_re[l], ssm_c_im[l], ssm_d[l])
        row = lambda a: a[l][None, :]
        bias_table = _bias_table(rel_bias[l])
        wg1, wu1, wg2, wu2 = _cast_pad([ffn1_w_gate[l], ffn1_w_up[l], ffn2_w_gate[l], ffn2_w_up[l]],
                                       axis=1, mult=FF_PAD, block=128)
        wd1, wd2 = _cast_pad([ffn1_w_down[l], ffn2_w_down[l]], axis=0, mult=FF_PAD, block=256, scale=0.5)
        p = dict(
            g_ffn1=row(norm_ffn1), g_mix=row(norm_mix), g_att=row(norm_att_out), g_ssm=row(norm_ssm_out),
            g_ffn2=row(norm_ffn2), g_final=row(norm_final), q_norm=row(q_norm), k_norm=row(k_norm),
            ffn1_wg=wg1, ffn1_wu=wu1, ffn1_wd=wd1, ffn2_wg=wg2, ffn2_wu=wu2, ffn2_wd=wd2,
            w_in=w_in[l].astype(BF16), w_glu=w_glu[l].astype(BF16), b_glu=row(b_glu), w_out=w_out[l].astype(BF16),
            bias_table=bias_table, bias_prompt=jnp.where(_band_mask()[None], bias_table, NEG_INF),
            lagk=lagk, bm=bm, cm=cm, acoef=acoef,
        )
        yp, kp, vp, rp, ip = _stream(yp, p, tm=1024, tf=FF_PAD, tb=CHUNK, ssm_groups=4)
        ys, kd, vd, rd, idd = _stream(ys, p, tm=bs * ls, tf=FF_PAD, tb=ls, ssm_groups=8,
                                      cache=(cache_attn_k[l], cache_attn_v[l], state_ssm_re[l], state_ssm_im[l]))
        as_cache = lambda a: a.reshape(a.shape[0], -1, N_HEADS, HEAD_DIM)
        for lst, val in zip(outs, (as_cache(kp), as_cache(vp), rp, ip, as_cache(kd), as_cache(vd), rd, idd)):
            lst.append(val)
    return (yp, ys) + tuple(jnp.stack(o) for o in outs)
```

```python
import functools

import jax
import jax.numpy as jnp
from jax import lax
from jax.experimental import pallas as pl
from jax.experimental.pallas import tpu as pltpu

EPS = 1e-6
NEG_INF = -1e30
CHUNK = 64
LEFT_CHUNKS = 8
ATT_LEFT = LEFT_CHUNKS * CHUNK
REL_CLIP = 256
N_HEADS = 8
HEAD_DIM = 128
SSM_GROUP = 16
SSM_STATE = 64
LANES = 128
SSM_T = 16
SSM_W = SSM_T * SSM_GROUP
GROUPS_PER_TILE = LANES // SSM_GROUP
ATT_QB = 256
ATT_KW = ATT_QB + ATT_LEFT
REGROUP_PAD = 8
FF_PAD = 512
VMEM_LIMIT_V7X = 60 * 1024 * 1024

BF16 = jnp.bfloat16
F32 = jnp.float32


def _dot(a, b):
    return jnp.dot(a, b, preferred_element_type=F32)


def _rms(x, g):
    return x * lax.rsqrt(jnp.mean(x * x, axis=-1, keepdims=True) + EPS) * g


def _ffn_kernel(x_ref, g_ref, wg_hbm, wu_hbm, wd_hbm, gf_ref, o_ref, xn_ref, wg_buf, wu_buf, wd_buf, sem,
                *, tf, n_chunks, final_norm):
    i = pl.program_id(0)
    first_slot = (i * n_chunks) % 2

    def chunk_copies(j, slot):
        cols = pl.ds(pl.multiple_of(j * tf, tf), tf)
        return (pltpu.make_async_copy(wg_hbm.at[:, cols], wg_buf.at[slot], sem.at[0, slot]),
                pltpu.make_async_copy(wu_hbm.at[:, cols], wu_buf.at[slot], sem.at[1, slot]),
                pltpu.make_async_copy(wd_hbm.at[cols, :], wd_buf.at[slot], sem.at[2, slot]))

    @pl.when(i == 0)
    def _():
        for c in chunk_copies(0, 0):
            c.start()

    x = x_ref[...]
    xn_ref[...] = _rms(x, g_ref[...]).astype(BF16)
    o_ref[...] = x

    def body(j, carry):
        slot = (first_slot + j) % 2
        nxt = jnp.where(j + 1 == n_chunks, 0, j + 1)
        for c in chunk_copies(nxt, 1 - slot):
            c.start()
        for c in chunk_copies(j, slot):
            c.wait()
        xn = xn_ref[...]
        a = _dot(xn, wg_buf[slot])
        b = _dot(xn, wu_buf[slot])
        h = (a * jax.nn.sigmoid(a) * b).astype(BF16)
        o_ref[...] += _dot(h, wd_buf[slot])
        return carry

    lax.fori_loop(0, n_chunks, body, 0)

    @pl.when(i == pl.num_programs(0) - 1)
    def _():
        for c in chunk_copies(0, (first_slot + n_chunks) % 2):
            c.wait()

    if final_norm:
        o_ref[...] = _rms(o_ref[...], gf_ref[...])


def _ffn(x, g, wg, wu, wd_half, gf, *, tm, tf, final_norm):
    n, d = x.shape
    fpad = wg.shape[1]
    hbm = pl.BlockSpec(memory_space=pl.ANY)
    return pl.pallas_call(
        functools.partial(_ffn_kernel, tf=tf, n_chunks=fpad // tf, final_norm=final_norm),
        grid=(n // tm,),
        in_specs=[
            pl.BlockSpec((tm, d), lambda i: (i, 0)),
            pl.BlockSpec((1, d), lambda i: (0, 0)),
            hbm, hbm, hbm,
            pl.BlockSpec((1, d), lambda i: (0, 0)),
        ],
        out_specs=pl.BlockSpec((tm, d), lambda i: (i, 0)),
        out_shape=jax.ShapeDtypeStruct((n, d), F32),
        scratch_shapes=[pltpu.VMEM((tm, d), BF16),
                        pltpu.VMEM((2, d, tf), BF16), pltpu.VMEM((2, d, tf), BF16), pltpu.VMEM((2, tf, d), BF16),
                        pltpu.SemaphoreType.DMA((3, 2))],
        compiler_params=pltpu.CompilerParams(
            dimension_semantics=("arbitrary",), vmem_limit_bytes=VMEM_LIMIT_V7X),
        name="ffn",
    )(x, g, wg, wu, wd_half, gf)


def _granule_transpose(vs):
    gran = lax.broadcasted_iota(jnp.int32, vs[0].shape, 1) // SSM_GROUP
    for s in (4, 2, 1):
        upper = (gran & s) != 0
        new = list(vs)
        for a in range(GROUPS_PER_TILE):
            if a & s == 0:
                lo, hi = vs[a], vs[a + s]
                new[a] = jnp.where(upper, pltpu.roll(hi, s * SSM_GROUP, 1), lo)
                new[a + s] = jnp.where(upper, hi, pltpu.roll(lo, LANES - s * SSM_GROUP, 1))
        vs = new
    return vs


def _regroup_geometry(nb, tb):
    return tb + REGROUP_PAD, tb // SSM_T


def _tokens_to_groups(slab_ref, ug_ref, *, nb, tb):
    pitch, ncl = _regroup_geometry(nb, tb)
    for j in range(slab_ref.shape[0]):
        xs = [jnp.concatenate([slab_ref[j, pl.ds(c * SSM_T + t, nb, stride=pitch), :] for c in range(ncl)], axis=0)
              for t in range(SSM_T)]
        lo = _granule_transpose(xs[:GROUPS_PER_TILE])
        hi = _granule_transpose(xs[GROUPS_PER_TILE:])
        for gl in range(GROUPS_PER_TILE):
            ug_ref[j * GROUPS_PER_TILE + gl] = jnp.concatenate([lo[gl], hi[gl]], axis=1).astype(ug_ref.dtype)


def _groups_to_tokens(yg_ref, slab_ref, *, nb, tb):
    pitch, ncl = _regroup_geometry(nb, tb)
    for j in range(slab_ref.shape[0]):
        ds = [yg_ref[j * GROUPS_PER_TILE + gl].astype(F32) for gl in range(GROUPS_PER_TILE)]
        lo = _granule_transpose([d[:, :LANES] for d in ds])
        hi = _granule_transpose([d[:, LANES:] for d in ds])
        for t, x in enumerate(lo + hi):
            for c in range(ncl):
                slab_ref[j, pl.ds(c * SSM_T + t, nb, stride=pitch), :] = x[c * nb:(c + 1) * nb]


def _proj_kernel(x_ref, g_ref, w_ref, qn_ref, kn_ref, q_ref, k_ref, v_ref, ug_ref, kf_ref, vf_ref, slab_ref,
                 *, first_tail_block):
    i = pl.program_id(0)
    nb, tb, d = x_ref.shape
    d_att = N_HEADS * HEAD_DIM
    pitch, _ = _regroup_geometry(nb, tb)
    h = _rms(x_ref[...].reshape(nb * tb, d), g_ref[...]).astype(BF16)
    u = _dot(h, w_ref[:, 3 * d_att:])
    for j in range(slab_ref.shape[0]):
        for b in range(nb):
            slab_ref[j, b * pitch:b * pitch + tb, :] = u[b * tb:(b + 1) * tb, j * LANES:(j + 1) * LANES]
    _tokens_to_groups(slab_ref, ug_ref, nb=nb, tb=tb)

    def head_norm(y, gain):
        return [_rms(y[:, hd * HEAD_DIM:(hd + 1) * HEAD_DIM], gain) for hd in range(N_HEADS)]

    def store_tokens(ref, y):
        for b in range(nb):
            ref[b] = y[b * tb:(b + 1) * tb].astype(ref.dtype)

    qs = head_norm(_dot(h, w_ref[:, 0:d_att]), qn_ref[...] * (HEAD_DIM ** -0.5))
    store_tokens(q_ref, jnp.concatenate(qs, axis=1))
    ks = head_norm(_dot(h, w_ref[:, d_att:2 * d_att]), kn_ref[...])
    store_tokens(k_ref, jnp.concatenate(ks, axis=1))
    v = _dot(h, w_ref[:, 2 * d_att:3 * d_att])
    store_tokens(v_ref, v)

    @pl.when(i >= first_tail_block)
    def _():
        for b in range(nb):
            rows = slice(b * tb, (b + 1) * tb)
            for hd in range(N_HEADS):
                head_rows = pl.ds(hd, tb, stride=N_HEADS)
                kf_ref[b, head_rows, :] = ks[hd][rows]
                vf_ref[b, head_rows, :] = v[rows, hd * HEAD_DIM:(hd + 1) * HEAD_DIM]


def _proj(x, g, w, qn, kn, *, tb):
    nb, seq, d = x.shape
    d_att = N_HEADS * HEAD_DIM
    d_ssm = w.shape[1] - 3 * d_att
    n_g = d_ssm // SSM_GROUP
    n_blocks = seq // tb
    tail = min(ATT_LEFT, seq)
    first_tail_block = n_blocks - tail // tb
    pitch, ncl = _regroup_geometry(nb, tb)
    tok = lambda i: (0, i, 0)
    const = lambda i: (0, 0)
    tail_map = lambda i: (0, jnp.maximum(i - first_tail_block, 0), 0)
    return pl.pallas_call(
        functools.partial(_proj_kernel, first_tail_block=first_tail_block),
        grid=(n_blocks,),
        in_specs=[
            pl.BlockSpec((nb, tb, d), tok),
            pl.BlockSpec((1, d), const),
            pl.BlockSpec(w.shape, const, pipeline_mode=pl.Buffered(1)),
            pl.BlockSpec((1, HEAD_DIM), const),
            pl.BlockSpec((1, HEAD_DIM), const),
        ],
        out_specs=[
            pl.BlockSpec((nb, tb, d_att), tok),
            pl.BlockSpec((nb, tb, d_att), tok),
            pl.BlockSpec((nb, tb, d_att), tok),
            pl.BlockSpec((n_g, ncl * nb, SSM_W), tok),
            pl.BlockSpec((nb, tb * N_HEADS, HEAD_DIM), tail_map),
            pl.BlockSpec((nb, tb * N_HEADS, HEAD_DIM), tail_map),
        ],
        out_shape=[
            jax.ShapeDtypeStruct((nb, seq, d_att), BF16),
            jax.ShapeDtypeStruct((nb, seq, d_att), BF16),
            jax.ShapeDtypeStruct((nb, seq, d_att), BF16),
            jax.ShapeDtypeStruct((n_g, (seq // SSM_T) * nb, SSM_W), BF16),
            jax.ShapeDtypeStruct((nb, tail * N_HEADS, HEAD_DIM), F32),
            jax.ShapeDtypeStruct((nb, tail * N_HEADS, HEAD_DIM), F32),
        ],
        scratch_shapes=[pltpu.VMEM((d_ssm // LANES, nb * pitch, LANES), F32)],
        compiler_params=pltpu.CompilerParams(
            dimension_semantics=("arbitrary",), vmem_limit_bytes=VMEM_LIMIT_V7X),
        name="proj",
    )(x, g, w, qn, kn)


def _softmax_pv(s, v):
    m = jnp.max(s, axis=-1, keepdims=True)
    p = jnp.exp(s - m)
    l = jnp.sum(p, axis=-1, keepdims=True)
    return _dot(p.astype(BF16), v) / l


def _qk(q, k):
    return lax.dot_general(q, k, (((1,), (1,)), ((), ())), preferred_element_type=F32)


def _attn_prompt_kernel(q_ref, k_ref, v_ref, bias_ref, o_ref, s_ref, p_ref, linv_ref, *, n_blocks):
    lead = ATT_LEFT // ATT_QB
    for i in range(min(lead, n_blocks)):
        kw = (i + 1) * ATT_QB
        rows = slice(i * ATT_QB, (i + 1) * ATT_QB)
        s = _qk(q_ref[rows, :], k_ref[0:kw, :]) + bias_ref[:, ATT_KW - kw:]
        o_ref[rows, :] = _softmax_pv(s, v_ref[0:kw, :]).astype(BF16)
    n_full = n_blocks - lead
    if n_full <= 0:
        return

    def start(i):
        return pl.multiple_of(i * ATT_QB, ATT_QB)

    def scores(i, slot):
        i = jnp.minimum(i, n_blocks - 1)
        s_ref[slot] = _qk(q_ref[pl.ds(start(i), ATT_QB), :], k_ref[pl.ds(start(i) - ATT_LEFT, ATT_KW), :])

    def softmax(slot):
        s = s_ref[slot] + bias_ref[...]
        p = jnp.exp(s - jnp.max(s, axis=-1, keepdims=True))
        p_ref[slot] = p.astype(BF16)
        linv_ref[slot] = jnp.broadcast_to(1.0 / jnp.sum(p, axis=-1, keepdims=True), linv_ref.shape[1:])

    def output(i, slot):
        v = v_ref[pl.ds(start(i) - ATT_LEFT, ATT_KW), :]
        o_ref[pl.ds(start(i), ATT_QB), :] = (_dot(p_ref[slot], v) * linv_ref[slot]).astype(BF16)

    def step(i, slot):
        scores(i + 1, 1 - slot)
        output(i - 1, 1 - slot)
        softmax(slot)

    scores(lead, 0)
    scores(lead + 1, 1)
    softmax(0)
    n_pairs = (n_full - 1) // 2

    def pair(r, carry):
        i = lead + 1 + 2 * r
        step(i, 1)
        step(i + 1, 0)
        return carry

    lax.fori_loop(0, n_pairs, pair, 0)
    last_slot = 0
    if (n_full - 1) % 2:
        step(n_blocks - 1, 1)
        last_slot = 1
    output(n_blocks - 1, last_slot)


def _attn_prompt(q, k, v, bias):
    batch, seq, d_att = q.shape
    blk = pl.BlockSpec((None, seq, HEAD_DIM), lambda b, h: (b, 0, h))
    return pl.pallas_call(
        functools.partial(_attn_prompt_kernel, n_blocks=seq // ATT_QB),
        grid=(batch, N_HEADS),
        in_specs=[blk, blk, blk, pl.BlockSpec((None, ATT_QB, ATT_KW), lambda b, h: (h, 0, 0))],
        out_specs=blk,
        out_shape=jax.ShapeDtypeStruct((batch, seq, d_att), BF16),
        scratch_shapes=[pltpu.VMEM((2, ATT_QB, ATT_KW), F32), pltpu.VMEM((2, ATT_QB, ATT_KW), BF16),
                        pltpu.VMEM((2, ATT_QB, HEAD_DIM), F32)],
        compiler_params=pltpu.CompilerParams(
            dimension_semantics=("parallel", "parallel"), vmem_limit_bytes=VMEM_LIMIT_V7X),
        name="attn_prompt",
    )(q, k, v, bias)


def _attn_sample_kernel(q_ref, kn_ref, vn_ref, kc_ref, vc_ref, bias_ref, o_ref, ko_ref, vo_ref, *, w_cache, seq):
    def head_rows(ref, n, hd):
        return ref[pl.ds(hd, n, stride=N_HEADS), :].astype(BF16)

    for hd in range(N_HEADS):
        sl = slice(hd * HEAD_DIM, (hd + 1) * HEAD_DIM)
        q = q_ref[:, sl]
        s1 = _qk(q, head_rows(kc_ref, w_cache, hd)) + bias_ref[hd, :, :w_cache]
        s2 = _qk(q, head_rows(kn_ref, seq, hd)) + bias_ref[hd, :, w_cache:]
        m = jnp.maximum(jnp.max(s1, axis=-1, keepdims=True), jnp.max(s2, axis=-1, keepdims=True))
        p1 = jnp.exp(s1 - m)
        p2 = jnp.exp(s2 - m)
        l = jnp.sum(p1, axis=-1, keepdims=True) + jnp.sum(p2, axis=-1, keepdims=True)
        o = (_dot(p1.astype(BF16), head_rows(vc_ref, w_cache, hd))
             + _dot(p2.astype(BF16), head_rows(vn_ref, seq, hd)))
        o_ref[:, sl] = (o / l).astype(BF16)

    keep = (w_cache - seq) * N_HEADS
    for new_ref, old_ref, out_ref in ((kn_ref, kc_ref, ko_ref), (vn_ref, vc_ref, vo_ref)):
        out_ref[:keep, :] = old_ref[seq * N_HEADS:, :]
        out_ref[keep:, :] = new_ref[...]


def _attn_sample(q, kn, vn, kc, vc, bias):
    batch, seq, d_att = q.shape
    w_cache = kc.shape[1] // N_HEADS
    assert seq <= w_cache
    tok = pl.BlockSpec((None, seq, d_att), lambda b: (b, 0, 0))
    new = pl.BlockSpec((None, seq * N_HEADS, HEAD_DIM), lambda b: (b, 0, 0))
    cache = pl.BlockSpec((None, w_cache * N_HEADS, HEAD_DIM), lambda b: (b, 0, 0))
    return pl.pallas_call(
        functools.partial(_attn_sample_kernel, w_cache=w_cache, seq=seq),
        grid=(batch,),
        in_specs=[tok, new, new, cache, cache, pl.BlockSpec(bias.shape, lambda b: (0, 0, 0))],
        out_specs=[tok, cache, cache],
        out_shape=[jax.ShapeDtypeStruct((batch, seq, d_att), BF16),
                   jax.ShapeDtypeStruct(kc.shape, F32), jax.ShapeDtypeStruct(vc.shape, F32)],
        compiler_params=pltpu.CompilerParams(
            dimension_semantics=("parallel",), vmem_limit_bytes=VMEM_LIMIT_V7X),
        name="attn_sample",
    )(q, kn, vn, kc, vc, bias)


def _ssm_kernel(u_ref, lag_ref, bm_ref, cm_ref, a_ref, s0_ref, y_ref, sf_ref, sl_ref, sp_ref, km_ref,
                *, rows, n_chunks, groups):
    half = 2 * SSM_STATE
    lane = lax.broadcasted_iota(jnp.int32, (SSM_GROUP, SSM_W), 1)
    for gi in range(groups):
        lag = lag_ref[gi]
        for t_in in range(SSM_T):
            shifted = lag if t_in == 0 else pltpu.roll(lag, t_in * SSM_GROUP, 1)
            km_ref[gi, t_in * SSM_GROUP:(t_in + 1) * SSM_GROUP, :] = jnp.where(
                lane >= t_in * SSM_GROUP, shifted, 0.0).astype(BF16)
        sl_ref[gi] = _dot(u_ref[gi], bm_ref[gi])

    coef = []
    for gi in range(groups):
        coef.append(tuple(jnp.broadcast_to(a_ref[gi, r:r + 1, :], (rows, half)) for r in range(3)))

    def body(c, carry):
        r0 = pl.multiple_of(c * rows, rows)
        new = []
        for gi in range(groups):
            s, sw = carry[gi]
            a1, a2, a2w = coef[gi]
            sp_ref[gi, pl.ds(r0, rows), :] = s
            loc = sl_ref[gi, pl.ds(r0, rows), :]
            new.append((a1 * s + a2 * sw + loc[:, :half], a1 * sw + a2w * s + loc[:, half:]))
        return tuple(new)

    init = tuple((s0_ref[gi, 0], s0_ref[gi, 1]) for gi in range(groups))
    last = lax.fori_loop(0, n_chunks, body, init)
    for gi in range(groups):
        sf_ref[gi] = last[gi][0]
        y = _dot(u_ref[gi], km_ref[gi]) + _dot(sp_ref[gi].astype(BF16), cm_ref[gi])
        y_ref[gi] = jax.nn.gelu(y).astype(BF16)


def _ssm(ug, lagk, bm, cm, acoef, s0, *, rows, groups):
    n_g, n_rows, _ = ug.shape
    n_chunks = n_rows // rows
    half = 2 * SSM_STATE
    g3 = lambda g: (g, 0, 0)
    return pl.pallas_call(
        functools.partial(_ssm_kernel, rows=rows, n_chunks=n_chunks, groups=groups),
        grid=(n_g // groups,),
        in_specs=[
            pl.BlockSpec((groups, n_rows, SSM_W), g3),
            pl.BlockSpec((groups, SSM_GROUP, SSM_W), g3),
            pl.BlockSpec((groups, SSM_W, 2 * half), g3),
            pl.BlockSpec((groups, half, SSM_W), g3),
            pl.BlockSpec((groups, 3, half), g3),
            pl.BlockSpec((groups, 2, rows, half), lambda g: (g, 0, 0, 0)),
        ],
        out_specs=[
            pl.BlockSpec((groups, n_rows, SSM_W), g3),
            pl.BlockSpec((groups, rows, half), g3),
        ],
        out_shape=[
            jax.ShapeDtypeStruct((n_g, n_rows, SSM_W), BF16),
            jax.ShapeDtypeStruct((n_g, rows, half), F32),
        ],
        scratch_shapes=[
            pltpu.VMEM((groups, n_rows, 2 * half), F32),
            pltpu.VMEM((groups, n_rows, half), F32),
            pltpu.VMEM((groups, SSM_W, SSM_W), BF16),
        ],
        compiler_params=pltpu.CompilerParams(
            dimension_semantics=("parallel",), vmem_limit_bytes=VMEM_LIMIT_V7X),
        name="ssm",
    )(ug, lagk, bm, cm, acoef, s0)


def _mix_kernel(att_ref, yg_ref, x_ref, wglu_ref, bglu_ref, ga_ref, gs_ref, wout_ref, o_ref, slab_ref, y_ref):
    nb, tb, d = x_ref.shape
    d_att = att_ref.shape[2]
    d_ssm = y_ref.shape[1]
    pitch, _ = _regroup_geometry(nb, tb)
    att = att_ref[...].reshape(nb * tb, d_att).astype(F32)
    mix_a = _rms(att, ga_ref[...]).astype(BF16)
    o = x_ref[...].reshape(nb * tb, d) + _dot(mix_a, wout_ref[:d_att, :])
    _groups_to_tokens(yg_ref, slab_ref, nb=nb, tb=tb)
    for j in range(slab_ref.shape[0]):
        for b in range(nb):
            y_ref[b * tb:(b + 1) * tb, j * LANES:(j + 1) * LANES] = slab_ref[j, b * pitch:b * pitch + tb, :].astype(BF16)
    glu = _dot(y_ref[...], wglu_ref[...]) + bglu_ref[...]
    ssm_out = glu[:, :d_ssm] * jax.nn.sigmoid(glu[:, d_ssm:])
    mix_s = _rms(ssm_out, gs_ref[...]).astype(BF16)
    o = o + _dot(mix_s, wout_ref[d_att:, :])
    o_ref[...] = o.reshape(nb, tb, d)


def _mix(att, yg, x, wglu, bglu, ga, gs, wout, *, tb):
    nb, seq, d = x.shape
    d_att = att.shape[2]
    n_g = yg.shape[0]
    d_ssm = n_g * SSM_GROUP
    pitch, ncl = _regroup_geometry(nb, tb)
    tok = lambda i: (0, i, 0)
    const = lambda i: (0, 0)
    once = lambda a: pl.BlockSpec(a.shape, const, pipeline_mode=pl.Buffered(1))
    return pl.pallas_call(
        _mix_kernel,
        grid=(seq // tb,),
        in_specs=[
            pl.BlockSpec((nb, tb, d_att), tok),
            pl.BlockSpec((n_g, ncl * nb, SSM_W), tok),
            pl.BlockSpec((nb, tb, d), tok),
            once(wglu), once(bglu), once(ga), once(gs), once(wout),
        ],
        out_specs=pl.BlockSpec((nb, tb, d), tok),
        out_shape=jax.ShapeDtypeStruct((nb, seq, d), F32),
        scratch_shapes=[pltpu.VMEM((d_ssm // LANES, nb * pitch, LANES), F32),
                        pltpu.VMEM((nb * tb, d_ssm), BF16)],
        compiler_params=pltpu.CompilerParams(
            dimension_semantics=("parallel",), vmem_limit_bytes=VMEM_LIMIT_V7X),
        name="mix",
    )(att, yg, x, wglu, bglu, ga, gs, wout)


def _ssm_matrices(lam_re, lam_im, log_dt, b_re, b_im, c_re, c_im, d_skip):
    hp = lax.Precision.HIGHEST
    n_g = lam_re.shape[0]
    dt = jnp.exp(log_dt)[:, None]
    n = jnp.arange(SSM_T + 1, dtype=F32)[:, None, None]
    mag = jnp.exp(lam_re * dt * n)
    ang = lam_im * dt * n
    pw_re, pw_im = mag * jnp.cos(ang), mag * jnp.sin(ang)
    x, y = pw_re[1] - 1.0, pw_im[1]
    den = lam_re * lam_re + lam_im * lam_im
    z_re, z_im = (x * lam_re + y * lam_im) / den, (y * lam_re - x * lam_im) / den
    bb_re = z_re[..., None] * b_re - z_im[..., None] * b_im
    bb_im = z_re[..., None] * b_im + z_im[..., None] * b_re
    pb_re = pw_re[..., None] * bb_re - pw_im[..., None] * bb_im
    pb_im = pw_re[..., None] * bb_im + pw_im[..., None] * bb_re
    lagk = (jnp.einsum('gop,dgpi->gido', c_re, pb_re[:SSM_T], precision=hp)
            - jnp.einsum('gop,dgpi->gido', c_im, pb_im[:SSM_T], precision=hp))
    skip = jnp.eye(SSM_GROUP, dtype=F32)[None] * d_skip[:, :, None]
    lagk = lagk.at[:, :, 0, :].add(skip).reshape(n_g, SSM_GROUP, SSM_W)
    inj_re = pb_re[:SSM_T][::-1].transpose(1, 0, 3, 2).reshape(n_g, SSM_W, SSM_STATE)
    inj_im = pb_im[:SSM_T][::-1].transpose(1, 0, 3, 2).reshape(n_g, SSM_W, SSM_STATE)
    bm = jnp.concatenate([inj_re, inj_im, inj_im, inj_re], axis=-1)
    cp_re = c_re[None] * pw_re[1:, :, None, :] - c_im[None] * pw_im[1:, :, None, :]
    cp_im = c_re[None] * pw_im[1:, :, None, :] + c_im[None] * pw_re[1:, :, None, :]
    out_re = cp_re.transpose(1, 3, 0, 2).reshape(n_g, SSM_STATE, SSM_W)
    out_im = -cp_im.transpose(1, 3, 0, 2).reshape(n_g, SSM_STATE, SSM_W)
    cm = jnp.concatenate([out_re, out_im], axis=1)
    ar, ai = pw_re[SSM_T], pw_im[SSM_T]
    acoef = jnp.stack([jnp.concatenate([ar, ar], -1), jnp.concatenate([-ai, ai], -1),
                       jnp.concatenate([ai, -ai], -1)], axis=1)
    return lagk, bm.astype(BF16), cm.astype(BF16), acoef


def _bias_table(rel_bias):
    n_heads = rel_bias.shape[0]
    ext = ATT_QB + ATT_KW
    n_edge = ATT_LEFT - REL_CLIP + 1
    assert ATT_KW - n_edge == 2 * REL_CLIP - 1 and ATT_LEFT >= REL_CLIP
    far = rel_bias[:, 2 * REL_CLIP:]
    row = jnp.concatenate([jnp.broadcast_to(far, (n_heads, n_edge)), rel_bias[:, 1:2 * REL_CLIP][:, ::-1],
                           jnp.broadcast_to(far, (n_heads, ATT_QB))], axis=1)
    skew = jnp.tile(row, (1, ATT_QB))[:, :ATT_QB * (ext - 1)].reshape(n_heads, ATT_QB, ext - 1)
    return skew[:, :, :ATT_KW].astype(F32)


def _band_mask():
    r = jnp.arange(ATT_QB)[:, None] // CHUNK
    j = jnp.arange(ATT_KW)[None, :] // CHUNK
    return (j >= r) & (j <= r + LEFT_CHUNKS)


def _state_rows(s_re, s_im):
    s = jnp.concatenate([s_re, s_im], -1).transpose(1, 0, 2)
    sw = jnp.concatenate([s_im, s_re], -1).transpose(1, 0, 2)
    return jnp.stack([s, sw], axis=1)


def _cast_pad_kernel(*refs, scale):
    n = len(refs) // 2
    for w_ref, o_ref in zip(refs[:n], refs[n:]):
        r, c = w_ref.shape
        w = w_ref[...]
        o_ref[:r, :c] = (w if scale == 1.0 else w * scale).astype(BF16)
        if o_ref.shape[0] > r:
            o_ref[r:, :] = jnp.zeros((o_ref.shape[0] - r, o_ref.shape[1]), BF16)
        if o_ref.shape[1] > c:
            o_ref[:, c:] = jnp.zeros((o_ref.shape[0], o_ref.shape[1] - c), BF16)


def _cast_pad(ws, *, axis, mult, block, scale=1.0):
    r, c = ws[0].shape
    padded = -(-ws[0].shape[axis] // mult) * mult
    if axis == 1:
        in_spec, out_spec = pl.BlockSpec((block, c), lambda i: (i, 0)), pl.BlockSpec((block, padded), lambda i: (i, 0))
        out_shape, steps = jax.ShapeDtypeStruct((r, padded), BF16), r // block
    else:
        in_spec, out_spec = pl.BlockSpec((r, block), lambda i: (0, i)), pl.BlockSpec((padded, block), lambda i: (0, i))
        out_shape, steps = jax.ShapeDtypeStruct((padded, c), BF16), c // block
    return pl.pallas_call(
        functools.partial(_cast_pad_kernel, scale=scale),
        grid=(steps,),
        in_specs=[in_spec] * len(ws),
        out_specs=[out_spec] * len(ws),
        out_shape=[out_shape] * len(ws),
        compiler_params=pltpu.CompilerParams(
            dimension_semantics=("parallel",), vmem_limit_bytes=VMEM_LIMIT_V7X),
        name="cast_pad",
    )(*ws)


def _stream(x, p, *, tm, tf, tb, ssm_groups, cache=None):
    batch, seq, d = x.shape
    ffn = functools.partial(_ffn, tm=tm, tf=tf)
    x1 = ffn(x.reshape(batch * seq, d), p['g_ffn1'], p['ffn1_wg'], p['ffn1_wu'], p['ffn1_wd'], p['g_final'],
             final_norm=False).reshape(batch, seq, d)
    q, k, v, ug, kf, vf = _proj(x1, p['g_mix'], p['w_in'], p['q_norm'], p['k_norm'], tb=tb)
    n_g = ug.shape[0]
    if cache is None:
        att = _attn_prompt(q, k, v, p['bias_prompt'])
        s0 = jnp.zeros((n_g, 2, batch, 2 * SSM_STATE), F32)
    else:
        ck, cv, s_re, s_im = cache
        w_cache = ck.shape[1]
        assert w_cache == ATT_LEFT and seq <= ATT_QB
        att, kf, vf = _attn_sample(q, kf, vf, ck.reshape(batch, w_cache * N_HEADS, HEAD_DIM),
                                   cv.reshape(batch, w_cache * N_HEADS, HEAD_DIM),
                                   p['bias_table'][:, :seq, :w_cache + seq])
        s0 = _state_rows(s_re, s_im)
    yg, sf = _ssm(ug, p['lagk'], p['bm'], p['cm'], p['acoef'], s0, rows=batch, groups=ssm_groups)
    x2 = _mix(att, yg, x1, p['w_glu'], p['b_glu'], p['g_att'], p['g_ssm'], p['w_out'], tb=tb)
    y = ffn(x2.reshape(batch * seq, d), p['g_ffn2'], p['ffn2_wg'], p['ffn2_wu'], p['ffn2_wd'], p['g_final'],
            final_norm=True).reshape(batch, seq, d)
    sf = sf.transpose(1, 0, 2)
    return y, kf, vf, sf[..., :SSM_STATE], sf[..., SSM_STATE:]


def kernel(x_prompt, x_sample, cache_attn_k, cache_attn_v, state_ssm_re, state_ssm_im, norm_ffn1, ffn1_w_gate, ffn1_w_up, ffn1_w_down, norm_mix, w_in, q_norm, k_norm, rel_bias, ssm_lambda_re, ssm_lambda_im, ssm_log_dt, ssm_b_re, ssm_b_im, ssm_c_re, ssm_c_im, ssm_d, w_glu, b_glu, norm_att_out, norm_ssm_out, w_out, norm_ffn2, ffn2_w_gate, ffn2_w_up, ffn2_w_down, norm_final):
    depth = norm_ffn1.shape[0]
    bs, ls, _ = x_sample.shape
    yp, ys = x_prompt, x_sample
    outs = [[] for _ in range(8)]
    for l in range(depth):
        lagk, bm, cm, acoef = _ssm_matrices(ssm_lambda_re[l], ssm_lambda_im[l], ssm_log_dt[l], ssm_b_re[l],
                                          ssm_b_im[l], ssm_c_re[l], ssm_c_im[l], ssm_d[l])
        row = lambda a: a[l][None, :]
        bias_table = _bias_table(rel_bias[l])
        wg1, wu1, wg2, wu2 = _cast_pad([ffn1_w_gate[l], ffn1_w_up[l], ffn2_w_gate[l], ffn2_w_up[l]],
                                       axis=1, mult=FF_PAD, block=128)
        wd1, wd2 = _cast_pad([ffn1_w_down[l], ffn2_w_down[l]], axis=0, mult=FF_PAD, block=256, scale=0.5)
        p = dict(
            g_ffn1=row(norm_ffn1), g_mix=row(norm_mix), g_att=row(norm_att_out), g_ssm=row(norm_ssm_out),
            g_ffn2=row(norm_ffn2), g_final=row(norm_final), q_norm=row(q_norm), k_norm=row(k_norm),
            ffn1_wg=wg1, ffn1_wu=wu1, ffn1_wd=wd1, ffn2_wg=wg2, ffn2_wu=wu2, ffn2_wd=wd2,
            w_in=w_in[l].astype(BF16), w_glu=w_glu[l].astype(BF16), b_glu=row(b_glu), w_out=w_out[l].astype(BF16),
            bias_table=bias_table, bias_prompt=jnp.where(_band_mask()[None], bias_table, NEG_INF),
            lagk=lagk, bm=bm, cm=cm, acoef=acoef,
        )
        yp, kp, vp, rp, ip = _stream(yp, p, tm=1024, tf=FF_PAD, tb=CHUNK, ssm_groups=4)
        ys, kd, vd, rd, idd = _stream(ys, p, tm=bs * ls, tf=FF_PAD, tb=ls, ssm_groups=8,
                                      cache=(cache_attn_k[l], cache_attn_v[l], state_ssm_re[l], state_ssm_im[l]))
        as_cache = lambda a: a.reshape(a.shape[0], -1, N_HEADS, HEAD_DIM)
        for lst, val in zip(outs, (as_cache(kp), as_cache(vp), rp, ip, as_cache(kd), as_cache(vd), rd, idd)):
            lst.append(val)
    return (yp, ys) + tuple(jnp.stack(o) for o in outs)
```

```python
import functools

import jax
import jax.numpy as jnp
from jax import lax
from jax.experimental import pallas as pl
from jax.experimental.pallas import tpu as pltpu

EPS = 1e-6
NEG_INF = -1e30
CHUNK = 64
LEFT_CHUNKS = 8
ATT_LEFT = LEFT_CHUNKS * CHUNK
REL_CLIP = 256
N_HEADS = 8
HEAD_DIM = 128
SSM_GROUP = 16
SSM_STATE = 64
LANES = 128
SSM_T = 16
SSM_W = SSM_T * SSM_GROUP
GROUPS_PER_TILE = LANES // SSM_GROUP
ATT_QB = 256
ATT_KW = ATT_QB + ATT_LEFT
ATT_HEADS_PER_STEP = 4
REGROUP_PAD = 8
FF_PAD = 512
VMEM_LIMIT_V7X = 60 * 1024 * 1024

BF16 = jnp.bfloat16
F32 = jnp.float32


def _dot(a, b):
    return jnp.dot(a, b, preferred_element_type=F32)


def _rms(x, g):
    return x * lax.rsqrt(jnp.mean(x * x, axis=-1, keepdims=True) + EPS) * g


def _ffn_kernel(x_ref, g_ref, wg_hbm, wu_hbm, wd_hbm, gf_ref, o_ref, xn_ref, wg_buf, wu_buf, wd_buf, sem,
                *, tf, n_chunks, final_norm):
    i = pl.program_id(0)
    first_slot = (i * n_chunks) % 2

    def chunk_copies(j, slot):
        cols = pl.ds(pl.multiple_of(j * tf, tf), tf)
        return (pltpu.make_async_copy(wg_hbm.at[:, cols], wg_buf.at[slot], sem.at[0, slot]),
                pltpu.make_async_copy(wu_hbm.at[:, cols], wu_buf.at[slot], sem.at[1, slot]),
                pltpu.make_async_copy(wd_hbm.at[cols, :], wd_buf.at[slot], sem.at[2, slot]))

    @pl.when(i == 0)
    def _():
        for c in chunk_copies(0, 0):
            c.start()

    x = x_ref[...]
    xn_ref[...] = _rms(x, g_ref[...]).astype(BF16)
    o_ref[...] = x

    def body(j, carry):
        slot = (first_slot + j) % 2
        nxt = jnp.where(j + 1 == n_chunks, 0, j + 1)
        for c in chunk_copies(nxt, 1 - slot):
            c.start()
        for c in chunk_copies(j, slot):
            c.wait()
        xn = xn_ref[...]
        a = _dot(xn, wg_buf[slot])
        b = _dot(xn, wu_buf[slot])
        h = (a * jax.nn.sigmoid(a) * b).astype(BF16)
        o_ref[...] += _dot(h, wd_buf[slot])
        return carry

    lax.fori_loop(0, n_chunks, body, 0)

    @pl.when(i == pl.num_programs(0) - 1)
    def _():
        for c in chunk_copies(0, (first_slot + n_chunks) % 2):
            c.wait()

    if final_norm:
        o_ref[...] = _rms(o_ref[...], gf_ref[...])


def _ffn(x, g, wg, wu, wd_half, gf, *, tm, tf, final_norm):
    n, d = x.shape
    fpad = wg.shape[1]
    hbm = pl.BlockSpec(memory_space=pl.ANY)
    return pl.pallas_call(
        functools.partial(_ffn_kernel, tf=tf, n_chunks=fpad // tf, final_norm=final_norm),
        grid=(n // tm,),
        in_specs=[
            pl.BlockSpec((tm, d), lambda i: (i, 0)),
            pl.BlockSpec((1, d), lambda i: (0, 0)),
            hbm, hbm, hbm,
            pl.BlockSpec((1, d), lambda i: (0, 0)),
        ],
        out_specs=pl.BlockSpec((tm, d), lambda i: (i, 0)),
        out_shape=jax.ShapeDtypeStruct((n, d), F32),
        scratch_shapes=[pltpu.VMEM((tm, d), BF16),
                        pltpu.VMEM((2, d, tf), BF16), pltpu.VMEM((2, d, tf), BF16), pltpu.VMEM((2, tf, d), BF16),
                        pltpu.SemaphoreType.DMA((3, 2))],
        compiler_params=pltpu.CompilerParams(
            dimension_semantics=("arbitrary",), vmem_limit_bytes=VMEM_LIMIT_V7X),
        name="ffn",
    )(x, g, wg, wu, wd_half, gf)


def _granule_transpose(vs):
    gran = lax.broadcasted_iota(jnp.int32, vs[0].shape, 1) // SSM_GROUP
    for s in (4, 2, 1):
        upper = (gran & s) != 0
        new = list(vs)
        for a in range(GROUPS_PER_TILE):
            if a & s == 0:
                lo, hi = vs[a], vs[a + s]
                new[a] = jnp.where(upper, pltpu.roll(hi, s * SSM_GROUP, 1), lo)
                new[a + s] = jnp.where(upper, hi, pltpu.roll(lo, LANES - s * SSM_GROUP, 1))
        vs = new
    return vs


def _regroup_geometry(nb, tb):
    return tb + REGROUP_PAD, tb // SSM_T


def _tokens_to_groups(slab_ref, ug_ref, *, nb, tb):
    pitch, ncl = _regroup_geometry(nb, tb)
    for j in range(slab_ref.shape[0]):
        xs = [jnp.concatenate([slab_ref[j, pl.ds(c * SSM_T + t, nb, stride=pitch), :] for c in range(ncl)], axis=0)
              for t in range(SSM_T)]
        lo = _granule_transpose(xs[:GROUPS_PER_TILE])
        hi = _granule_transpose(xs[GROUPS_PER_TILE:])
        for gl in range(GROUPS_PER_TILE):
            ug_ref[j * GROUPS_PER_TILE + gl] = jnp.concatenate([lo[gl], hi[gl]], axis=1).astype(ug_ref.dtype)


def _groups_to_tokens(yg_ref, slab_ref, *, nb, tb):
    pitch, ncl = _regroup_geometry(nb, tb)
    for j in range(slab_ref.shape[0]):
        ds = [yg_ref[j * GROUPS_PER_TILE + gl].astype(F32) for gl in range(GROUPS_PER_TILE)]
        lo = _granule_transpose([d[:, :LANES] for d in ds])
        hi = _granule_transpose([d[:, LANES:] for d in ds])
        for t, x in enumerate(lo + hi):
            for c in range(ncl):
                slab_ref[j, pl.ds(c * SSM_T + t, nb, stride=pitch), :] = x[c * nb:(c + 1) * nb]


def _proj_kernel(x_ref, g_ref, w_ref, qn_ref, kn_ref, q_ref, k_ref, v_ref, ug_ref, kf_ref, vf_ref, slab_ref,
                 *, first_tail_block):
    i = pl.program_id(0)
    nb, tb, d = x_ref.shape
    d_att = N_HEADS * HEAD_DIM
    pitch, _ = _regroup_geometry(nb, tb)
    h = _rms(x_ref[...].reshape(nb * tb, d), g_ref[...]).astype(BF16)
    u = _dot(h, w_ref[:, 3 * d_att:])
    for j in range(slab_ref.shape[0]):
        for b in range(nb):
            slab_ref[j, b * pitch:b * pitch + tb, :] = u[b * tb:(b + 1) * tb, j * LANES:(j + 1) * LANES]
    _tokens_to_groups(slab_ref, ug_ref, nb=nb, tb=tb)

    def head_norm(y, gain):
        return [_rms(y[:, hd * HEAD_DIM:(hd + 1) * HEAD_DIM], gain) for hd in range(N_HEADS)]

    def store_tokens(ref, y):
        for b in range(nb):
            ref[b] = y[b * tb:(b + 1) * tb].astype(ref.dtype)

    qs = head_norm(_dot(h, w_ref[:, 0:d_att]), qn_ref[...] * (HEAD_DIM ** -0.5))
    store_tokens(q_ref, jnp.concatenate(qs, axis=1))
    ks = head_norm(_dot(h, w_ref[:, d_att:2 * d_att]), kn_ref[...])
    store_tokens(k_ref, jnp.concatenate(ks, axis=1))
    v = _dot(h, w_ref[:, 2 * d_att:3 * d_att])
    store_tokens(v_ref, v)

    @pl.when(i >= first_tail_block)
    def _():
        for b in range(nb):
            rows = slice(b * tb, (b + 1) * tb)
            for hd in range(N_HEADS):
                head_rows = pl.ds(hd, tb, stride=N_HEADS)
                kf_ref[b, head_rows, :] = ks[hd][rows]
                vf_ref[b, head_rows, :] = v[rows, hd * HEAD_DIM:(hd + 1) * HEAD_DIM]


def _proj(x, g, w, qn, kn, *, tb):
    nb, seq, d = x.shape
    d_att = N_HEADS * HEAD_DIM
    d_ssm = w.shape[1] - 3 * d_att
    n_g = d_ssm // SSM_GROUP
    n_blocks = seq // tb
    tail = min(ATT_LEFT, seq)
    first_tail_block = n_blocks - tail // tb
    pitch, ncl = _regroup_geometry(nb, tb)
    tok = lambda i: (0, i, 0)
    const = lambda i: (0, 0)
    tail_map = lambda i: (0, jnp.maximum(i - first_tail_block, 0), 0)
    return pl.pallas_call(
        functools.partial(_proj_kernel, first_tail_block=first_tail_block),
        grid=(n_blocks,),
        in_specs=[
            pl.BlockSpec((nb, tb, d), tok),
            pl.BlockSpec((1, d), const),
            pl.BlockSpec(w.shape, const, pipeline_mode=pl.Buffered(1)),
            pl.BlockSpec((1, HEAD_DIM), const),
            pl.BlockSpec((1, HEAD_DIM), const),
        ],
        out_specs=[
            pl.BlockSpec((nb, tb, d_att), tok),
            pl.BlockSpec((nb, tb, d_att), tok),
            pl.BlockSpec((nb, tb, d_att), tok),
            pl.BlockSpec((n_g, ncl * nb, SSM_W), tok),
            pl.BlockSpec((nb, tb * N_HEADS, HEAD_DIM), tail_map),
            pl.BlockSpec((nb, tb * N_HEADS, HEAD_DIM), tail_map),
        ],
        out_shape=[
            jax.ShapeDtypeStruct((nb, seq, d_att), BF16),
            jax.ShapeDtypeStruct((nb, seq, d_att), BF16),
            jax.ShapeDtypeStruct((nb, seq, d_att), BF16),
            jax.ShapeDtypeStruct((n_g, (seq // SSM_T) * nb, SSM_W), BF16),
            jax.ShapeDtypeStruct((nb, tail * N_HEADS, HEAD_DIM), F32),
            jax.ShapeDtypeStruct((nb, tail * N_HEADS, HEAD_DIM), F32),
        ],
        scratch_shapes=[pltpu.VMEM((d_ssm // LANES, nb * pitch, LANES), F32)],
        compiler_params=pltpu.CompilerParams(
            dimension_semantics=("arbitrary",), vmem_limit_bytes=VMEM_LIMIT_V7X),
        name="proj",
    )(x, g, w, qn, kn)


def _softmax_pv(s, v):
    m = jnp.max(s, axis=-1, keepdims=True)
    p = jnp.exp(s - m)
    l = jnp.sum(p, axis=-1, keepdims=True)
    return _dot(p.astype(BF16), v) / l


def _qk(q, k):
    return lax.dot_general(q, k, (((1,), (1,)), ((), ())), preferred_element_type=F32)


def _attn_prompt_kernel(q_ref, k_ref, v_ref, bias_ref, o_ref, s_ref, p_ref, linv_ref, *, n_blocks):
    lead = ATT_LEFT // ATT_QB
    n_full = n_blocks - lead

    def lanes(h):
        return slice(h * HEAD_DIM, (h + 1) * HEAD_DIM)

    def start(i):
        return pl.multiple_of(i * ATT_QB, ATT_QB)

    for i in range(min(lead, n_blocks)):
        kw = (i + 1) * ATT_QB
        rows = slice(i * ATT_QB, (i + 1) * ATT_QB)
        ss = [_qk(q_ref[rows, lanes(h)], k_ref[0:kw, lanes(h)]) for h in range(ATT_HEADS_PER_STEP)]
        for h in range(ATT_HEADS_PER_STEP):
            s = ss[h] + bias_ref[h, :, ATT_KW - kw:]
            o_ref[rows, lanes(h)] = _softmax_pv(s, v_ref[0:kw, lanes(h)]).astype(BF16)
    if n_full <= 0:
        return

    def scores(h, i, slot):
        s_ref[slot] = _qk(q_ref[pl.ds(start(i), ATT_QB), lanes(h)],
                          k_ref[pl.ds(start(i) - ATT_LEFT, ATT_KW), lanes(h)])

    def softmax(h, slot):
        s = s_ref[slot] + bias_ref[h]
        p = jnp.exp(s - jnp.max(s, axis=-1, keepdims=True))
        p_ref[slot] = p.astype(BF16)
        linv_ref[slot] = jnp.broadcast_to(1.0 / jnp.sum(p, axis=-1, keepdims=True), linv_ref.shape[1:])

    def output(h, i, slot):
        v = v_ref[pl.ds(start(i) - ATT_LEFT, ATT_KW), lanes(h)]
        o_ref[pl.ds(start(i), ATT_QB), lanes(h)] = (_dot(p_ref[slot], v) * linv_ref[slot]).astype(BF16)

    for h0 in range(0, ATT_HEADS_PER_STEP, 2):
        h1 = h0 + 1
        scores(h0, lead, 0)
        scores(h1, lead, 1)
        softmax(h0, 0)

        def body(r, carry):
            i = lead + r
            scores(h0, i + 1, 0)
            output(h0, i, 0)
            softmax(h1, 1)
            scores(h1, i + 1, 1)
            output(h1, i, 1)
            softmax(h0, 0)
            return carry

        lax.fori_loop(0, n_full - 1, body, 0)
        output(h0, n_blocks - 1, 0)
        softmax(h1, 1)
        output(h1, n_blocks - 1, 1)


def _attn_prompt(q, k, v, bias):
    batch, seq, d_att = q.shape
    hs = ATT_HEADS_PER_STEP
    blk = pl.BlockSpec((None, seq, hs * HEAD_DIM), lambda b, h: (b, 0, h))
    return pl.pallas_call(
        functools.partial(_attn_prompt_kernel, n_blocks=seq // ATT_QB),
        grid=(batch, N_HEADS // hs),
        in_specs=[blk, blk, blk, pl.BlockSpec((hs, ATT_QB, ATT_KW), lambda b, h: (h, 0, 0))],
        out_specs=blk,
        out_shape=jax.ShapeDtypeStruct((batch, seq, d_att), BF16),
        scratch_shapes=[pltpu.VMEM((2, ATT_QB, ATT_KW), F32), pltpu.VMEM((2, ATT_QB, ATT_KW), BF16),
                        pltpu.VMEM((2, ATT_QB, HEAD_DIM), F32)],
        compiler_params=pltpu.CompilerParams(
            dimension_semantics=("parallel", "parallel"), vmem_limit_bytes=VMEM_LIMIT_V7X),
        name="attn_prompt",
    )(q, k, v, bias)


def _attn_sample_kernel(q_ref, kn_ref, vn_ref, kc_ref, vc_ref, bias_ref, o_ref, ko_ref, vo_ref, *, w_cache, seq):
    def head_rows(ref, n, hd):
        return ref[pl.ds(hd, n, stride=N_HEADS), :].astype(BF16)

    for hd in range(N_HEADS):
        sl = slice(hd * HEAD_DIM, (hd + 1) * HEAD_DIM)
        q = q_ref[:, sl]
        s1 = _qk(q, head_rows(kc_ref, w_cache, hd)) + bias_ref[hd, :, :w_cache]
        s2 = _qk(q, head_rows(kn_ref, seq, hd)) + bias_ref[hd, :, w_cache:]
        m = jnp.maximum(jnp.max(s1, axis=-1, keepdims=True), jnp.max(s2, axis=-1, keepdims=True))
        p1 = jnp.exp(s1 - m)
        p2 = jnp.exp(s2 - m)
        l = jnp.sum(p1, axis=-1, keepdims=True) + jnp.sum(p2, axis=-1, keepdims=True)
        o = (_dot(p1.astype(BF16), head_rows(vc_ref, w_cache, hd))
             + _dot(p2.astype(BF16), head_rows(vn_ref, seq, hd)))
        o_ref[:, sl] = (o / l).astype(BF16)

    keep = (w_cache - seq) * N_HEADS
    for new_ref, old_ref, out_ref in ((kn_ref, kc_ref, ko_ref), (vn_ref, vc_ref, vo_ref)):
        out_ref[:keep, :] = old_ref[seq * N_HEADS:, :]
        out_ref[keep:, :] = new_ref[...]


def _attn_sample(q, kn, vn, kc, vc, bias):
    batch, seq, d_att = q.shape
    w_cache = kc.shape[1] // N_HEADS
    assert seq <= w_cache
    tok = pl.BlockSpec((None, seq, d_att), lambda b: (b, 0, 0))
    new = pl.BlockSpec((None, seq * N_HEADS, HEAD_DIM), lambda b: (b, 0, 0))
    cache = pl.BlockSpec((None, w_cache * N_HEADS, HEAD_DIM), lambda b: (b, 0, 0))
    return pl.pallas_call(
        functools.partial(_attn_sample_kernel, w_cache=w_cache, seq=seq),
        grid=(batch,),
        in_specs=[tok, new, new, cache, cache, pl.BlockSpec(bias.shape, lambda b: (0, 0, 0))],
        out_specs=[tok, cache, cache],
        out_shape=[jax.ShapeDtypeStruct((batch, seq, d_att), BF16),
                   jax.ShapeDtypeStruct(kc.shape, F32), jax.ShapeDtypeStruct(vc.shape, F32)],
        compiler_params=pltpu.CompilerParams(
            dimension_semantics=("parallel",), vmem_limit_bytes=VMEM_LIMIT_V7X),
        name="attn_sample",
    )(q, kn, vn, kc, vc, bias)


def _ssm_kernel(u_ref, lag_ref, bm_ref, cm_ref, a_ref, s0_ref, y_ref, sf_ref, sl_ref, sp_ref, km_ref,
                *, rows, n_chunks, groups):
    half = 2 * SSM_STATE
    lane = lax.broadcasted_iota(jnp.int32, (SSM_GROUP, SSM_W), 1)
    for gi in range(groups):
        lag = lag_ref[gi]
        for t_in in range(SSM_T):
            shifted = lag if t_in == 0 else pltpu.roll(lag, t_in * SSM_GROUP, 1)
            km_ref[gi, t_in * SSM_GROUP:(t_in + 1) * SSM_GROUP, :] = jnp.where(
                lane >= t_in * SSM_GROUP, shifted, 0.0).astype(BF16)
        sl_ref[gi] = _dot(u_ref[gi], bm_ref[gi])

    coef = []
    for gi in range(groups):
        coef.append(tuple(jnp.broadcast_to(a_ref[gi, r:r + 1, :], (rows, half)) for r in range(3)))

    def body(c, carry):
        r0 = pl.multiple_of(c * rows, rows)
        new = []
        for gi in range(groups):
            s, sw = carry[gi]
            a1, a2, a2w = coef[gi]
            sp_ref[gi, pl.ds(r0, rows), :] = s
            loc = sl_ref[gi, pl.ds(r0, rows), :]
            new.append((a1 * s + a2 * sw + loc[:, :half], a1 * sw + a2w * s + loc[:, half:]))
        return tuple(new)

    init = tuple((s0_ref[gi, 0], s0_ref[gi, 1]) for gi in range(groups))
    last = lax.fori_loop(0, n_chunks, body, init)
    for gi in range(groups):
        sf_ref[gi] = last[gi][0]
        y = _dot(u_ref[gi], km_ref[gi]) + _dot(sp_ref[gi].astype(BF16), cm_ref[gi])
        y_ref[gi] = jax.nn.gelu(y).astype(BF16)


def _ssm(ug, lagk, bm, cm, acoef, s0, *, rows, groups):
    n_g, n_rows, _ = ug.shape
    n_chunks = n_rows // rows
    half = 2 * SSM_STATE
    g3 = lambda g: (g, 0, 0)
    return pl.pallas_call(
        functools.partial(_ssm_kernel, rows=rows, n_chunks=n_chunks, groups=groups),
        grid=(n_g // groups,),
        in_specs=[
            pl.BlockSpec((groups, n_rows, SSM_W), g3),
            pl.BlockSpec((groups, SSM_GROUP, SSM_W), g3),
            pl.BlockSpec((groups, SSM_W, 2 * half), g3),
            pl.BlockSpec((groups, half, SSM_W), g3),
            pl.BlockSpec((groups, 3, half), g3),
            pl.BlockSpec((groups, 2, rows, half), lambda g: (g, 0, 0, 0)),
        ],
        out_specs=[
            pl.BlockSpec((groups, n_rows, SSM_W), g3),
            pl.BlockSpec((groups, rows, half), g3),
        ],
        out_shape=[
            jax.ShapeDtypeStruct((n_g, n_rows, SSM_W), BF16),
            jax.ShapeDtypeStruct((n_g, rows, half), F32),
        ],
        scratch_shapes=[
            pltpu.VMEM((groups, n_rows, 2 * half), F32),
            pltpu.VMEM((groups, n_rows, half), F32),
            pltpu.VMEM((groups, SSM_W, SSM_W), BF16),
        ],
        compiler_params=pltpu.CompilerParams(
            dimension_semantics=("parallel",), vmem_limit_bytes=VMEM_LIMIT_V7X),
        name="ssm",
    )(ug, lagk, bm, cm, acoef, s0)


def _mix_kernel(att_ref, yg_ref, x_ref, wglu_ref, bglu_ref, ga_ref, gs_ref, wout_ref, o_ref, slab_ref, y_ref):
    nb, tb, d = x_ref.shape
    d_att = att_ref.shape[2]
    d_ssm = y_ref.shape[1]
    pitch, _ = _regroup_geometry(nb, tb)
    att = att_ref[...].reshape(nb * tb, d_att).astype(F32)
    mix_a = _rms(att, ga_ref[...]).astype(BF16)
    o = x_ref[...].reshape(nb * tb, d) + _dot(mix_a, wout_ref[:d_att, :])
    _groups_to_tokens(yg_ref, slab_ref, nb=nb, tb=tb)
    for j in range(slab_ref.shape[0]):
        for b in range(nb):
            y_ref[b * tb:(b + 1) * tb, j * LANES:(j + 1) * LANES] = slab_ref[j, b * pitch:b * pitch + tb, :].astype(BF16)
    glu = _dot(y_ref[...], wglu_ref[...]) + bglu_ref[...]
    ssm_out = glu[:, :d_ssm] * jax.nn.sigmoid(glu[:, d_ssm:])
    mix_s = _rms(ssm_out, gs_ref[...]).astype(BF16)
    o = o + _dot(mix_s, wout_ref[d_att:, :])
    o_ref[...] = o.reshape(nb, tb, d)


def _mix(att, yg, x, wglu, bglu, ga, gs, wout, *, tb):
    nb, seq, d = x.shape
    d_att = att.shape[2]
    n_g = yg.shape[0]
    d_ssm = n_g * SSM_GROUP
    pitch, ncl = _regroup_geometry(nb, tb)
    tok = lambda i: (0, i, 0)
    const = lambda i: (0, 0)
    once = lambda a: pl.BlockSpec(a.shape, const, pipeline_mode=pl.Buffered(1))
    return pl.pallas_call(
        _mix_kernel,
        grid=(seq // tb,),
        in_specs=[
            pl.BlockSpec((nb, tb, d_att), tok),
            pl.BlockSpec((n_g, ncl * nb, SSM_W), tok),
            pl.BlockSpec((nb, tb, d), tok),
            once(wglu), once(bglu), once(ga), once(gs), once(wout),
        ],
        out_specs=pl.BlockSpec((nb, tb, d), tok),
        out_shape=jax.ShapeDtypeStruct((nb, seq, d), F32),
        scratch_shapes=[pltpu.VMEM((d_ssm // LANES, nb * pitch, LANES), F32),
                        pltpu.VMEM((nb * tb, d_ssm), BF16)],
        compiler_params=pltpu.CompilerParams(
            dimension_semantics=("parallel",), vmem_limit_bytes=VMEM_LIMIT_V7X),
        name="mix",
    )(att, yg, x, wglu, bglu, ga, gs, wout)


def _ssm_matrices(lam_re, lam_im, log_dt, b_re, b_im, c_re, c_im, d_skip):
    hp = lax.Precision.HIGHEST
    n_g = lam_re.shape[0]
    dt = jnp.exp(log_dt)[:, None]
    n = jnp.arange(SSM_T + 1, dtype=F32)[:, None, None]
    mag = jnp.exp(lam_re * dt * n)
    ang = lam_im * dt * n
    pw_re, pw_im = mag * jnp.cos(ang), mag * jnp.sin(ang)
    x, y = pw_re[1] - 1.0, pw_im[1]
    den = lam_re * lam_re + lam_im * lam_im
    z_re, z_im = (x * lam_re + y * lam_im) / den, (y * lam_re - x * lam_im) / den
    bb_re = z_re[..., None] * b_re - z_im[..., None] * b_im
    bb_im = z_re[..., None] * b_im + z_im[..., None] * b_re
    pb_re = pw_re[..., None] * bb_re - pw_im[..., None] * bb_im
    pb_im = pw_re[..., None] * bb_im + pw_im[..., None] * bb_re
    lagk = (jnp.einsum('gop,dgpi->gido', c_re, pb_re[:SSM_T], precision=hp)
            - jnp.einsum('gop,dgpi->gido', c_im, pb_im[:SSM_T], precision=hp))
    skip = jnp.eye(SSM_GROUP, dtype=F32)[None] * d_skip[:, :, None]
    lagk = lagk.at[:, :, 0, :].add(skip).reshape(n_g, SSM_GROUP, SSM_W)
    inj_re = pb_re[:SSM_T][::-1].transpose(1, 0, 3, 2).reshape(n_g, SSM_W, SSM_STATE)
    inj_im = pb_im[:SSM_T][::-1].transpose(1, 0, 3, 2).reshape(n_g, SSM_W, SSM_STATE)
    bm = jnp.concatenate([inj_re, inj_im, inj_im, inj_re], axis=-1)
    cp_re = c_re[None] * pw_re[1:, :, None, :] - c_im[None] * pw_im[1:, :, None, :]
    cp_im = c_re[None] * pw_im[1:, :, None, :] + c_im[None] * pw_re[1:, :, None, :]
    out_re = cp_re.transpose(1, 3, 0, 2).reshape(n_g, SSM_STATE, SSM_W)
    out_im = -cp_im.transpose(1, 3, 0, 2).reshape(n_g, SSM_STATE, SSM_W)
    cm = jnp.concatenate([out_re, out_im], axis=1)
    ar, ai = pw_re[SSM_T], pw_im[SSM_T]
    acoef = jnp.stack([jnp.concatenate([ar, ar], -1), jnp.concatenate([-ai, ai], -1),
                       jnp.concatenate([ai, -ai], -1)], axis=1)
    return lagk, bm.astype(BF16), cm.astype(BF16), acoef


def _bias_table(rel_bias):
    n_heads = rel_bias.shape[0]
    ext = ATT_QB + ATT_KW
    n_edge = ATT_LEFT - REL_CLIP + 1
    assert ATT_KW - n_edge == 2 * REL_CLIP - 1 and ATT_LEFT >= REL_CLIP
    far = rel_bias[:, 2 * REL_CLIP:]
    row = jnp.concatenate([jnp.broadcast_to(far, (n_heads, n_edge)), rel_bias[:, 1:2 * REL_CLIP][:, ::-1],
                           jnp.broadcast_to(far, (n_heads, ATT_QB))], axis=1)
    skew = jnp.tile(row, (1, ATT_QB))[:, :ATT_QB * (ext - 1)].reshape(n_heads, ATT_QB, ext - 1)
    return skew[:, :, :ATT_KW].astype(F32)


def _band_mask():
    r = jnp.arange(ATT_QB)[:, None] // CHUNK
    j = jnp.arange(ATT_KW)[None, :] // CHUNK
    return (j >= r) & (j <= r + LEFT_CHUNKS)


def _state_rows(s_re, s_im):
    s = jnp.concatenate([s_re, s_im], -1).transpose(1, 0, 2)
    sw = jnp.concatenate([s_im, s_re], -1).transpose(1, 0, 2)
    return jnp.stack([s, sw], axis=1)


def _cast_pad_kernel(*refs, scale):
    n = len(refs) // 2
    for w_ref, o_ref in zip(refs[:n], refs[n:]):
        r, c = w_ref.shape
        w = w_ref[...]
        o_ref[:r, :c] = (w if scale == 1.0 else w * scale).astype(BF16)
        if o_ref.shape[0] > r:
            o_ref[r:, :] = jnp.zeros((o_ref.shape[0] - r, o_ref.shape[1]), BF16)
        if o_ref.shape[1] > c:
            o_ref[:, c:] = jnp.zeros((o_ref.shape[0], o_ref.shape[1] - c), BF16)


def _cast_pad(ws, *, axis, mult, block, scale=1.0):
    r, c = ws[0].shape
    padded = -(-ws[0].shape[axis] // mult) * mult
    if axis == 1:
        in_spec, out_spec = pl.BlockSpec((block, c), lambda i: (i, 0)), pl.BlockSpec((block, padded), lambda i: (i, 0))
        out_shape, steps = jax.ShapeDtypeStruct((r, padded), BF16), r // block
    else:
        in_spec, out_spec = pl.BlockSpec((r, block), lambda i: (0, i)), pl.BlockSpec((padded, block), lambda i: (0, i))
        out_shape, steps = jax.ShapeDtypeStruct((padded, c), BF16), c // block
    return pl.pallas_call(
        functools.partial(_cast_pad_kernel, scale=scale),
        grid=(steps,),
        in_specs=[in_spec] * len(ws),
        out_specs=[out_spec] * len(ws),
        out_shape=[out_shape] * len(ws),
        compiler_params=pltpu.CompilerParams(
            dimension_semantics=("parallel",), vmem_limit_bytes=VMEM_LIMIT_V7X),
        name="cast_pad",
    )(*ws)


def _stream(x, p, *, tm, tf, tb, ssm_groups, cache=None):
    batch, seq, d = x.shape
    ffn = functools.partial(_ffn, tm=tm, tf=tf)
    x1 = ffn(x.reshape(batch * seq, d), p['g_ffn1'], p['ffn1_wg'], p['ffn1_wu'], p['ffn1_wd'], p['g_final'],
             final_norm=False).reshape(batch, seq, d)
    q, k, v, ug, kf, vf = _proj(x1, p['g_mix'], p['w_in'], p['q_norm'], p['k_norm'], tb=tb)
    n_g = ug.shape[0]
    if cache is None:
        att = _attn_prompt(q, k, v, p['bias_prompt'])
        s0 = jnp.zeros((n_g, 2, batch, 2 * SSM_STATE), F32)
    else:
        ck, cv, s_re, s_im = cache
        w_cache = ck.shape[1]
        assert w_cache == ATT_LEFT and seq <= ATT_QB
        att, kf, vf = _attn_sample(q, kf, vf, ck.reshape(batch, w_cache * N_HEADS, HEAD_DIM),
                                   cv.reshape(batch, w_cache * N_HEADS, HEAD_DIM),
                                   p['bias_table'][:, :seq, :w_cache + seq])
        s0 = _state_rows(s_re, s_im)
    yg, sf = _ssm(ug, p['lagk'], p['bm'], p['cm'], p['acoef'], s0, rows=batch, groups=ssm_groups)
    x2 = _mix(att, yg, x1, p['w_glu'], p['b_glu'], p['g_att'], p['g_ssm'], p['w_out'], tb=tb)
    y = ffn(x2.reshape(batch * seq, d), p['g_ffn2'], p['ffn2_wg'], p['ffn2_wu'], p['ffn2_wd'], p['g_final'],
            final_norm=True).reshape(batch, seq, d)
    sf = sf.transpose(1, 0, 2)
    return y, kf, vf, sf[..., :SSM_STATE], sf[..., SSM_STATE:]


def kernel(x_prompt, x_sample, cache_attn_k, cache_attn_v, state_ssm_re, state_ssm_im, norm_ffn1, ffn1_w_gate, ffn1_w_up, ffn1_w_down, norm_mix, w_in, q_norm, k_norm, rel_bias, ssm_lambda_re, ssm_lambda_im, ssm_log_dt, ssm_b_re, ssm_b_im, ssm_c_re, ssm_c_im, ssm_d, w_glu, b_glu, norm_att_out, norm_ssm_out, w_out, norm_ffn2, ffn2_w_gate, ffn2_w_up, ffn2_w_down, norm_final):
    depth = norm_ffn1.shape[0]
    bs, ls, _ = x_sample.shape
    yp, ys = x_prompt, x_sample
    outs = [[] for _ in range(8)]
    for l in range(depth):
        lagk, bm, cm, acoef = _ssm_matrices(ssm_lambda_re[l], ssm_lambda_im[l], ssm_log_dt[l], ssm_b_re[l],
                                          ssm_b_im[l], ssm_c_re[l], ssm_c_im[l], ssm_d[l])
        row = lambda a: a[l][None, :]
        bias_table = _bias_table(rel_bias[l])
        wg1, wu1, wg2, wu2 = _cast_pad([ffn1_w_gate[l], ffn1_w_up[l], ffn2_w_gate[l], ffn2_w_up[l]],
                                       axis=1, mult=FF_PAD, block=128)
        wd1, wd2 = _cast_pad([ffn1_w_down[l], ffn2_w_down[l]], axis=0, mult=FF_PAD, block=256, scale=0.5)
        p = dict(
            g_ffn1=row(norm_ffn1), g_mix=row(norm_mix), g_att=row(norm_att_out), g_ssm=row(norm_ssm_out),
            g_ffn2=row(norm_ffn2), g_final=row(norm_final), q_norm=row(q_norm), k_norm=row(k_norm),
            ffn1_wg=wg1, ffn1_wu=wu1, ffn1_wd=wd1, ffn2_wg=wg2, ffn2_wu=wu2, ffn2_wd=wd2,
            w_in=w_in[l].astype(BF16), w_glu=w_glu[l].astype(BF16), b_glu=row(b_glu), w_out=w_out[l].astype(BF16),
            bias_table=bias_table, bias_prompt=jnp.where(_band_mask()[None], bias_table, NEG_INF),
            lagk=lagk, bm=bm, cm=cm, acoef=acoef,
        )
        yp, kp, vp, rp, ip = _stream(yp, p, tm=1024, tf=FF_PAD, tb=CHUNK, ssm_groups=4)
        ys, kd, vd, rd, idd = _stream(ys, p, tm=bs * ls, tf=FF_PAD, tb=ls, ssm_groups=8,
                                      cache=(cache_attn_k[l], cache_attn_v[l], state_ssm_re[l], state_ssm_im[l]))
        as_cache = lambda a: a.reshape(a.shape[0], -1, N_HEADS, HEAD_DIM)
        for lst, val in zip(outs, (as_cache(kp), as_cache(vp), rp, ip, as_cache(kd), as_cache(vd), rd, idd)):
            lst.append(val)
    return (yp, ys) + tuple(jnp.stack(o) for o in outs)
```

```python
import functools

import jax
import jax.numpy as jnp
from jax import lax
from jax.experimental import pallas as pl
from jax.experimental.pallas import tpu as pltpu

EPS = 1e-6
NEG_INF = -1e30
CHUNK = 64
LEFT_CHUNKS = 8
ATT_LEFT = LEFT_CHUNKS * CHUNK
REL_CLIP = 256
N_HEADS = 8
HEAD_DIM = 128
SSM_GROUP = 16
SSM_STATE = 64
LANES = 128
SSM_T = 16
SSM_W = SSM_T * SSM_GROUP
GROUPS_PER_TILE = LANES // SSM_GROUP
ATT_QB = 256
ATT_KW = ATT_QB + ATT_LEFT
ATT_HEADS_PER_STEP = 4
REGROUP_PAD = 8
FF_PAD = 512
VMEM_LIMIT_V7X = 60 * 1024 * 1024

BF16 = jnp.bfloat16
F32 = jnp.float32


def _dot(a, b):
    return jnp.dot(a, b, preferred_element_type=F32)


def _rms(x, g):
    return x * lax.rsqrt(jnp.mean(x * x, axis=-1, keepdims=True) + EPS) * g


def _ffn_kernel(x_ref, g_ref, wg_hbm, wu_hbm, wd_hbm, gf_ref, o_ref, xn_ref, wg_buf, wu_buf, wd_buf, sem,
                *, tf, n_chunks, final_norm):
    i = pl.program_id(0)
    first_slot = (i * n_chunks) % 2

    def chunk_copies(j, slot):
        cols = pl.ds(pl.multiple_of(j * tf, tf), tf)
        return (pltpu.make_async_copy(wg_hbm.at[:, cols], wg_buf.at[slot], sem.at[0, slot]),
                pltpu.make_async_copy(wu_hbm.at[:, cols], wu_buf.at[slot], sem.at[1, slot]),
                pltpu.make_async_copy(wd_hbm.at[cols, :], wd_buf.at[slot], sem.at[2, slot]))

    @pl.when(i == 0)
    def _():
        for c in chunk_copies(0, 0):
            c.start()

    x = x_ref[...]
    xn_ref[...] = _rms(x, g_ref[...]).astype(BF16)
    o_ref[...] = x

    def body(j, carry):
        slot = (first_slot + j) % 2
        nxt = jnp.where(j + 1 == n_chunks, 0, j + 1)
        for c in chunk_copies(j, slot):
            c.wait()
        xn = xn_ref[...]
        a = _dot(xn, wg_buf[slot])
        b = _dot(xn, wu_buf[slot])
        for c in chunk_copies(nxt, 1 - slot):
            c.start()
        h = (a * jax.nn.sigmoid(a) * b).astype(BF16)
        o_ref[...] += _dot(h, wd_buf[slot])
        return carry

    lax.fori_loop(0, n_chunks, body, 0)

    @pl.when(i == pl.num_programs(0) - 1)
    def _():
        for c in chunk_copies(0, (first_slot + n_chunks) % 2):
            c.wait()

    if final_norm:
        o_ref[...] = _rms(o_ref[...], gf_ref[...])


def _ffn(x, g, wg, wu, wd_half, gf, *, tm, tf, final_norm):
    n, d = x.shape
    fpad = wg.shape[1]
    hbm = pl.BlockSpec(memory_space=pl.ANY)
    return pl.pallas_call(
        functools.partial(_ffn_kernel, tf=tf, n_chunks=fpad // tf, final_norm=final_norm),
        grid=(n // tm,),
        in_specs=[
            pl.BlockSpec((tm, d), lambda i: (i, 0)),
            pl.BlockSpec((1, d), lambda i: (0, 0)),
            hbm, hbm, hbm,
            pl.BlockSpec((1, d), lambda i: (0, 0)),
        ],
        out_specs=pl.BlockSpec((tm, d), lambda i: (i, 0)),
        out_shape=jax.ShapeDtypeStruct((n, d), F32),
        scratch_shapes=[pltpu.VMEM((tm, d), BF16),
                        pltpu.VMEM((2, d, tf), BF16), pltpu.VMEM((2, d, tf), BF16), pltpu.VMEM((2, tf, d), BF16),
                        pltpu.SemaphoreType.DMA((3, 2))],
        compiler_params=pltpu.CompilerParams(
            dimension_semantics=("arbitrary",), vmem_limit_bytes=VMEM_LIMIT_V7X),
        name="ffn",
    )(x, g, wg, wu, wd_half, gf)


def _granule_transpose(vs):
    gran = lax.broadcasted_iota(jnp.int32, vs[0].shape, 1) // SSM_GROUP
    for s in (4, 2, 1):
        upper = (gran & s) != 0
        new = list(vs)
        for a in range(GROUPS_PER_TILE):
            if a & s == 0:
                lo, hi = vs[a], vs[a + s]
                new[a] = jnp.where(upper, pltpu.roll(hi, s * SSM_GROUP, 1), lo)
                new[a + s] = jnp.where(upper, hi, pltpu.roll(lo, LANES - s * SSM_GROUP, 1))
        vs = new
    return vs


def _regroup_geometry(nb, tb):
    return tb + REGROUP_PAD, tb // SSM_T


def _tokens_to_groups(slab_ref, ug_ref, *, nb, tb):
    pitch, ncl = _regroup_geometry(nb, tb)
    for j in range(slab_ref.shape[0]):
        xs = [jnp.concatenate([slab_ref[j, pl.ds(c * SSM_T + t, nb, stride=pitch), :] for c in range(ncl)], axis=0)
              for t in range(SSM_T)]
        lo = _granule_transpose(xs[:GROUPS_PER_TILE])
        hi = _granule_transpose(xs[GROUPS_PER_TILE:])
        for gl in range(GROUPS_PER_TILE):
            ug_ref[j * GROUPS_PER_TILE + gl] = jnp.concatenate([lo[gl], hi[gl]], axis=1).astype(ug_ref.dtype)


def _groups_to_tokens(yg_ref, slab_ref, *, nb, tb):
    pitch, ncl = _regroup_geometry(nb, tb)
    for j in range(slab_ref.shape[0]):
        ds = [yg_ref[j * GROUPS_PER_TILE + gl].astype(F32) for gl in range(GROUPS_PER_TILE)]
        lo = _granule_transpose([d[:, :LANES] for d in ds])
        hi = _granule_transpose([d[:, LANES:] for d in ds])
        for t, x in enumerate(lo + hi):
            for c in range(ncl):
                slab_ref[j, pl.ds(c * SSM_T + t, nb, stride=pitch), :] = x[c * nb:(c + 1) * nb]


def _proj_kernel(x_ref, g_ref, w_ref, qn_ref, kn_ref, q_ref, k_ref, v_ref, ug_ref, kf_ref, vf_ref, slab_ref,
                 *, first_tail_block):
    i = pl.program_id(0)
    nb, tb, d = x_ref.shape
    d_att = N_HEADS * HEAD_DIM
    pitch, _ = _regroup_geometry(nb, tb)
    h = _rms(x_ref[...].reshape(nb * tb, d), g_ref[...]).astype(BF16)
    u = _dot(h, w_ref[:, 3 * d_att:])
    for j in range(slab_ref.shape[0]):
        for b in range(nb):
            slab_ref[j, b * pitch:b * pitch + tb, :] = u[b * tb:(b + 1) * tb, j * LANES:(j + 1) * LANES]
    _tokens_to_groups(slab_ref, ug_ref, nb=nb, tb=tb)

    def head_norm(y, gain):
        return [_rms(y[:, hd * HEAD_DIM:(hd + 1) * HEAD_DIM], gain) for hd in range(N_HEADS)]

    def store_tokens(ref, y):
        for b in range(nb):
            ref[b] = y[b * tb:(b + 1) * tb].astype(ref.dtype)

    qs = head_norm(_dot(h, w_ref[:, 0:d_att]), qn_ref[...] * (HEAD_DIM ** -0.5))
    store_tokens(q_ref, jnp.concatenate(qs, axis=1))
    ks = head_norm(_dot(h, w_ref[:, d_att:2 * d_att]), kn_ref[...])
    store_tokens(k_ref, jnp.concatenate(ks, axis=1))
    v = _dot(h, w_ref[:, 2 * d_att:3 * d_att])
    store_tokens(v_ref, v)

    @pl.when(i >= first_tail_block)
    def _():
        for b in range(nb):
            rows = slice(b * tb, (b + 1) * tb)
            for hd in range(N_HEADS):
                head_rows = pl.ds(hd, tb, stride=N_HEADS)
                kf_ref[b, head_rows, :] = ks[hd][rows]
                vf_ref[b, head_rows, :] = v[rows, hd * HEAD_DIM:(hd + 1) * HEAD_DIM]


def _proj(x, g, w, qn, kn, *, tb):
    nb, seq, d = x.shape
    d_att = N_HEADS * HEAD_DIM
    d_ssm = w.shape[1] - 3 * d_att
    n_g = d_ssm // SSM_GROUP
    n_blocks = seq // tb
    tail = min(ATT_LEFT, seq)
    first_tail_block = n_blocks - tail // tb
    pitch, ncl = _regroup_geometry(nb, tb)
    tok = lambda i: (0, i, 0)
    const = lambda i: (0, 0)
    tail_map = lambda i: (0, jnp.maximum(i - first_tail_block, 0), 0)
    return pl.pallas_call(
        functools.partial(_proj_kernel, first_tail_block=first_tail_block),
        grid=(n_blocks,),
        in_specs=[
            pl.BlockSpec((nb, tb, d), tok),
            pl.BlockSpec((1, d), const),
            pl.BlockSpec(w.shape, const, pipeline_mode=pl.Buffered(1)),
            pl.BlockSpec((1, HEAD_DIM), const),
            pl.BlockSpec((1, HEAD_DIM), const),
        ],
        out_specs=[
            pl.BlockSpec((nb, tb, d_att), tok),
            pl.BlockSpec((nb, tb, d_att), tok),
            pl.BlockSpec((nb, tb, d_att), tok),
            pl.BlockSpec((n_g, ncl * nb, SSM_W), tok),
            pl.BlockSpec((nb, tb * N_HEADS, HEAD_DIM), tail_map),
            pl.BlockSpec((nb, tb * N_HEADS, HEAD_DIM), tail_map),
        ],
        out_shape=[
            jax.ShapeDtypeStruct((nb, seq, d_att), BF16),
            jax.ShapeDtypeStruct((nb, seq, d_att), BF16),
            jax.ShapeDtypeStruct((nb, seq, d_att), BF16),
            jax.ShapeDtypeStruct((n_g, (seq // SSM_T) * nb, SSM_W), BF16),
            jax.ShapeDtypeStruct((nb, tail * N_HEADS, HEAD_DIM), F32),
            jax.ShapeDtypeStruct((nb, tail * N_HEADS, HEAD_DIM), F32),
        ],
        scratch_shapes=[pltpu.VMEM((d_ssm // LANES, nb * pitch, LANES), F32)],
        compiler_params=pltpu.CompilerParams(
            dimension_semantics=("arbitrary",), vmem_limit_bytes=VMEM_LIMIT_V7X),
        name="proj",
    )(x, g, w, qn, kn)


def _softmax_pv(s, v):
    m = jnp.max(s, axis=-1, keepdims=True)
    p = jnp.exp(s - m)
    l = jnp.sum(p, axis=-1, keepdims=True)
    return _dot(p.astype(BF16), v) / l


def _qk(q, k):
    return lax.dot_general(q, k, (((1,), (1,)), ((), ())), preferred_element_type=F32)


def _attn_prompt_kernel(q_ref, k_ref, v_ref, bias_ref, o_ref, s_ref, p_ref, linv_ref, *, n_blocks):
    lead = ATT_LEFT // ATT_QB
    n_full = n_blocks - lead

    def lanes(h):
        return slice(h * HEAD_DIM, (h + 1) * HEAD_DIM)

    def start(i):
        return pl.multiple_of(i * ATT_QB, ATT_QB)

    for i in range(min(lead, n_blocks)):
        kw = (i + 1) * ATT_QB
        rows = slice(i * ATT_QB, (i + 1) * ATT_QB)
        ss = [_qk(q_ref[rows, lanes(h)], k_ref[0:kw, lanes(h)]) for h in range(ATT_HEADS_PER_STEP)]
        for h in range(ATT_HEADS_PER_STEP):
            s = ss[h] + bias_ref[h, :, ATT_KW - kw:]
            o_ref[rows, lanes(h)] = _softmax_pv(s, v_ref[0:kw, lanes(h)]).astype(BF16)
    if n_full <= 0:
        return

    def scores(h, i, slot):
        s_ref[slot] = _qk(q_ref[pl.ds(start(i), ATT_QB), lanes(h)],
                          k_ref[pl.ds(start(i) - ATT_LEFT, ATT_KW), lanes(h)])

    def softmax(h, slot):
        win = ATT_KW - LANES
        for c in range(ATT_QB // CHUNK):
            rows = slice(c * CHUNK, (c + 1) * CHUNK)
            c0 = (c * CHUNK) // LANES * LANES
            assert c0 + win <= ATT_KW and c0 <= c * CHUNK and (c + LEFT_CHUNKS + 1) * CHUNK <= c0 + win
            s = s_ref[slot, rows, c0:c0 + win] + bias_ref[h, rows, c0:c0 + win]
            p = jnp.exp(s - jnp.max(s, axis=-1, keepdims=True))
            p_ref[slot, rows, c0:c0 + win] = p.astype(BF16)
            dead = slice(win, ATT_KW) if c0 == 0 else slice(0, c0)
            p_ref[slot, rows, dead] = jnp.zeros((CHUNK, LANES), BF16)
            linv_ref[slot, rows] = jnp.broadcast_to(1.0 / jnp.sum(p, axis=-1, keepdims=True), (CHUNK, HEAD_DIM))

    def output(h, i, slot):
        v = v_ref[pl.ds(start(i) - ATT_LEFT, ATT_KW), lanes(h)]
        o_ref[pl.ds(start(i), ATT_QB), lanes(h)] = (_dot(p_ref[slot], v) * linv_ref[slot]).astype(BF16)

    for h0 in range(0, ATT_HEADS_PER_STEP, 2):
        h1 = h0 + 1
        scores(h0, lead, 0)
        scores(h1, lead, 1)
        softmax(h0, 0)

        def body(r, carry):
            i = lead + r
            scores(h0, i + 1, 0)
            output(h0, i, 0)
            softmax(h1, 1)
            scores(h1, i + 1, 1)
            output(h1, i, 1)
            softmax(h0, 0)
            return carry

        lax.fori_loop(0, n_full - 1, body, 0)
        output(h0, n_blocks - 1, 0)
        softmax(h1, 1)
        output(h1, n_blocks - 1, 1)


def _attn_prompt(q, k, v, bias):
    batch, seq, d_att = q.shape
    hs = ATT_HEADS_PER_STEP
    blk = pl.BlockSpec((None, seq, hs * HEAD_DIM), lambda b, h: (b, 0, h))
    return pl.pallas_call(
        functools.partial(_attn_prompt_kernel, n_blocks=seq // ATT_QB),
        grid=(batch, N_HEADS // hs),
        in_specs=[blk, blk, blk, pl.BlockSpec((hs, ATT_QB, ATT_KW), lambda b, h: (h, 0, 0))],
        out_specs=blk,
        out_shape=jax.ShapeDtypeStruct((batch, seq, d_att), BF16),
        scratch_shapes=[pltpu.VMEM((2, ATT_QB, ATT_KW), F32), pltpu.VMEM((2, ATT_QB, ATT_KW), BF16),
                        pltpu.VMEM((2, ATT_QB, HEAD_DIM), F32)],
        compiler_params=pltpu.CompilerParams(
            dimension_semantics=("parallel", "parallel"), vmem_limit_bytes=VMEM_LIMIT_V7X),
        name="attn_prompt",
    )(q, k, v, bias)


def _attn_sample_kernel(q_ref, kn_ref, vn_ref, kc_ref, vc_ref, bias_ref, o_ref, ko_ref, vo_ref, *, w_cache, seq):
    def head_rows(ref, n, hd):
        return ref[pl.ds(hd, n, stride=N_HEADS), :].astype(BF16)

    for hd in range(N_HEADS):
        sl = slice(hd * HEAD_DIM, (hd + 1) * HEAD_DIM)
        q = q_ref[:, sl]
        s1 = _qk(q, head_rows(kc_ref, w_cache, hd)) + bias_ref[hd, :, :w_cache]
        s2 = _qk(q, head_rows(kn_ref, seq, hd)) + bias_ref[hd, :, w_cache:]
        m = jnp.maximum(jnp.max(s1, axis=-1, keepdims=True), jnp.max(s2, axis=-1, keepdims=True))
        p1 = jnp.exp(s1 - m)
        p2 = jnp.exp(s2 - m)
        l = jnp.sum(p1, axis=-1, keepdims=True) + jnp.sum(p2, axis=-1, keepdims=True)
        o = (_dot(p1.astype(BF16), head_rows(vc_ref, w_cache, hd))
             + _dot(p2.astype(BF16), head_rows(vn_ref, seq, hd)))
        o_ref[:, sl] = (o / l).astype(BF16)

    keep = (w_cache - seq) * N_HEADS
    for new_ref, old_ref, out_ref in ((kn_ref, kc_ref, ko_ref), (vn_ref, vc_ref, vo_ref)):
        out_ref[:keep, :] = old_ref[seq * N_HEADS:, :]
        out_ref[keep:, :] = new_ref[...]


def _attn_sample(q, kn, vn, kc, vc, bias):
    batch, seq, d_att = q.shape
    w_cache = kc.shape[1] // N_HEADS
    assert seq <= w_cache
    tok = pl.BlockSpec((None, seq, d_att), lambda b: (b, 0, 0))
    new = pl.BlockSpec((None, seq * N_HEADS, HEAD_DIM), lambda b: (b, 0, 0))
    cache = pl.BlockSpec((None, w_cache * N_HEADS, HEAD_DIM), lambda b: (b, 0, 0))
    return pl.pallas_call(
        functools.partial(_attn_sample_kernel, w_cache=w_cache, seq=seq),
        grid=(batch,),
        in_specs=[tok, new, new, cache, cache, pl.BlockSpec(bias.shape, lambda b: (0, 0, 0))],
        out_specs=[tok, cache, cache],
        out_shape=[jax.ShapeDtypeStruct((batch, seq, d_att), BF16),
                   jax.ShapeDtypeStruct(kc.shape, F32), jax.ShapeDtypeStruct(vc.shape, F32)],
        compiler_params=pltpu.CompilerParams(
            dimension_semantics=("parallel",), vmem_limit_bytes=VMEM_LIMIT_V7X),
        name="attn_sample",
    )(q, kn, vn, kc, vc, bias)


def _ssm_kernel(u_ref, lag_ref, bm_ref, cm_ref, a_ref, s0_ref, y_ref, sf_ref, sl_ref, sp_ref, km_ref,
                *, rows, n_chunks, groups):
    half = 2 * SSM_STATE
    lane = lax.broadcasted_iota(jnp.int32, (SSM_GROUP, SSM_W), 1)
    for gi in range(groups):
        lag = lag_ref[gi]
        for t_in in range(SSM_T):
            shifted = lag if t_in == 0 else pltpu.roll(lag, t_in * SSM_GROUP, 1)
            km_ref[gi, t_in * SSM_GROUP:(t_in + 1) * SSM_GROUP, :] = jnp.where(
                lane >= t_in * SSM_GROUP, shifted, 0.0).astype(BF16)
        sl_ref[gi] = _dot(u_ref[gi], bm_ref[gi])

    coef = []
    for gi in range(groups):
        coef.append(tuple(jnp.broadcast_to(a_ref[gi, r:r + 1, :], (rows, half)) for r in range(3)))

    def body(c, carry):
        r0 = pl.multiple_of(c * rows, rows)
        new = []
        for gi in range(groups):
            s, sw = carry[gi]
            a1, a2, a2w = coef[gi]
            sp_ref[gi, pl.ds(r0, rows), :] = s
            loc = sl_ref[gi, pl.ds(r0, rows), :]
            new.append((a1 * s + a2 * sw + loc[:, :half], a1 * sw + a2w * s + loc[:, half:]))
        return tuple(new)

    init = tuple((s0_ref[gi, 0], s0_ref[gi, 1]) for gi in range(groups))
    last = lax.fori_loop(0, n_chunks, body, init)
    for gi in range(groups):
        sf_ref[gi] = last[gi][0]
        y = _dot(u_ref[gi], km_ref[gi]) + _dot(sp_ref[gi].astype(BF16), cm_ref[gi])
        y_ref[gi] = jax.nn.gelu(y).astype(BF16)


def _ssm(ug, lagk, bm, cm, acoef, s0, *, rows, groups):
    n_g, n_rows, _ = ug.shape
    n_chunks = n_rows // rows
    half = 2 * SSM_STATE
    g3 = lambda g: (g, 0, 0)
    return pl.pallas_call(
        functools.partial(_ssm_kernel, rows=rows, n_chunks=n_chunks, groups=groups),
        grid=(n_g // groups,),
        in_specs=[
            pl.BlockSpec((groups, n_rows, SSM_W), g3),
            pl.BlockSpec((groups, SSM_GROUP, SSM_W), g3),
            pl.BlockSpec((groups, SSM_W, 2 * half), g3),
            pl.BlockSpec((groups, half, SSM_W), g3),
            pl.BlockSpec((groups, 3, half), g3),
            pl.BlockSpec((groups, 2, rows, half), lambda g: (g, 0, 0, 0)),
        ],
        out_specs=[
            pl.BlockSpec((groups, n_rows, SSM_W), g3),
            pl.BlockSpec((groups, rows, half), g3),
        ],
        out_shape=[
            jax.ShapeDtypeStruct((n_g, n_rows, SSM_W), BF16),
            jax.ShapeDtypeStruct((n_g, rows, half), F32),
        ],
        scratch_shapes=[
            pltpu.VMEM((groups, n_rows, 2 * half), F32),
            pltpu.VMEM((groups, n_rows, half), F32),
            pltpu.VMEM((groups, SSM_W, SSM_W), BF16),
        ],
        compiler_params=pltpu.CompilerParams(
            dimension_semantics=("parallel",), vmem_limit_bytes=VMEM_LIMIT_V7X),
        name="ssm",
    )(ug, lagk, bm, cm, acoef, s0)


def _mix_kernel(att_ref, yg_ref, x_ref, wglu_ref, bglu_ref, ga_ref, gs_ref, wout_ref, o_ref, slab_ref, y_ref):
    nb, tb, d = x_ref.shape
    d_att = att_ref.shape[2]
    d_ssm = y_ref.shape[1]
    pitch, _ = _regroup_geometry(nb, tb)
    att = att_ref[...].reshape(nb * tb, d_att).astype(F32)
    mix_a = _rms(att, ga_ref[...]).astype(BF16)
    o = x_ref[...].reshape(nb * tb, d) + _dot(mix_a, wout_ref[:d_att, :])
    _groups_to_tokens(yg_ref, slab_ref, nb=nb, tb=tb)
    for j in range(slab_ref.shape[0]):
        for b in range(nb):
            y_ref[b * tb:(b + 1) * tb, j * LANES:(j + 1) * LANES] = slab_ref[j, b * pitch:b * pitch + tb, :].astype(BF16)
    glu = _dot(y_ref[...], wglu_ref[...]) + bglu_ref[...]
    ssm_out = glu[:, :d_ssm] * jax.nn.sigmoid(glu[:, d_ssm:])
    mix_s = _rms(ssm_out, gs_ref[...]).astype(BF16)
    o = o + _dot(mix_s, wout_ref[d_att:, :])
    o_ref[...] = o.reshape(nb, tb, d)


def _mix(att, yg, x, wglu, bglu, ga, gs, wout, *, tb):
    nb, seq, d = x.shape
    d_att = att.shape[2]
    n_g = yg.shape[0]
    d_ssm = n_g * SSM_GROUP
    pitch, ncl = _regroup_geometry(nb, tb)
    tok = lambda i: (0, i, 0)
    const = lambda i: (0, 0)
    once = lambda a: pl.BlockSpec(a.shape, const, pipeline_mode=pl.Buffered(1))
    return pl.pallas_call(
        _mix_kernel,
        grid=(seq // tb,),
        in_specs=[
            pl.BlockSpec((nb, tb, d_att), tok),
            pl.BlockSpec((n_g, ncl * nb, SSM_W), tok),
            pl.BlockSpec((nb, tb, d), tok),
            once(wglu), once(bglu), once(ga), once(gs), once(wout),
        ],
        out_specs=pl.BlockSpec((nb, tb, d), tok),
        out_shape=jax.ShapeDtypeStruct((nb, seq, d), F32),
        scratch_shapes=[pltpu.VMEM((d_ssm // LANES, nb * pitch, LANES), F32),
                        pltpu.VMEM((nb * tb, d_ssm), BF16)],
        compiler_params=pltpu.CompilerParams(
            dimension_semantics=("parallel",), vmem_limit_bytes=VMEM_LIMIT_V7X),
        name="mix",
    )(att, yg, x, wglu, bglu, ga, gs, wout)


def _ssm_matrices(lam_re, lam_im, log_dt, b_re, b_im, c_re, c_im, d_skip):
    hp = lax.Precision.HIGHEST
    n_g = lam_re.shape[0]
    dt = jnp.exp(log_dt)[:, None]
    n = jnp.arange(SSM_T + 1, dtype=F32)[:, None, None]
    mag = jnp.exp(lam_re * dt * n)
    ang = lam_im * dt * n
    pw_re, pw_im = mag * jnp.cos(ang), mag * jnp.sin(ang)
    x, y = pw_re[1] - 1.0, pw_im[1]
    den = lam_re * lam_re + lam_im * lam_im
    z_re, z_im = (x * lam_re + y * lam_im) / den, (y * lam_re - x * lam_im) / den
    bb_re = z_re[..., None] * b_re - z_im[..., None] * b_im
    bb_im = z_re[..., None] * b_im + z_im[..., None] * b_re
    pb_re = pw_re[..., None] * bb_re - pw_im[..., None] * bb_im
    pb_im = pw_re[..., None] * bb_im + pw_im[..., None] * bb_re
    lagk = (jnp.einsum('gop,dgpi->gido', c_re, pb_re[:SSM_T], precision=hp)
            - jnp.einsum('gop,dgpi->gido', c_im, pb_im[:SSM_T], precision=hp))
    skip = jnp.eye(SSM_GROUP, dtype=F32)[None] * d_skip[:, :, None]
    lagk = lagk.at[:, :, 0, :].add(skip).reshape(n_g, SSM_GROUP, SSM_W)
    inj_re = pb_re[:SSM_T][::-1].transpose(1, 0, 3, 2).reshape(n_g, SSM_W, SSM_STATE)
    inj_im = pb_im[:SSM_T][::-1].transpose(1, 0, 3, 2).reshape(n_g, SSM_W, SSM_STATE)
    bm = jnp.concatenate([inj_re, inj_im, inj_im, inj_re], axis=-1)
    cp_re = c_re[None] * pw_re[1:, :, None, :] - c_im[None] * pw_im[1:, :, None, :]
    cp_im = c_re[None] * pw_im[1:, :, None, :] + c_im[None] * pw_re[1:, :, None, :]
    out_re = cp_re.transpose(1, 3, 0, 2).reshape(n_g, SSM_STATE, SSM_W)
    out_im = -cp_im.transpose(1, 3, 0, 2).reshape(n_g, SSM_STATE, SSM_W)
    cm = jnp.concatenate([out_re, out_im], axis=1)
    ar, ai = pw_re[SSM_T], pw_im[SSM_T]
    acoef = jnp.stack([jnp.concatenate([ar, ar], -1), jnp.concatenate([-ai, ai], -1),
                       jnp.concatenate([ai, -ai], -1)], axis=1)
    return lagk, bm.astype(BF16), cm.astype(BF16), acoef


def _bias_table(rel_bias):
    n_heads = rel_bias.shape[0]
    ext = ATT_QB + ATT_KW
    n_edge = ATT_LEFT - REL_CLIP + 1
    assert ATT_KW - n_edge == 2 * REL_CLIP - 1 and ATT_LEFT >= REL_CLIP
    far = rel_bias[:, 2 * REL_CLIP:]
    row = jnp.concatenate([jnp.broadcast_to(far, (n_heads, n_edge)), rel_bias[:, 1:2 * REL_CLIP][:, ::-1],
                           jnp.broadcast_to(far, (n_heads, ATT_QB))], axis=1)
    skew = jnp.tile(row, (1, ATT_QB))[:, :ATT_QB * (ext - 1)].reshape(n_heads, ATT_QB, ext - 1)
    return skew[:, :, :ATT_KW].astype(F32)


def _band_mask():
    r = jnp.arange(ATT_QB)[:, None] // CHUNK
    j = jnp.arange(ATT_KW)[None, :] // CHUNK
    return (j >= r) & (j <= r + LEFT_CHUNKS)


def _state_rows(s_re, s_im):
    s = jnp.concatenate([s_re, s_im], -1).transpose(1, 0, 2)
    sw = jnp.concatenate([s_im, s_re], -1).transpose(1, 0, 2)
    return jnp.stack([s, sw], axis=1)


def _cast_pad_kernel(*refs, scale):
    n = len(refs) // 2
    for w_ref, o_ref in zip(refs[:n], refs[n:]):
        r, c = w_ref.shape
        w = w_ref[...]
        o_ref[:r, :c] = (w if scale == 1.0 else w * scale).astype(BF16)
        if o_ref.shape[0] > r:
            o_ref[r:, :] = jnp.zeros((o_ref.shape[0] - r, o_ref.shape[1]), BF16)
        if o_ref.shape[1] > c:
            o_ref[:, c:] = jnp.zeros((o_ref.shape[0], o_ref.shape[1] - c), BF16)


def _cast_pad(ws, *, axis, mult, block, scale=1.0):
    r, c = ws[0].shape
    padded = -(-ws[0].shape[axis] // mult) * mult
    if axis == 1:
        in_spec, out_spec = pl.BlockSpec((block, c), lambda i: (i, 0)), pl.BlockSpec((block, padded), lambda i: (i, 0))
        out_shape, steps = jax.ShapeDtypeStruct((r, padded), BF16), r // block
    else:
        in_spec, out_spec = pl.BlockSpec((r, block), lambda i: (0, i)), pl.BlockSpec((padded, block), lambda i: (0, i))
        out_shape, steps = jax.ShapeDtypeStruct((padded, c), BF16), c // block
    return pl.pallas_call(
        functools.partial(_cast_pad_kernel, scale=scale),
        grid=(steps,),
        in_specs=[in_spec] * len(ws),
        out_specs=[out_spec] * len(ws),
        out_shape=[out_shape] * len(ws),
        compiler_params=pltpu.CompilerParams(
            dimension_semantics=("parallel",), vmem_limit_bytes=VMEM_LIMIT_V7X),
        name="cast_pad",
    )(*ws)


def _stream(x, p, *, tm, tf, tb, ssm_groups, cache=None):
    batch, seq, d = x.shape
    ffn = functools.partial(_ffn, tm=tm, tf=tf)
    x1 = ffn(x.reshape(batch * seq, d), p['g_ffn1'], p['ffn1_wg'], p['ffn1_wu'], p['ffn1_wd'], p['g_final'],
             final_norm=False).reshape(batch, seq, d)
    q, k, v, ug, kf, vf = _proj(x1, p['g_mix'], p['w_in'], p['q_norm'], p['k_norm'], tb=tb)
    n_g = ug.shape[0]
    if cache is None:
        att = _attn_prompt(q, k, v, p['bias_prompt'])
        s0 = jnp.zeros((n_g, 2, batch, 2 * SSM_STATE), F32)
    else:
        ck, cv, s_re, s_im = cache
        w_cache = ck.shape[1]
        assert w_cache == ATT_LEFT and seq <= ATT_QB
        att, kf, vf = _attn_sample(q, kf, vf, ck.reshape(batch, w_cache * N_HEADS, HEAD_DIM),
                                   cv.reshape(batch, w_cache * N_HEADS, HEAD_DIM),
                                   p['bias_table'][:, :seq, :w_cache + seq])
        s0 = _state_rows(s_re, s_im)
    yg, sf = _ssm(ug, p['lagk'], p['bm'], p['cm'], p['acoef'], s0, rows=batch, groups=ssm_groups)
    x2 = _mix(att, yg, x1, p['w_glu'], p['b_glu'], p['g_att'], p['g_ssm'], p['w_out'], tb=tb)
    y = ffn(x2.reshape(batch * seq, d), p['g_ffn2'], p['ffn2_wg'], p['ffn2_wu'], p['ffn2_wd'], p['g_final'],
            final_norm=True).reshape(batch, seq, d)
    sf = sf.transpose(1, 0, 2)
    return y, kf, vf, sf[..., :SSM_STATE], sf[..., SSM_STATE:]


def kernel(x_prompt, x_sample, cache_attn_k, cache_attn_v, state_ssm_re, state_ssm_im, norm_ffn1, ffn1_w_gate, ffn1_w_up, ffn1_w_down, norm_mix, w_in, q_norm, k_norm, rel_bias, ssm_lambda_re, ssm_lambda_im, ssm_log_dt, ssm_b_re, ssm_b_im, ssm_c_re, ssm_c_im, ssm_d, w_glu, b_glu, norm_att_out, norm_ssm_out, w_out, norm_ffn2, ffn2_w_gate, ffn2_w_up, ffn2_w_down, norm_final):
    depth = norm_ffn1.shape[0]
    bs, ls, _ = x_sample.shape
    yp, ys = x_prompt, x_sample
    outs = [[] for _ in range(8)]
    for l in range(depth):
        lagk, bm, cm, acoef = _ssm_matrices(ssm_lambda_re[l], ssm_lambda_im[l], ssm_log_dt[l], ssm_b_re[l],
                                          ssm_b_im[l], ssm_c_re[l], ssm_c_im[l], ssm_d[l])
        row = lambda a: a[l][None, :]
        bias_table = _bias_table(rel_bias[l])
        wg1, wu1, wg2, wu2 = _cast_pad([ffn1_w_gate[l], ffn1_w_up[l], ffn2_w_gate[l], ffn2_w_up[l]],
                                       axis=1, mult=FF_PAD, block=128)
        wd1, wd2 = _cast_pad([ffn1_w_down[l], ffn2_w_down[l]], axis=0, mult=FF_PAD, block=256, scale=0.5)
        p = dict(
            g_ffn1=row(norm_ffn1), g_mix=row(norm_mix), g_att=row(norm_att_out), g_ssm=row(norm_ssm_out),
            g_ffn2=row(norm_ffn2), g_final=row(norm_final), q_norm=row(q_norm), k_norm=row(k_norm),
            ffn1_wg=wg1, ffn1_wu=wu1, ffn1_wd=wd1, ffn2_wg=wg2, ffn2_wu=wu2, ffn2_wd=wd2,
            w_in=w_in[l].astype(BF16), w_glu=w_glu[l].astype(BF16), b_glu=row(b_glu), w_out=w_out[l].astype(BF16),
            bias_table=bias_table, bias_prompt=jnp.where(_band_mask()[None], bias_table, NEG_INF),
            lagk=lagk, bm=bm, cm=cm, acoef=acoef,
        )
        yp, kp, vp, rp, ip = _stream(yp, p, tm=1024, tf=FF_PAD, tb=CHUNK, ssm_groups=4)
        ys, kd, vd, rd, idd = _stream(ys, p, tm=bs * ls, tf=FF_PAD, tb=ls, ssm_groups=8,
                                      cache=(cache_attn_k[l], cache_attn_v[l], state_ssm_re[l], state_ssm_im[l]))
        as_cache = lambda a: a.reshape(a.shape[0], -1, N_HEADS, HEAD_DIM)
        for lst, val in zip(outs, (as_cache(kp), as_cache(vp), rp, ip, as_cache(kd), as_cache(vd), rd, idd)):
            lst.append(val)
    return (yp, ys) + tuple(jnp.stack(o) for o in outs)
```

```python
import functools

import jax
import jax.numpy as jnp
from jax import lax
from jax.experimental import pallas as pl
from jax.experimental.pallas import tpu as pltpu

EPS = 1e-6
NEG_INF = -1e30
CHUNK = 64
LEFT_CHUNKS = 8
ATT_LEFT = LEFT_CHUNKS * CHUNK
REL_CLIP = 256
N_HEADS = 8
HEAD_DIM = 128
SSM_GROUP = 16
SSM_STATE = 64
LANES = 128
SSM_T = 16
SSM_W = SSM_T * SSM_GROUP
GROUPS_PER_TILE = LANES // SSM_GROUP
ATT_QB = 256
ATT_KW = ATT_QB + ATT_LEFT
ATT_HEADS_PER_STEP = 4
REGROUP_PAD = 8
FF_PAD = 512
VMEM_LIMIT_V7X = 60 * 1024 * 1024

BF16 = jnp.bfloat16
F32 = jnp.float32


def _dot(a, b):
    return jnp.dot(a, b, preferred_element_type=F32)


def _rms(x, g):
    return x * lax.rsqrt(jnp.mean(x * x, axis=-1, keepdims=True) + EPS) * g


def _ffn_kernel(x_ref, g_ref, wg_hbm, wu_hbm, wd_hbm, gf_ref, o_ref, xn_ref, wg_buf, wu_buf, wd_buf, sem,
                *, tf, n_chunks, final_norm):
    i = pl.program_id(0)
    first_slot = (i * n_chunks) % 2

    def chunk_copies(j, slot):
        cols = pl.ds(pl.multiple_of(j * tf, tf), tf)
        return (pltpu.make_async_copy(wg_hbm.at[:, cols], wg_buf.at[slot], sem.at[0, slot]),
                pltpu.make_async_copy(wu_hbm.at[:, cols], wu_buf.at[slot], sem.at[1, slot]),
                pltpu.make_async_copy(wd_hbm.at[cols, :], wd_buf.at[slot], sem.at[2, slot]))

    @pl.when(i == 0)
    def _():
        for c in chunk_copies(0, 0):
            c.start()

    x = x_ref[...]
    xn_ref[...] = _rms(x, g_ref[...]).astype(BF16)
    o_ref[...] = x

    def body(j, carry):
        slot = (first_slot + j) % 2
        nxt = jnp.where(j + 1 == n_chunks, 0, j + 1)
        for c in chunk_copies(nxt, 1 - slot):
            c.start()
        for c in chunk_copies(j, slot):
            c.wait()
        xn = xn_ref[...]
        a = _dot(xn, wg_buf[slot])
        b = _dot(xn, wu_buf[slot])
        h = (a * jax.nn.sigmoid(a) * b).astype(BF16)
        o_ref[...] += _dot(h, wd_buf[slot])
        return carry

    lax.fori_loop(0, n_chunks, body, 0)

    @pl.when(i == pl.num_programs(0) - 1)
    def _():
        for c in chunk_copies(0, (first_slot + n_chunks) % 2):
            c.wait()

    if final_norm:
        o_ref[...] = _rms(o_ref[...], gf_ref[...])


def _ffn(x, g, wg, wu, wd_half, gf, *, tm, tf, final_norm):
    n, d = x.shape
    fpad = wg.shape[1]
    hbm = pl.BlockSpec(memory_space=pl.ANY)
    return pl.pallas_call(
        functools.partial(_ffn_kernel, tf=tf, n_chunks=fpad // tf, final_norm=final_norm),
        grid=(n // tm,),
        in_specs=[
            pl.BlockSpec((tm, d), lambda i: (i, 0)),
            pl.BlockSpec((1, d), lambda i: (0, 0)),
            hbm, hbm, hbm,
            pl.BlockSpec((1, d), lambda i: (0, 0)),
        ],
        out_specs=pl.BlockSpec((tm, d), lambda i: (i, 0)),
        out_shape=jax.ShapeDtypeStruct((n, d), F32),
        scratch_shapes=[pltpu.VMEM((tm, d), BF16),
                        pltpu.VMEM((2, d, tf), BF16), pltpu.VMEM((2, d, tf), BF16), pltpu.VMEM((2, tf, d), BF16),
                        pltpu.SemaphoreType.DMA((3, 2))],
        compiler_params=pltpu.CompilerParams(
            dimension_semantics=("arbitrary",), vmem_limit_bytes=VMEM_LIMIT_V7X),
        name="ffn",
    )(x, g, wg, wu, wd_half, gf)


def _granule_transpose(vs):
    gran = lax.broadcasted_iota(jnp.int32, vs[0].shape, 1) // SSM_GROUP
    for s in (4, 2, 1):
        upper = (gran & s) != 0
        new = list(vs)
        for a in range(GROUPS_PER_TILE):
            if a & s == 0:
                lo, hi = vs[a], vs[a + s]
                new[a] = jnp.where(upper, pltpu.roll(hi, s * SSM_GROUP, 1), lo)
                new[a + s] = jnp.where(upper, hi, pltpu.roll(lo, LANES - s * SSM_GROUP, 1))
        vs = new
    return vs


def _regroup_geometry(nb, tb):
    return tb + REGROUP_PAD, tb // SSM_T


def _tokens_to_groups(slab_ref, ug_ref, *, nb, tb):
    pitch, ncl = _regroup_geometry(nb, tb)
    for j in range(slab_ref.shape[0]):
        xs = [jnp.concatenate([slab_ref[j, pl.ds(c * SSM_T + t, nb, stride=pitch), :] for c in range(ncl)], axis=0)
              for t in range(SSM_T)]
        lo = _granule_transpose(xs[:GROUPS_PER_TILE])
        hi = _granule_transpose(xs[GROUPS_PER_TILE:])
        for gl in range(GROUPS_PER_TILE):
            ug_ref[j * GROUPS_PER_TILE + gl] = jnp.concatenate([lo[gl], hi[gl]], axis=1).astype(ug_ref.dtype)


def _groups_to_tokens(yg_ref, slab_ref, *, nb, tb):
    pitch, ncl = _regroup_geometry(nb, tb)
    for j in range(slab_ref.shape[0]):
        ds = [yg_ref[j * GROUPS_PER_TILE + gl].astype(F32) for gl in range(GROUPS_PER_TILE)]
        lo = _granule_transpose([d[:, :LANES] for d in ds])
        hi = _granule_transpose([d[:, LANES:] for d in ds])
        for t, x in enumerate(lo + hi):
            for c in range(ncl):
                slab_ref[j, pl.ds(c * SSM_T + t, nb, stride=pitch), :] = x[c * nb:(c + 1) * nb]


def _proj_kernel(x_ref, g_ref, w_ref, qn_ref, kn_ref, q_ref, k_ref, v_ref, ug_ref, kf_ref, vf_ref, slab_ref,
                 *, first_tail_block):
    i = pl.program_id(0)
    nb, tb, d = x_ref.shape
    d_att = N_HEADS * HEAD_DIM
    pitch, _ = _regroup_geometry(nb, tb)
    h = _rms(x_ref[...].reshape(nb * tb, d), g_ref[...]).astype(BF16)
    u = _dot(h, w_ref[:, 3 * d_att:])
    for j in range(slab_ref.shape[0]):
        for b in range(nb):
            slab_ref[j, b * pitch:b * pitch + tb, :] = u[b * tb:(b + 1) * tb, j * LANES:(j + 1) * LANES]
    _tokens_to_groups(slab_ref, ug_ref, nb=nb, tb=tb)

    def head_norm(y, gain):
        return [_rms(y[:, hd * HEAD_DIM:(hd + 1) * HEAD_DIM], gain) for hd in range(N_HEADS)]

    def store_tokens(ref, y):
        for b in range(nb):
            ref[b] = y[b * tb:(b + 1) * tb].astype(ref.dtype)

    qs = head_norm(_dot(h, w_ref[:, 0:d_att]), qn_ref[...] * (HEAD_DIM ** -0.5))
    store_tokens(q_ref, jnp.concatenate(qs, axis=1))
    ks = head_norm(_dot(h, w_ref[:, d_att:2 * d_att]), kn_ref[...])
    store_tokens(k_ref, jnp.concatenate(ks, axis=1))
    v = _dot(h, w_ref[:, 2 * d_att:3 * d_att])
    store_tokens(v_ref, v)

    @pl.when(i >= first_tail_block)
    def _():
        for b in range(nb):
            rows = slice(b * tb, (b + 1) * tb)
            for hd in range(N_HEADS):
                head_rows = pl.ds(hd, tb, stride=N_HEADS)
                kf_ref[b, head_rows, :] = ks[hd][rows]
                vf_ref[b, head_rows, :] = v[rows, hd * HEAD_DIM:(hd + 1) * HEAD_DIM]


def _proj(x, g, w, qn, kn, *, tb):
    nb, seq, d = x.shape
    d_att = N_HEADS * HEAD_DIM
    d_ssm = w.shape[1] - 3 * d_att
    n_g = d_ssm // SSM_GROUP
    n_blocks = seq // tb
    tail = min(ATT_LEFT, seq)
    first_tail_block = n_blocks - tail // tb
    pitch, ncl = _regroup_geometry(nb, tb)
    tok = lambda i: (0, i, 0)
    const = lambda i: (0, 0)
    tail_map = lambda i: (0, jnp.maximum(i - first_tail_block, 0), 0)
    return pl.pallas_call(
        functools.partial(_proj_kernel, first_tail_block=first_tail_block),
        grid=(n_blocks,),
        in_specs=[
            pl.BlockSpec((nb, tb, d), tok),
            pl.BlockSpec((1, d), const),
            pl.BlockSpec(w.shape, const, pipeline_mode=pl.Buffered(1)),
            pl.BlockSpec((1, HEAD_DIM), const),
            pl.BlockSpec((1, HEAD_DIM), const),
        ],
        out_specs=[
            pl.BlockSpec((nb, tb, d_att), tok),
            pl.BlockSpec((nb, tb, d_att), tok),
            pl.BlockSpec((nb, tb, d_att), tok),
            pl.BlockSpec((n_g, ncl * nb, SSM_W), tok),
            pl.BlockSpec((nb, tb * N_HEADS, HEAD_DIM), tail_map),
            pl.BlockSpec((nb, tb * N_HEADS, HEAD_DIM), tail_map),
        ],
        out_shape=[
            jax.ShapeDtypeStruct((nb, seq, d_att), BF16),
            jax.ShapeDtypeStruct((nb, seq, d_att), BF16),
            jax.ShapeDtypeStruct((nb, seq, d_att), BF16),
            jax.ShapeDtypeStruct((n_g, (seq // SSM_T) * nb, SSM_W), BF16),
            jax.ShapeDtypeStruct((nb, tail * N_HEADS, HEAD_DIM), F32),
            jax.ShapeDtypeStruct((nb, tail * N_HEADS, HEAD_DIM), F32),
        ],
        scratch_shapes=[pltpu.VMEM((d_ssm // LANES, nb * pitch, LANES), F32)],
        compiler_params=pltpu.CompilerParams(
            dimension_semantics=("arbitrary",), vmem_limit_bytes=VMEM_LIMIT_V7X),
        name="proj",
    )(x, g, w, qn, kn)


def _softmax_pv(s, v):
    m = jnp.max(s, axis=-1, keepdims=True)
    p = jnp.exp(s - m)
    l = jnp.sum(p, axis=-1, keepdims=True)
    return _dot(p.astype(BF16), v) / l


def _qk(q, k):
    return lax.dot_general(q, k, (((1,), (1,)), ((), ())), preferred_element_type=F32)


def _attn_prompt_kernel(q_ref, k_ref, v_ref, bias_ref, o_ref, s_ref, p_ref, linv_ref, *, n_blocks):
    lead = ATT_LEFT // ATT_QB
    n_full = n_blocks - lead

    def lanes(h):
        return slice(h * HEAD_DIM, (h + 1) * HEAD_DIM)

    def start(i):
        return pl.multiple_of(i * ATT_QB, ATT_QB)

    for i in range(min(lead, n_blocks)):
        kw = (i + 1) * ATT_QB
        rows = slice(i * ATT_QB, (i + 1) * ATT_QB)
        ss = [_qk(q_ref[rows, lanes(h)], k_ref[0:kw, lanes(h)]) for h in range(ATT_HEADS_PER_STEP)]
        for h in range(ATT_HEADS_PER_STEP):
            s = ss[h] + bias_ref[h, :, ATT_KW - kw:]
            o_ref[rows, lanes(h)] = _softmax_pv(s, v_ref[0:kw, lanes(h)]).astype(BF16)
    if n_full <= 0:
        return

    def scores(h, i, slot):
        s_ref[slot] = _qk(q_ref[pl.ds(start(i), ATT_QB), lanes(h)],
                          k_ref[pl.ds(start(i) - ATT_LEFT, ATT_KW), lanes(h)])

    def softmax(h, slot):
        win = ATT_KW - LANES
        for c in range(ATT_QB // CHUNK):
            rows = slice(c * CHUNK, (c + 1) * CHUNK)
            c0 = (c * CHUNK) // LANES * LANES
            assert c0 + win <= ATT_KW and c0 <= c * CHUNK and (c + LEFT_CHUNKS + 1) * CHUNK <= c0 + win
            s = s_ref[slot, rows, c0:c0 + win] + bias_ref[h, rows, c0:c0 + win]
            p = jnp.exp(s - jnp.max(s, axis=-1, keepdims=True))
            p_ref[slot, rows, c0:c0 + win] = p.astype(BF16)
            dead = slice(win, ATT_KW) if c0 == 0 else slice(0, c0)
            p_ref[slot, rows, dead] = jnp.zeros((CHUNK, LANES), BF16)
            linv_ref[slot, rows] = jnp.broadcast_to(1.0 / jnp.sum(p, axis=-1, keepdims=True), (CHUNK, HEAD_DIM))

    def output(h, i, slot):
        v = v_ref[pl.ds(start(i) - ATT_LEFT, ATT_KW), lanes(h)]
        o_ref[pl.ds(start(i), ATT_QB), lanes(h)] = (_dot(p_ref[slot], v) * linv_ref[slot]).astype(BF16)

    for h0 in range(0, ATT_HEADS_PER_STEP, 2):
        h1 = h0 + 1
        scores(h0, lead, 0)
        scores(h1, lead, 1)
        softmax(h0, 0)

        def body(r, carry):
            i = lead + r
            scores(h0, i + 1, 0)
            output(h0, i, 0)
            softmax(h1, 1)
            scores(h1, i + 1, 1)
            output(h1, i, 1)
            softmax(h0, 0)
            return carry

        lax.fori_loop(0, n_full - 1, body, 0)
        output(h0, n_blocks - 1, 0)
        softmax(h1, 1)
        output(h1, n_blocks - 1, 1)


def _attn_prompt(q, k, v, bias):
    batch, seq, d_att = q.shape
    hs = ATT_HEADS_PER_STEP
    blk = pl.BlockSpec((None, seq, hs * HEAD_DIM), lambda b, h: (b, 0, h))
    return pl.pallas_call(
        functools.partial(_attn_prompt_kernel, n_blocks=seq // ATT_QB),
        grid=(batch, N_HEADS // hs),
        in_specs=[blk, blk, blk, pl.BlockSpec((hs, ATT_QB, ATT_KW), lambda b, h: (h, 0, 0))],
        out_specs=blk,
        out_shape=jax.ShapeDtypeStruct((batch, seq, d_att), BF16),
        scratch_shapes=[pltpu.VMEM((2, ATT_QB, ATT_KW), F32), pltpu.VMEM((2, ATT_QB, ATT_KW), BF16),
                        pltpu.VMEM((2, ATT_QB, HEAD_DIM), F32)],
        compiler_params=pltpu.CompilerParams(
            dimension_semantics=("parallel", "parallel"), vmem_limit_bytes=VMEM_LIMIT_V7X),
        name="attn_prompt",
    )(q, k, v, bias)


def _attn_sample_kernel(q_ref, kn_ref, vn_ref, kc_ref, vc_ref, bias_ref, o_ref, ko_ref, vo_ref, *, w_cache, seq):
    def head_rows(ref, n, hd):
        return ref[pl.ds(hd, n, stride=N_HEADS), :].astype(BF16)

    for hd in range(N_HEADS):
        sl = slice(hd * HEAD_DIM, (hd + 1) * HEAD_DIM)
        q = q_ref[:, sl]
        s1 = _qk(q, head_rows(kc_ref, w_cache, hd)) + bias_ref[hd, :, :w_cache]
        s2 = _qk(q, head_rows(kn_ref, seq, hd)) + bias_ref[hd, :, w_cache:]
        m = jnp.maximum(jnp.max(s1, axis=-1, keepdims=True), jnp.max(s2, axis=-1, keepdims=True))
        p1 = jnp.exp(s1 - m)
        p2 = jnp.exp(s2 - m)
        l = jnp.sum(p1, axis=-1, keepdims=True) + jnp.sum(p2, axis=-1, keepdims=True)
        o = (_dot(p1.astype(BF16), head_rows(vc_ref, w_cache, hd))
             + _dot(p2.astype(BF16), head_rows(vn_ref, seq, hd)))
        o_ref[:, sl] = (o / l).astype(BF16)

    keep = (w_cache - seq) * N_HEADS
    for new_ref, old_ref, out_ref in ((kn_ref, kc_ref, ko_ref), (vn_ref, vc_ref, vo_ref)):
        out_ref[:keep, :] = old_ref[seq * N_HEADS:, :]
        out_ref[keep:, :] = new_ref[...]


def _attn_sample(q, kn, vn, kc, vc, bias):
    batch, seq, d_att = q.shape
    w_cache = kc.shape[1] // N_HEADS
    assert seq <= w_cache
    tok = pl.BlockSpec((None, seq, d_att), lambda b: (b, 0, 0))
    new = pl.BlockSpec((None, seq * N_HEADS, HEAD_DIM), lambda b: (b, 0, 0))
    cache = pl.BlockSpec((None, w_cache * N_HEADS, HEAD_DIM), lambda b: (b, 0, 0))
    return pl.pallas_call(
        functools.partial(_attn_sample_kernel, w_cache=w_cache, seq=seq),
        grid=(batch,),
        in_specs=[tok, new, new, cache, cache, pl.BlockSpec(bias.shape, lambda b: (0, 0, 0))],
        out_specs=[tok, cache, cache],
        out_shape=[jax.ShapeDtypeStruct((batch, seq, d_att), BF16),
                   jax.ShapeDtypeStruct(kc.shape, F32), jax.ShapeDtypeStruct(vc.shape, F32)],
        compiler_params=pltpu.CompilerParams(
            dimension_semantics=("parallel",), vmem_limit_bytes=VMEM_LIMIT_V7X),
        name="attn_sample",
    )(q, kn, vn, kc, vc, bias)


def _ssm_kernel(u_ref, lag_ref, bm_ref, cm_ref, a_ref, s0_ref, y_ref, sf_ref, sl_ref, sp_ref, km_ref,
                *, rows, n_chunks, groups):
    half = 2 * SSM_STATE
    lane = lax.broadcasted_iota(jnp.int32, (SSM_GROUP, SSM_W), 1)
    for gi in range(groups):
        lag = lag_ref[gi]
        for t_in in range(SSM_T):
            shifted = lag if t_in == 0 else pltpu.roll(lag, t_in * SSM_GROUP, 1)
            km_ref[gi, t_in * SSM_GROUP:(t_in + 1) * SSM_GROUP, :] = jnp.where(
                lane >= t_in * SSM_GROUP, shifted, 0.0).astype(BF16)
        sl_ref[gi] = _dot(u_ref[gi], bm_ref[gi])

    coef = []
    for gi in range(groups):
        coef.append(tuple(jnp.broadcast_to(a_ref[gi, r:r + 1, :], (rows, half)) for r in range(3)))

    def body(c, carry):
        r0 = pl.multiple_of(c * rows, rows)
        new = []
        for gi in range(groups):
            s, sw = carry[gi]
            a1, a2, a2w = coef[gi]
            sp_ref[gi, pl.ds(r0, rows), :] = s
            loc = sl_ref[gi, pl.ds(r0, rows), :]
            new.append((a1 * s + a2 * sw + loc[:, :half], a1 * sw + a2w * s + loc[:, half:]))
        return tuple(new)

    init = tuple((s0_ref[gi, 0], s0_ref[gi, 1]) for gi in range(groups))
    last = lax.fori_loop(0, n_chunks, body, init)
    for gi in range(groups):
        sf_ref[gi] = last[gi][0]
        y = _dot(u_ref[gi], km_ref[gi]) + _dot(sp_ref[gi].astype(BF16), cm_ref[gi])
        y_ref[gi] = jax.nn.gelu(y).astype(BF16)


def _ssm(ug, lagk, bm, cm, acoef, s0, *, rows, groups):
    n_g, n_rows, _ = ug.shape
    n_chunks = n_rows // rows
    half = 2 * SSM_STATE
    g3 = lambda g: (g, 0, 0)
    return pl.pallas_call(
        functools.partial(_ssm_kernel, rows=rows, n_chunks=n_chunks, groups=groups),
        grid=(n_g // groups,),
        in_specs=[
            pl.BlockSpec((groups, n_rows, SSM_W), g3),
            pl.BlockSpec((groups, SSM_GROUP, SSM_W), g3),
            pl.BlockSpec((groups, SSM_W, 2 * half), g3),
            pl.BlockSpec((groups, half, SSM_W), g3),
            pl.BlockSpec((groups, 3, half), g3),
            pl.BlockSpec((groups, 2, rows, half), lambda g: (g, 0, 0, 0)),
        ],
        out_specs=[
            pl.BlockSpec((groups, n_rows, SSM_W), g3),
            pl.BlockSpec((groups, rows, half), g3),
        ],
        out_shape=[
            jax.ShapeDtypeStruct((n_g, n_rows, SSM_W), BF16),
            jax.ShapeDtypeStruct((n_g, rows, half), F32),
        ],
        scratch_shapes=[
            pltpu.VMEM((groups, n_rows, 2 * half), F32),
            pltpu.VMEM((groups, n_rows, half), F32),
            pltpu.VMEM((groups, SSM_W, SSM_W), BF16),
        ],
        compiler_params=pltpu.CompilerParams(
            dimension_semantics=("parallel",), vmem_limit_bytes=VMEM_LIMIT_V7X),
        name="ssm",
    )(ug, lagk, bm, cm, acoef, s0)


def _mix_kernel(att_ref, yg_ref, x_ref, wglu_ref, bglu_ref, ga_ref, gs_ref, wout_ref, o_ref, slab_ref, y_ref):
    nb, tb, d = x_ref.shape
    d_att = att_ref.shape[2]
    d_ssm = y_ref.shape[1]
    pitch, _ = _regroup_geometry(nb, tb)
    att = att_ref[...].reshape(nb * tb, d_att).astype(F32)
    mix_a = _rms(att, ga_ref[...]).astype(BF16)
    o = x_ref[...].reshape(nb * tb, d) + _dot(mix_a, wout_ref[:d_att, :])
    _groups_to_tokens(yg_ref, slab_ref, nb=nb, tb=tb)
    for j in range(slab_ref.shape[0]):
        for b in range(nb):
            y_ref[b * tb:(b + 1) * tb, j * LANES:(j + 1) * LANES] = slab_ref[j, b * pitch:b * pitch + tb, :].astype(BF16)
    glu = _dot(y_ref[...], wglu_ref[...]) + bglu_ref[...]
    ssm_out = glu[:, :d_ssm] * jax.nn.sigmoid(glu[:, d_ssm:])
    mix_s = _rms(ssm_out, gs_ref[...]).astype(BF16)
    o = o + _dot(mix_s, wout_ref[d_att:, :])
    o_ref[...] = o.reshape(nb, tb, d)


def _mix(att, yg, x, wglu, bglu, ga, gs, wout, *, tb):
    nb, seq, d = x.shape
    d_att = att.shape[2]
    n_g = yg.shape[0]
    d_ssm = n_g * SSM_GROUP
    pitch, ncl = _regroup_geometry(nb, tb)
    tok = lambda i: (0, i, 0)
    const = lambda i: (0, 0)
    once = lambda a: pl.BlockSpec(a.shape, const, pipeline_mode=pl.Buffered(1))
    return pl.pallas_call(
        _mix_kernel,
        grid=(seq // tb,),
        in_specs=[
            pl.BlockSpec((nb, tb, d_att), tok),
            pl.BlockSpec((n_g, ncl * nb, SSM_W), tok),
            pl.BlockSpec((nb, tb, d), tok),
            once(wglu), once(bglu), once(ga), once(gs), once(wout),
        ],
        out_specs=pl.BlockSpec((nb, tb, d), tok),
        out_shape=jax.ShapeDtypeStruct((nb, seq, d), F32),
        scratch_shapes=[pltpu.VMEM((d_ssm // LANES, nb * pitch, LANES), F32),
                        pltpu.VMEM((nb * tb, d_ssm), BF16)],
        compiler_params=pltpu.CompilerParams(
            dimension_semantics=("parallel",), vmem_limit_bytes=VMEM_LIMIT_V7X),
        name="mix",
    )(att, yg, x, wglu, bglu, ga, gs, wout)


def _ssm_matrices(lam_re, lam_im, log_dt, b_re, b_im, c_re, c_im, d_skip):
    hp = lax.Precision.HIGHEST
    n_g = lam_re.shape[0]
    dt = jnp.exp(log_dt)[:, None]
    n = jnp.arange(SSM_T + 1, dtype=F32)[:, None, None]
    mag = jnp.exp(lam_re * dt * n)
    ang = lam_im * dt * n
    pw_re, pw_im = mag * jnp.cos(ang), mag * jnp.sin(ang)
    x, y = pw_re[1] - 1.0, pw_im[1]
    den = lam_re * lam_re + lam_im * lam_im
    z_re, z_im = (x * lam_re + y * lam_im) / den, (y * lam_re - x * lam_im) / den
    bb_re = z_re[..., None] * b_re - z_im[..., None] * b_im
    bb_im = z_re[..., None] * b_im + z_im[..., None] * b_re
    pb_re = pw_re[..., None] * bb_re - pw_im[..., None] * bb_im
    pb_im = pw_re[..., None] * bb_im + pw_im[..., None] * bb_re
    lagk = (jnp.einsum('gop,dgpi->gido', c_re, pb_re[:SSM_T], precision=hp)
            - jnp.einsum('gop,dgpi->gido', c_im, pb_im[:SSM_T], precision=hp))
    skip = jnp.eye(SSM_GROUP, dtype=F32)[None] * d_skip[:, :, None]
    lagk = lagk.at[:, :, 0, :].add(skip).reshape(n_g, SSM_GROUP, SSM_W)
    inj_re = pb_re[:SSM_T][::-1].transpose(1, 0, 3, 2).reshape(n_g, SSM_W, SSM_STATE)
    inj_im = pb_im[:SSM_T][::-1].transpose(1, 0, 3, 2).reshape(n_g, SSM_W, SSM_STATE)
    bm = jnp.concatenate([inj_re, inj_im, inj_im, inj_re], axis=-1)
    cp_re = c_re[None] * pw_re[1:, :, None, :] - c_im[None] * pw_im[1:, :, None, :]
    cp_im = c_re[None] * pw_im[1:, :, None, :] + c_im[None] * pw_re[1:, :, None, :]
    out_re = cp_re.transpose(1, 3, 0, 2).reshape(n_g, SSM_STATE, SSM_W)
    out_im = -cp_im.transpose(1, 3, 0, 2).reshape(n_g, SSM_STATE, SSM_W)
    cm = jnp.concatenate([out_re, out_im], axis=1)
    ar, ai = pw_re[SSM_T], pw_im[SSM_T]
    acoef = jnp.stack([jnp.concatenate([ar, ar], -1), jnp.concatenate([-ai, ai], -1),
                       jnp.concatenate([ai, -ai], -1)], axis=1)
    return lagk, bm.astype(BF16), cm.astype(BF16), acoef


def _bias_table(rel_bias):
    n_heads = rel_bias.shape[0]
    ext = ATT_QB + ATT_KW
    n_edge = ATT_LEFT - REL_CLIP + 1
    assert ATT_KW - n_edge == 2 * REL_CLIP - 1 and ATT_LEFT >= REL_CLIP
    far = rel_bias[:, 2 * REL_CLIP:]
    row = jnp.concatenate([jnp.broadcast_to(far, (n_heads, n_edge)), rel_bias[:, 1:2 * REL_CLIP][:, ::-1],
                           jnp.broadcast_to(far, (n_heads, ATT_QB))], axis=1)
    skew = jnp.tile(row, (1, ATT_QB))[:, :ATT_QB * (ext - 1)].reshape(n_heads, ATT_QB, ext - 1)
    return skew[:, :, :ATT_KW].astype(F32)


def _band_mask():
    r = jnp.arange(ATT_QB)[:, None] // CHUNK
    j = jnp.arange(ATT_KW)[None, :] // CHUNK
    return (j >= r) & (j <= r + LEFT_CHUNKS)


def _state_rows(s_re, s_im):
    s = jnp.concatenate([s_re, s_im], -1).transpose(1, 0, 2)
    sw = jnp.concatenate([s_im, s_re], -1).transpose(1, 0, 2)
    return jnp.stack([s, sw], axis=1)


def _cast_pad_kernel(*refs, scale):
    n = len(refs) // 2
    for w_ref, o_ref in zip(refs[:n], refs[n:]):
        r, c = w_ref.shape
        w = w_ref[...]
        o_ref[:r, :c] = (w if scale == 1.0 else w * scale).astype(BF16)
        if o_ref.shape[0] > r:
            o_ref[r:, :] = jnp.zeros((o_ref.shape[0] - r, o_ref.shape[1]), BF16)
        if o_ref.shape[1] > c:
            o_ref[:, c:] = jnp.zeros((o_ref.shape[0], o_ref.shape[1] - c), BF16)


def _cast_pad(ws, *, axis, mult, block, scale=1.0):
    r, c = ws[0].shape
    padded = -(-ws[0].shape[axis] // mult) * mult
    if axis == 1:
        in_spec, out_spec = pl.BlockSpec((block, c), lambda i: (i, 0)), pl.BlockSpec((block, padded), lambda i: (i, 0))
        out_shape, steps = jax.ShapeDtypeStruct((r, padded), BF16), r // block
    else:
        in_spec, out_spec = pl.BlockSpec((r, block), lambda i: (0, i)), pl.BlockSpec((padded, block), lambda i: (0, i))
        out_shape, steps = jax.ShapeDtypeStruct((padded, c), BF16), c // block
    return pl.pallas_call(
        functools.partial(_cast_pad_kernel, scale=scale),
        grid=(steps,),
        in_specs=[in_spec] * len(ws),
        out_specs=[out_spec] * len(ws),
        out_shape=[out_shape] * len(ws),
        compiler_params=pltpu.CompilerParams(
            dimension_semantics=("parallel",), vmem_limit_bytes=VMEM_LIMIT_V7X),
        name="cast_pad",
    )(*ws)


def _stream(x, p, *, tm, tf, tb, ssm_groups, cache=None):
    batch, seq, d = x.shape
    ffn = functools.partial(_ffn, tm=tm, tf=tf)
    x1 = ffn(x.reshape(batch * seq, d), p['g_ffn1'], p['ffn1_wg'], p['ffn1_wu'], p['ffn1_wd'], p['g_final'],
             final_norm=False).reshape(batch, seq, d)
    q, k, v, ug, kf, vf = _proj(x1, p['g_mix'], p['w_in'], p['q_norm'], p['k_norm'], tb=tb)
    n_g = ug.shape[0]
    if cache is None:
        att = _attn_prompt(q, k, v, p['bias_prompt'])
        s0 = jnp.zeros((n_g, 2, batch, 2 * SSM_STATE), F32)
    else:
        ck, cv, s_re, s_im = cache
        w_cache = ck.shape[1]
        assert w_cache == ATT_LEFT and seq <= ATT_QB
        att, kf, vf = _attn_sample(q, kf, vf, ck.reshape(batch, w_cache * N_HEADS, HEAD_DIM),
                                   cv.reshape(batch, w_cache * N_HEADS, HEAD_DIM),
                                   p['bias_table'][:, :seq, :w_cache + seq])
        s0 = _state_rows(s_re, s_im)
    yg, sf = _ssm(ug, p['lagk'], p['bm'], p['cm'], p['acoef'], s0, rows=batch, groups=ssm_groups)
    x2 = _mix(att, yg, x1, p['w_glu'], p['b_glu'], p['g_att'], p['g_ssm'], p['w_out'], tb=tb)
    y = ffn(x2.reshape(batch * seq, d), p['g_ffn2'], p['ffn2_wg'], p['ffn2_wu'], p['ffn2_wd'], p['g_final'],
            final_norm=True).reshape(batch, seq, d)
    sf = sf.transpose(1, 0, 2)
    return y, kf, vf, sf[..., :SSM_STATE], sf[..., SSM_STATE:]


def kernel(x_prompt, x_sample, cache_attn_k, cache_attn_v, state_ssm_re, state_ssm_im, norm_ffn1, ffn1_w_gate, ffn1_w_up, ffn1_w_down, norm_mix, w_in, q_norm, k_norm, rel_bias, ssm_lambda_re, ssm_lambda_im, ssm_log_dt, ssm_b_re, ssm_b_im, ssm_c_re, ssm_c_im, ssm_d, w_glu, b_glu, norm_att_out, norm_ssm_out, w_out, norm_ffn2, ffn2_w_gate, ffn2_w_up, ffn2_w_down, norm_final):
    depth = norm_ffn1.shape[0]
    bs, ls, _ = x_sample.shape
    yp, ys = x_prompt, x_sample
    outs = [[] for _ in range(8)]
    for l in range(depth):
        lagk, bm, cm, acoef = _ssm_matrices(ssm_lambda_re[l], ssm_lambda_im[l], ssm_log_dt[l], ssm_b_re[l],
                                          ssm_b_im[l], ssm_c_re[l], ssm_c_im[l], ssm_d[l])
        row = lambda a: a[l][None, :]
        bias_table = _bias_table(rel_bias[l])
        wg1, wu1, wg2, wu2 = _cast_pad([ffn1_w_gate[l], ffn1_w_up[l], ffn2_w_gate[l], ffn2_w_up[l]],
                                       axis=1, mult=FF_PAD, block=128)
        wd1, wd2 = _cast_pad([ffn1_w_down[l], ffn2_w_down[l]], axis=0, mult=FF_PAD, block=256, scale=0.5)
        p = dict(
            g_ffn1=row(norm_ffn1), g_mix=row(norm_mix), g_att=row(norm_att_out), g_ssm=row(norm_ssm_out),
            g_ffn2=row(norm_ffn2), g_final=row(norm_final), q_norm=row(q_norm), k_norm=row(k_norm),
            ffn1_wg=wg1, ffn1_wu=wu1, ffn1_wd=wd1, ffn2_wg=wg2, ffn2_wu=wu2, ffn2_wd=wd2,
            w_in=w_in[l].astype(BF16), w_glu=w_glu[l].astype(BF16), b_glu=row(b_glu), w_out=w_out[l].astype(BF16),
            bias_table=bias_table, bias_prompt=jnp.where(_band_mask()[None], bias_table, NEG_INF),
            lagk=lagk, bm=bm, cm=cm, acoef=acoef,
        )
        yp, kp, vp, rp, ip = _stream(yp, p, tm=1024, tf=FF_PAD, tb=CHUNK, ssm_groups=4)
        ys, kd, vd, rd, idd = _stream(ys, p, tm=bs * ls, tf=FF_PAD, tb=ls, ssm_groups=8,
                                      cache=(cache_attn_k[l], cache_attn_v[l], state_ssm_re[l], state_ssm_im[l]))
        as_cache = lambda a: a.reshape(a.shape[0], -1, N_HEADS, HEAD_DIM)
        for lst, val in zip(outs, (as_cache(kp), as_cache(vp), rp, ip, as_cache(kd), as_cache(vd), rd, idd)):
            lst.append(val)
    return (yp, ys) + tuple(jnp.stack(o) for o in outs)
```

```python
import functools

import jax
import jax.numpy as jnp
from jax import lax
from jax.experimental import pallas as pl
from jax.experimental.pallas import tpu as pltpu

EPS = 1e-6
NEG_INF = -1e30
CHUNK = 64
LEFT_CHUNKS = 8
ATT_LEFT = LEFT_CHUNKS * CHUNK
REL_CLIP = 256
N_HEADS = 8
HEAD_DIM = 128
SSM_GROUP = 16
SSM_STATE = 64
LANES = 128
SSM_T = 16
SSM_W = SSM_T * SSM_GROUP
GROUPS_PER_TILE = LANES // SSM_GROUP
ATT_QB = 256
ATT_KW = ATT_QB + ATT_LEFT
ATT_HEADS_PER_STEP = 4
REGROUP_PAD = 8
FF_PAD = 512
VMEM_LIMIT_V7X = 60 * 1024 * 1024

BF16 = jnp.bfloat16
F32 = jnp.float32


def _dot(a, b):
    return jnp.dot(a, b, preferred_element_type=F32)


def _rms(x, g):
    return x * lax.rsqrt(jnp.mean(x * x, axis=-1, keepdims=True) + EPS) * g


def _ffn_kernel(x_ref, g_ref, wg_hbm, wu_hbm, wd_hbm, gf_ref, o_ref, xn_ref, wg_buf, wu_buf, wd_buf, sem,
                *, tf, n_chunks, final_norm):
    i = pl.program_id(0)
    first_slot = (i * n_chunks) % 2

    def chunk_copies(j, slot):
        cols = pl.ds(pl.multiple_of(j * tf, tf), tf)
        return (pltpu.make_async_copy(wg_hbm.at[:, cols], wg_buf.at[slot], sem.at[0, slot]),
                pltpu.make_async_copy(wu_hbm.at[:, cols], wu_buf.at[slot], sem.at[1, slot]),
                pltpu.make_async_copy(wd_hbm.at[cols, :], wd_buf.at[slot], sem.at[2, slot]))

    @pl.when(i == 0)
    def _():
        for c in chunk_copies(0, 0):
            c.start()

    x = x_ref[...]
    xn_ref[...] = _rms(x, g_ref[...]).astype(BF16)
    o_ref[...] = x

    def body(j, carry):
        slot = (first_slot + j) % 2
        nxt = jnp.where(j + 1 == n_chunks, 0, j + 1)
        for c in chunk_copies(nxt, 1 - slot):
            c.start()
        for c in chunk_copies(j, slot):
            c.wait()
        xn = xn_ref[...]
        a = _dot(xn, wg_buf[slot])
        b = _dot(xn, wu_buf[slot])
        h = (a * jax.nn.sigmoid(a) * b).astype(BF16)
        o_ref[...] += _dot(h, wd_buf[slot])
        return carry

    lax.fori_loop(0, n_chunks, body, 0)

    @pl.when(i == pl.num_programs(0) - 1)
    def _():
        for c in chunk_copies(0, (first_slot + n_chunks) % 2):
            c.wait()

    if final_norm:
        o_ref[...] = _rms(o_ref[...], gf_ref[...])


def _ffn(x, g, wg, wu, wd_half, gf, *, tm, tf, final_norm):
    n, d = x.shape
    fpad = wg.shape[1]
    hbm = pl.BlockSpec(memory_space=pl.ANY)
    return pl.pallas_call(
        functools.partial(_ffn_kernel, tf=tf, n_chunks=fpad // tf, final_norm=final_norm),
        grid=(n // tm,),
        in_specs=[
            pl.BlockSpec((tm, d), lambda i: (i, 0)),
            pl.BlockSpec((1, d), lambda i: (0, 0)),
            hbm, hbm, hbm,
            pl.BlockSpec((1, d), lambda i: (0, 0)),
        ],
        out_specs=pl.BlockSpec((tm, d), lambda i: (i, 0)),
        out_shape=jax.ShapeDtypeStruct((n, d), F32),
        scratch_shapes=[pltpu.VMEM((tm, d), BF16),
                        pltpu.VMEM((2, d, tf), BF16), pltpu.VMEM((2, d, tf), BF16), pltpu.VMEM((2, tf, d), BF16),
                        pltpu.SemaphoreType.DMA((3, 2))],
        compiler_params=pltpu.CompilerParams(
            dimension_semantics=("arbitrary",), vmem_limit_bytes=VMEM_LIMIT_V7X),
        name="ffn",
    )(x, g, wg, wu, wd_half, gf)


def _granule_transpose(vs):
    gran = lax.broadcasted_iota(jnp.int32, vs[0].shape, 1) // SSM_GROUP
    for s in (4, 2, 1):
        upper = (gran & s) != 0
        new = list(vs)
        for a in range(GROUPS_PER_TILE):
            if a & s == 0:
                lo, hi = vs[a], vs[a + s]
                new[a] = jnp.where(upper, pltpu.roll(hi, s * SSM_GROUP, 1), lo)
                new[a + s] = jnp.where(upper, hi, pltpu.roll(lo, LANES - s * SSM_GROUP, 1))
        vs = new
    return vs


def _regroup_geometry(nb, tb):
    return tb + REGROUP_PAD, tb // SSM_T


def _tokens_to_groups(slab_ref, ug_ref, *, nb, tb):
    pitch, ncl = _regroup_geometry(nb, tb)
    for j in range(slab_ref.shape[0]):
        xs = [jnp.concatenate([slab_ref[j, pl.ds(c * SSM_T + t, nb, stride=pitch), :] for c in range(ncl)], axis=0)
              for t in range(SSM_T)]
        lo = _granule_transpose(xs[:GROUPS_PER_TILE])
        hi = _granule_transpose(xs[GROUPS_PER_TILE:])
        for gl in range(GROUPS_PER_TILE):
            ug_ref[j * GROUPS_PER_TILE + gl] = jnp.concatenate([lo[gl], hi[gl]], axis=1).astype(ug_ref.dtype)


def _groups_to_tokens(yg_ref, slab_ref, *, nb, tb):
    pitch, ncl = _regroup_geometry(nb, tb)
    for j in range(slab_ref.shape[0]):
        ds = [yg_ref[j * GROUPS_PER_TILE + gl].astype(F32) for gl in range(GROUPS_PER_TILE)]
        lo = _granule_transpose([d[:, :LANES] for d in ds])
        hi = _granule_transpose([d[:, LANES:] for d in ds])
        for t, x in enumerate(lo + hi):
            for c in range(ncl):
                slab_ref[j, pl.ds(c * SSM_T + t, nb, stride=pitch), :] = x[c * nb:(c + 1) * nb]


def _proj_kernel(*refs, first_tail_block, n_side):
    x_ref, g_ref, w_ref, qn_ref, kn_ref = refs[:5]
    q_ref, k_ref, v_ref, ug_ref, kf_ref, vf_ref = refs[5 + n_side:11 + n_side]
    slab_ref = refs[-1]
    if n_side:
        _cast_pad_kernel(*refs[5:5 + n_side], *refs[11 + n_side:11 + 2 * n_side], scale=1.0)
    i = pl.program_id(0)
    nb, tb, d = x_ref.shape
    d_att = N_HEADS * HEAD_DIM
    pitch, _ = _regroup_geometry(nb, tb)
    h = _rms(x_ref[...].reshape(nb * tb, d), g_ref[...]).astype(BF16)
    u = _dot(h, w_ref[:, 3 * d_att:])
    for j in range(slab_ref.shape[0]):
        for b in range(nb):
            slab_ref[j, b * pitch:b * pitch + tb, :] = u[b * tb:(b + 1) * tb, j * LANES:(j + 1) * LANES]
    _tokens_to_groups(slab_ref, ug_ref, nb=nb, tb=tb)

    def head_norm(y, gain):
        return [_rms(y[:, hd * HEAD_DIM:(hd + 1) * HEAD_DIM], gain) for hd in range(N_HEADS)]

    def store_tokens(ref, y):
        for b in range(nb):
            ref[b] = y[b * tb:(b + 1) * tb].astype(ref.dtype)

    qs = head_norm(_dot(h, w_ref[:, 0:d_att]), qn_ref[...] * (HEAD_DIM ** -0.5))
    store_tokens(q_ref, jnp.concatenate(qs, axis=1))
    ks = head_norm(_dot(h, w_ref[:, d_att:2 * d_att]), kn_ref[...])
    store_tokens(k_ref, jnp.concatenate(ks, axis=1))
    v = _dot(h, w_ref[:, 2 * d_att:3 * d_att])
    store_tokens(v_ref, v)

    @pl.when(i >= first_tail_block)
    def _():
        for b in range(nb):
            rows = slice(b * tb, (b + 1) * tb)
            for hd in range(N_HEADS):
                head_rows = pl.ds(hd, tb, stride=N_HEADS)
                kf_ref[b, head_rows, :] = ks[hd][rows]
                vf_ref[b, head_rows, :] = v[rows, hd * HEAD_DIM:(hd + 1) * HEAD_DIM]


def _proj(x, g, w, qn, kn, *, tb, side=()):
    nb, seq, d = x.shape
    d_att = N_HEADS * HEAD_DIM
    d_ssm = w.shape[1] - 3 * d_att
    n_g = d_ssm // SSM_GROUP
    n_blocks = seq // tb
    tail = min(ATT_LEFT, seq)
    first_tail_block = n_blocks - tail // tb
    pitch, ncl = _regroup_geometry(nb, tb)
    tok = lambda i: (0, i, 0)
    const = lambda i: (0, 0)
    tail_map = lambda i: (0, jnp.maximum(i - first_tail_block, 0), 0)
    side_in, side_out, side_shape = [], [], []
    for a in side:
        rows, cols = a.shape[0] // n_blocks, -(-a.shape[1] // FF_PAD) * FF_PAD
        assert rows * n_blocks == a.shape[0] and rows % 16 == 0
        side_in.append(pl.BlockSpec((rows, a.shape[1]), lambda i: (i, 0)))
        side_out.append(pl.BlockSpec((rows, cols), lambda i: (i, 0)))
        side_shape.append(jax.ShapeDtypeStruct((a.shape[0], cols), BF16))
    return pl.pallas_call(
        functools.partial(_proj_kernel, first_tail_block=first_tail_block, n_side=len(side)),
        grid=(n_blocks,),
        in_specs=[
            pl.BlockSpec((nb, tb, d), tok),
            pl.BlockSpec((1, d), const),
            pl.BlockSpec(w.shape, const, pipeline_mode=pl.Buffered(1)),
            pl.BlockSpec((1, HEAD_DIM), const),
            pl.BlockSpec((1, HEAD_DIM), const),
        ] + side_in,
        out_specs=[
            pl.BlockSpec((nb, tb, d_att), tok),
            pl.BlockSpec((nb, tb, d_att), tok),
            pl.BlockSpec((nb, tb, d_att), tok),
            pl.BlockSpec((n_g, ncl * nb, SSM_W), tok),
            pl.BlockSpec((nb, tb * N_HEADS, HEAD_DIM), tail_map),
            pl.BlockSpec((nb, tb * N_HEADS, HEAD_DIM), tail_map),
        ] + side_out,
        out_shape=[
            jax.ShapeDtypeStruct((nb, seq, d_att), BF16),
            jax.ShapeDtypeStruct((nb, seq, d_att), BF16),
            jax.ShapeDtypeStruct((nb, seq, d_att), BF16),
            jax.ShapeDtypeStruct((n_g, (seq // SSM_T) * nb, SSM_W), BF16),
            jax.ShapeDtypeStruct((nb, tail * N_HEADS, HEAD_DIM), F32),
            jax.ShapeDtypeStruct((nb, tail * N_HEADS, HEAD_DIM), F32),
        ] + side_shape,
        scratch_shapes=[pltpu.VMEM((d_ssm // LANES, nb * pitch, LANES), F32)],
        compiler_params=pltpu.CompilerParams(
            dimension_semantics=("arbitrary",), vmem_limit_bytes=VMEM_LIMIT_V7X),
        name="proj",
    )(x, g, w, qn, kn, *side)


def _softmax_pv(s, v):
    m = jnp.max(s, axis=-1, keepdims=True)
    p = jnp.exp(s - m)
    l = jnp.sum(p, axis=-1, keepdims=True)
    return _dot(p.astype(BF16), v) / l


def _qk(q, k):
    return lax.dot_general(q, k, (((1,), (1,)), ((), ())), preferred_element_type=F32)


def _attn_prompt_kernel(*refs, n_blocks, n_side, side_scale):
    q_ref, k_ref, v_ref, bias_ref = refs[:4]
    o_ref = refs[4 + n_side]
    s_ref, p_ref, linv_ref = refs[-3:]
    if n_side:
        _cast_pad_kernel(*refs[4:4 + n_side], *refs[5 + n_side:5 + 2 * n_side], scale=side_scale)
    lead = ATT_LEFT // ATT_QB
    n_full = n_blocks - lead

    def lanes(h):
        return slice(h * HEAD_DIM, (h + 1) * HEAD_DIM)

    def start(i):
        return pl.multiple_of(i * ATT_QB, ATT_QB)

    for i in range(min(lead, n_blocks)):
        kw = (i + 1) * ATT_QB
        rows = slice(i * ATT_QB, (i + 1) * ATT_QB)
        ss = [_qk(q_ref[rows, lanes(h)], k_ref[0:kw, lanes(h)]) for h in range(ATT_HEADS_PER_STEP)]
        for h in range(ATT_HEADS_PER_STEP):
            s = ss[h] + bias_ref[h, :, ATT_KW - kw:]
            o_ref[rows, lanes(h)] = _softmax_pv(s, v_ref[0:kw, lanes(h)]).astype(BF16)
    if n_full <= 0:
        return

    def scores(h, i, slot):
        s_ref[slot] = _qk(q_ref[pl.ds(start(i), ATT_QB), lanes(h)],
                          k_ref[pl.ds(start(i) - ATT_LEFT, ATT_KW), lanes(h)])

    def softmax(h, slot):
        win = ATT_KW - LANES
        for c in range(ATT_QB // CHUNK):
            rows = slice(c * CHUNK, (c + 1) * CHUNK)
            c0 = (c * CHUNK) // LANES * LANES
            assert c0 + win <= ATT_KW and c0 <= c * CHUNK and (c + LEFT_CHUNKS + 1) * CHUNK <= c0 + win
            s = s_ref[slot, rows, c0:c0 + win] + bias_ref[h, rows, c0:c0 + win]
            p = jnp.exp(s - jnp.max(s, axis=-1, keepdims=True))
            p_ref[slot, rows, c0:c0 + win] = p.astype(BF16)
            dead = slice(win, ATT_KW) if c0 == 0 else slice(0, c0)
            p_ref[slot, rows, dead] = jnp.zeros((CHUNK, LANES), BF16)
            linv_ref[slot, rows] = jnp.broadcast_to(1.0 / jnp.sum(p, axis=-1, keepdims=True), (CHUNK, HEAD_DIM))

    def output(h, i, slot):
        v = v_ref[pl.ds(start(i) - ATT_LEFT, ATT_KW), lanes(h)]
        o_ref[pl.ds(start(i), ATT_QB), lanes(h)] = (_dot(p_ref[slot], v) * linv_ref[slot]).astype(BF16)

    for h0 in range(0, ATT_HEADS_PER_STEP, 2):
        h1 = h0 + 1
        scores(h0, lead, 0)
        scores(h1, lead, 1)
        softmax(h0, 0)

        def body(r, carry):
            i = lead + r
            scores(h0, i + 1, 0)
            output(h0, i, 0)
            softmax(h1, 1)
            scores(h1, i + 1, 1)
            output(h1, i, 1)
            softmax(h0, 0)
            return carry

        lax.fori_loop(0, n_full - 1, body, 0)
        output(h0, n_blocks - 1, 0)
        softmax(h1, 1)
        output(h1, n_blocks - 1, 1)


def _attn_prompt(q, k, v, bias, *, side=(), side_scale=1.0):
    batch, seq, d_att = q.shape
    hs = ATT_HEADS_PER_STEP
    n_hg = N_HEADS // hs
    blk = pl.BlockSpec((None, seq, hs * HEAD_DIM), lambda b, h: (b, 0, h))
    side_in, side_out, side_shape = [], [], []
    for a in side:
        cols, rows = a.shape[1] // (batch * n_hg), -(-a.shape[0] // FF_PAD) * FF_PAD
        assert cols * batch * n_hg == a.shape[1] and cols % LANES == 0
        side_in.append(pl.BlockSpec((a.shape[0], cols), lambda b, h: (0, b * n_hg + h)))
        side_out.append(pl.BlockSpec((rows, cols), lambda b, h: (0, b * n_hg + h)))
        side_shape.append(jax.ShapeDtypeStruct((rows, a.shape[1]), BF16))
    return pl.pallas_call(
        functools.partial(_attn_prompt_kernel, n_blocks=seq // ATT_QB, n_side=len(side), side_scale=side_scale),
        grid=(batch, n_hg),
        in_specs=[blk, blk, blk, pl.BlockSpec((hs, ATT_QB, ATT_KW), lambda b, h: (h, 0, 0))] + side_in,
        out_specs=[blk] + side_out,
        out_shape=[jax.ShapeDtypeStruct((batch, seq, d_att), BF16)] + side_shape,
        scratch_shapes=[pltpu.VMEM((2, ATT_QB, ATT_KW), F32), pltpu.VMEM((2, ATT_QB, ATT_KW), BF16),
                        pltpu.VMEM((2, ATT_QB, HEAD_DIM), F32)],
        compiler_params=pltpu.CompilerParams(
            dimension_semantics=("parallel", "parallel"), vmem_limit_bytes=VMEM_LIMIT_V7X),
        name="attn_prompt",
    )(q, k, v, bias, *side)


def _attn_sample_kernel(q_ref, kn_ref, vn_ref, kc_ref, vc_ref, bias_ref, o_ref, ko_ref, vo_ref, *, w_cache, seq):
    def head_rows(ref, n, hd):
        return ref[pl.ds(hd, n, stride=N_HEADS), :].astype(BF16)

    for hd in range(N_HEADS):
        sl = slice(hd * HEAD_DIM, (hd + 1) * HEAD_DIM)
        q = q_ref[:, sl]
        s1 = _qk(q, head_rows(kc_ref, w_cache, hd)) + bias_ref[hd, :, :w_cache]
        s2 = _qk(q, head_rows(kn_ref, seq, hd)) + bias_ref[hd, :, w_cache:]
        m = jnp.maximum(jnp.max(s1, axis=-1, keepdims=True), jnp.max(s2, axis=-1, keepdims=True))
        p1 = jnp.exp(s1 - m)
        p2 = jnp.exp(s2 - m)
        l = jnp.sum(p1, axis=-1, keepdims=True) + jnp.sum(p2, axis=-1, keepdims=True)
        o = (_dot(p1.astype(BF16), head_rows(vc_ref, w_cache, hd))
             + _dot(p2.astype(BF16), head_rows(vn_ref, seq, hd)))
        o_ref[:, sl] = (o / l).astype(BF16)

    keep = (w_cache - seq) * N_HEADS
    for new_ref, old_ref, out_ref in ((kn_ref, kc_ref, ko_ref), (vn_ref, vc_ref, vo_ref)):
        out_ref[:keep, :] = old_ref[seq * N_HEADS:, :]
        out_ref[keep:, :] = new_ref[...]


def _attn_sample(q, kn, vn, kc, vc, bias):
    batch, seq, d_att = q.shape
    w_cache = kc.shape[1] // N_HEADS
    assert seq <= w_cache
    tok = pl.BlockSpec((None, seq, d_att), lambda b: (b, 0, 0))
    new = pl.BlockSpec((None, seq * N_HEADS, HEAD_DIM), lambda b: (b, 0, 0))
    cache = pl.BlockSpec((None, w_cache * N_HEADS, HEAD_DIM), lambda b: (b, 0, 0))
    return pl.pallas_call(
        functools.partial(_attn_sample_kernel, w_cache=w_cache, seq=seq),
        grid=(batch,),
        in_specs=[tok, new, new, cache, cache, pl.BlockSpec(bias.shape, lambda b: (0, 0, 0))],
        out_specs=[tok, cache, cache],
        out_shape=[jax.ShapeDtypeStruct((batch, seq, d_att), BF16),
                   jax.ShapeDtypeStruct(kc.shape, F32), jax.ShapeDtypeStruct(vc.shape, F32)],
        compiler_params=pltpu.CompilerParams(
            dimension_semantics=("parallel",), vmem_limit_bytes=VMEM_LIMIT_V7X),
        name="attn_sample",
    )(q, kn, vn, kc, vc, bias)


def _ssm_kernel(u_ref, lag_ref, bm_ref, cm_ref, a_ref, s0_ref, y_ref, sf_ref, sl_ref, sp_ref, km_ref,
                *, rows, n_chunks, groups):
    half = 2 * SSM_STATE
    lane = lax.broadcasted_iota(jnp.int32, (SSM_GROUP, SSM_W), 1)
    for gi in range(groups):
        lag = lag_ref[gi]
        for t_in in range(SSM_T):
            shifted = lag if t_in == 0 else pltpu.roll(lag, t_in * SSM_GROUP, 1)
            km_ref[gi, t_in * SSM_GROUP:(t_in + 1) * SSM_GROUP, :] = jnp.where(
                lane >= t_in * SSM_GROUP, shifted, 0.0).astype(BF16)
        sl_ref[gi] = _dot(u_ref[gi], bm_ref[gi])

    coef = []
    for gi in range(groups):
        coef.append(tuple(jnp.broadcast_to(a_ref[gi, r:r + 1, :], (rows, half)) for r in range(3)))

    def body(c, carry):
        r0 = pl.multiple_of(c * rows, rows)
        new = []
        for gi in range(groups):
            s, sw = carry[gi]
            a1, a2, a2w = coef[gi]
            sp_ref[gi, pl.ds(r0, rows), :] = s
            loc = sl_ref[gi, pl.ds(r0, rows), :]
            new.append((a1 * s + a2 * sw + loc[:, :half], a1 * sw + a2w * s + loc[:, half:]))
        return tuple(new)

    init = tuple((s0_ref[gi, 0], s0_ref[gi, 1]) for gi in range(groups))
    last = lax.fori_loop(0, n_chunks, body, init)
    for gi in range(groups):
        sf_ref[gi] = last[gi][0]
        y = _dot(u_ref[gi], km_ref[gi]) + _dot(sp_ref[gi].astype(BF16), cm_ref[gi])
        y_ref[gi] = jax.nn.gelu(y).astype(BF16)


def _ssm(ug, lagk, bm, cm, acoef, s0, *, rows, groups):
    n_g, n_rows, _ = ug.shape
    n_chunks = n_rows // rows
    half = 2 * SSM_STATE
    g3 = lambda g: (g, 0, 0)
    return pl.pallas_call(
        functools.partial(_ssm_kernel, rows=rows, n_chunks=n_chunks, groups=groups),
        grid=(n_g // groups,),
        in_specs=[
            pl.BlockSpec((groups, n_rows, SSM_W), g3),
            pl.BlockSpec((groups, SSM_GROUP, SSM_W), g3),
            pl.BlockSpec((groups, SSM_W, 2 * half), g3),
            pl.BlockSpec((groups, half, SSM_W), g3),
            pl.BlockSpec((groups, 3, half), g3),
            pl.BlockSpec((groups, 2, rows, half), lambda g: (g, 0, 0, 0)),
        ],
        out_specs=[
            pl.BlockSpec((groups, n_rows, SSM_W), g3),
            pl.BlockSpec((groups, rows, half), g3),
        ],
        out_shape=[
            jax.ShapeDtypeStruct((n_g, n_rows, SSM_W), BF16),
            jax.ShapeDtypeStruct((n_g, rows, half), F32),
        ],
        scratch_shapes=[
            pltpu.VMEM((groups, n_rows, 2 * half), F32),
            pltpu.VMEM((groups, n_rows, half), F32),
            pltpu.VMEM((groups, SSM_W, SSM_W), BF16),
        ],
        compiler_params=pltpu.CompilerParams(
            dimension_semantics=("parallel",), vmem_limit_bytes=VMEM_LIMIT_V7X),
        name="ssm",
    )(ug, lagk, bm, cm, acoef, s0)


def _mix_kernel(att_ref, yg_ref, x_ref, wglu_ref, bglu_ref, ga_ref, gs_ref, wout_ref, o_ref, slab_ref, y_ref):
    nb, tb, d = x_ref.shape
    d_att = att_ref.shape[2]
    d_ssm = y_ref.shape[1]
    pitch, _ = _regroup_geometry(nb, tb)
    att = att_ref[...].reshape(nb * tb, d_att).astype(F32)
    mix_a = _rms(att, ga_ref[...]).astype(BF16)
    o = x_ref[...].reshape(nb * tb, d) + _dot(mix_a, wout_ref[:d_att, :])
    _groups_to_tokens(yg_ref, slab_ref, nb=nb, tb=tb)
    for j in range(slab_ref.shape[0]):
        for b in range(nb):
            y_ref[b * tb:(b + 1) * tb, j * LANES:(j + 1) * LANES] = slab_ref[j, b * pitch:b * pitch + tb, :].astype(BF16)
    glu = _dot(y_ref[...], wglu_ref[...]) + bglu_ref[...]
    ssm_out = glu[:, :d_ssm] * jax.nn.sigmoid(glu[:, d_ssm:])
    mix_s = _rms(ssm_out, gs_ref[...]).astype(BF16)
    o = o + _dot(mix_s, wout_ref[d_att:, :])
    o_ref[...] = o.reshape(nb, tb, d)


def _mix(att, yg, x, wglu, bglu, ga, gs, wout, *, tb):
    nb, seq, d = x.shape
    d_att = att.shape[2]
    n_g = yg.shape[0]
    d_ssm = n_g * SSM_GROUP
    pitch, ncl = _regroup_geometry(nb, tb)
    tok = lambda i: (0, i, 0)
    const = lambda i: (0, 0)
    once = lambda a: pl.BlockSpec(a.shape, const, pipeline_mode=pl.Buffered(1))
    return pl.pallas_call(
        _mix_kernel,
        grid=(seq // tb,),
        in_specs=[
            pl.BlockSpec((nb, tb, d_att), tok),
            pl.BlockSpec((n_g, ncl * nb, SSM_W), tok),
            pl.BlockSpec((nb, tb, d), tok),
            once(wglu), once(bglu), once(ga), once(gs), once(wout),
        ],
        out_specs=pl.BlockSpec((nb, tb, d), tok),
        out_shape=jax.ShapeDtypeStruct((nb, seq, d), F32),
        scratch_shapes=[pltpu.VMEM((d_ssm // LANES, nb * pitch, LANES), F32),
                        pltpu.VMEM((nb * tb, d_ssm), BF16)],
        compiler_params=pltpu.CompilerParams(
            dimension_semantics=("parallel",), vmem_limit_bytes=VMEM_LIMIT_V7X),
        name="mix",
    )(att, yg, x, wglu, bglu, ga, gs, wout)


def _ssm_matrices(lam_re, lam_im, log_dt, b_re, b_im, c_re, c_im, d_skip):
    hp = lax.Precision.HIGHEST
    n_g = lam_re.shape[0]
    dt = jnp.exp(log_dt)[:, None]
    n = jnp.arange(SSM_T + 1, dtype=F32)[:, None, None]
    mag = jnp.exp(lam_re * dt * n)
    ang = lam_im * dt * n
    pw_re, pw_im = mag * jnp.cos(ang), mag * jnp.sin(ang)
    x, y = pw_re[1] - 1.0, pw_im[1]
    den = lam_re * lam_re + lam_im * lam_im
    z_re, z_im = (x * lam_re + y * lam_im) / den, (y * lam_re - x * lam_im) / den
    bb_re = z_re[..., None] * b_re - z_im[..., None] * b_im
    bb_im = z_re[..., None] * b_im + z_im[..., None] * b_re
    pb_re = pw_re[..., None] * bb_re - pw_im[..., None] * bb_im
    pb_im = pw_re[..., None] * bb_im + pw_im[..., None] * bb_re
    lagk = (jnp.einsum('gop,dgpi->gido', c_re, pb_re[:SSM_T], precision=hp)
            - jnp.einsum('gop,dgpi->gido', c_im, pb_im[:SSM_T], precision=hp))
    skip = jnp.eye(SSM_GROUP, dtype=F32)[None] * d_skip[:, :, None]
    lagk = lagk.at[:, :, 0, :].add(skip).reshape(n_g, SSM_GROUP, SSM_W)
    inj_re = pb_re[:SSM_T][::-1].transpose(1, 0, 3, 2).reshape(n_g, SSM_W, SSM_STATE)
    inj_im = pb_im[:SSM_T][::-1].transpose(1, 0, 3, 2).reshape(n_g, SSM_W, SSM_STATE)
    bm = jnp.concatenate([inj_re, inj_im, inj_im, inj_re], axis=-1)
    cp_re = c_re[None] * pw_re[1:, :, None, :] - c_im[None] * pw_im[1:, :, None, :]
    cp_im = c_re[None] * pw_im[1:, :, None, :] + c_im[None] * pw_re[1:, :, None, :]
    out_re = cp_re.transpose(1, 3, 0, 2).reshape(n_g, SSM_STATE, SSM_W)
    out_im = -cp_im.transpose(1, 3, 0, 2).reshape(n_g, SSM_STATE, SSM_W)
    cm = jnp.concatenate([out_re, out_im], axis=1)
    ar, ai = pw_re[SSM_T], pw_im[SSM_T]
    acoef = jnp.stack([jnp.concatenate([ar, ar], -1), jnp.concatenate([-ai, ai], -1),
                       jnp.concatenate([ai, -ai], -1)], axis=1)
    return lagk, bm.astype(BF16), cm.astype(BF16), acoef


def _bias_table(rel_bias):
    n_heads = rel_bias.shape[0]
    ext = ATT_QB + ATT_KW
    n_edge = ATT_LEFT - REL_CLIP + 1
    assert ATT_KW - n_edge == 2 * REL_CLIP - 1 and ATT_LEFT >= REL_CLIP
    far = rel_bias[:, 2 * REL_CLIP:]
    row = jnp.concatenate([jnp.broadcast_to(far, (n_heads, n_edge)), rel_bias[:, 1:2 * REL_CLIP][:, ::-1],
                           jnp.broadcast_to(far, (n_heads, ATT_QB))], axis=1)
    skew = jnp.tile(row, (1, ATT_QB))[:, :ATT_QB * (ext - 1)].reshape(n_heads, ATT_QB, ext - 1)
    return skew[:, :, :ATT_KW].astype(F32)


def _band_mask():
    r = jnp.arange(ATT_QB)[:, None] // CHUNK
    j = jnp.arange(ATT_KW)[None, :] // CHUNK
    return (j >= r) & (j <= r + LEFT_CHUNKS)


def _state_rows(s_re, s_im):
    s = jnp.concatenate([s_re, s_im], -1).transpose(1, 0, 2)
    sw = jnp.concatenate([s_im, s_re], -1).transpose(1, 0, 2)
    return jnp.stack([s, sw], axis=1)


def _cast_pad_kernel(*refs, scale):
    n = len(refs) // 2
    for w_ref, o_ref in zip(refs[:n], refs[n:]):
        r, c = w_ref.shape
        w = w_ref[...]
        o_ref[:r, :c] = (w if scale == 1.0 else w * scale).astype(BF16)
        if o_ref.shape[0] > r:
            o_ref[r:, :] = jnp.zeros((o_ref.shape[0] - r, o_ref.shape[1]), BF16)
        if o_ref.shape[1] > c:
            o_ref[:, c:] = jnp.zeros((o_ref.shape[0], o_ref.shape[1] - c), BF16)


def _cast_pad(ws, *, axis, mult, block, scale=1.0):
    r, c = ws[0].shape
    padded = -(-ws[0].shape[axis] // mult) * mult
    if axis == 1:
        in_spec, out_spec = pl.BlockSpec((block, c), lambda i: (i, 0)), pl.BlockSpec((block, padded), lambda i: (i, 0))
        out_shape, steps = jax.ShapeDtypeStruct((r, padded), BF16), r // block
    else:
        in_spec, out_spec = pl.BlockSpec((r, block), lambda i: (0, i)), pl.BlockSpec((padded, block), lambda i: (0, i))
        out_shape, steps = jax.ShapeDtypeStruct((padded, c), BF16), c // block
    return pl.pallas_call(
        functools.partial(_cast_pad_kernel, scale=scale),
        grid=(steps,),
        in_specs=[in_spec] * len(ws),
        out_specs=[out_spec] * len(ws),
        out_shape=[out_shape] * len(ws),
        compiler_params=pltpu.CompilerParams(
            dimension_semantics=("parallel",), vmem_limit_bytes=VMEM_LIMIT_V7X),
        name="cast_pad",
    )(*ws)


def _side_jobs_fit(w_rows, w_cols, batch, proj_steps):
    att_steps = batch * (N_HEADS // ATT_HEADS_PER_STEP)
    return (w_rows.shape[0] % (16 * proj_steps) == 0) and (w_cols.shape[1] % (LANES * att_steps) == 0)


def _stream(x, p, *, tm, tf, tb, ssm_groups, cache=None):
    batch, seq, d = x.shape
    ffn = functools.partial(_ffn, tm=tm, tf=tf)
    x1 = ffn(x.reshape(batch * seq, d), p['g_ffn1'], p['ffn1_wg'], p['ffn1_wu'], p['ffn1_wd'], p['g_final'],
             final_norm=False).reshape(batch, seq, d)
    side_gu, side_d = p.get('ffn2_gate_up_f32', ()), p.get('ffn2_down_f32', ())
    if side_gu and not _side_jobs_fit(side_gu[0], side_d[0], batch, seq // tb):
        gu = _cast_pad(list(side_gu), axis=1, mult=FF_PAD, block=256)
        dn = _cast_pad(list(side_d), axis=0, mult=FF_PAD, block=256, scale=0.5)
        p.update(ffn2_wg=gu[0], ffn2_wu=gu[1], ffn2_wd=dn[0])
        del p['ffn2_gate_up_f32'], p['ffn2_down_f32']
        side_gu, side_d = (), ()
    q, k, v, ug, kf, vf, *gu = _proj(x1, p['g_mix'], p['w_in'], p['q_norm'], p['k_norm'], tb=tb, side=side_gu)
    n_g = ug.shape[0]
    if cache is None:
        att, *dn = _attn_prompt(q, k, v, p['bias_prompt'], side=side_d, side_scale=0.5)
        if gu:
            p.update(ffn2_wg=gu[0], ffn2_wu=gu[1], ffn2_wd=dn[0])
            del p['ffn2_gate_up_f32'], p['ffn2_down_f32']
        s0 = jnp.zeros((n_g, 2, batch, 2 * SSM_STATE), F32)
    else:
        ck, cv, s_re, s_im = cache
        w_cache = ck.shape[1]
        assert w_cache == ATT_LEFT and seq <= ATT_QB
        att, kf, vf = _attn_sample(q, kf, vf, ck.reshape(batch, w_cache * N_HEADS, HEAD_DIM),
                                   cv.reshape(batch, w_cache * N_HEADS, HEAD_DIM),
                                   p['bias_table'][:, :seq, :w_cache + seq])
        s0 = _state_rows(s_re, s_im)
    yg, sf = _ssm(ug, p['lagk'], p['bm'], p['cm'], p['acoef'], s0, rows=batch, groups=ssm_groups)
    x2 = _mix(att, yg, x1, p['w_glu'], p['b_glu'], p['g_att'], p['g_ssm'], p['w_out'], tb=tb)
    y = ffn(x2.reshape(batch * seq, d), p['g_ffn2'], p['ffn2_wg'], p['ffn2_wu'], p['ffn2_wd'], p['g_final'],
            final_norm=True).reshape(batch, seq, d)
    sf = sf.transpose(1, 0, 2)
    return y, kf, vf, sf[..., :SSM_STATE], sf[..., SSM_STATE:]


def kernel(x_prompt, x_sample, cache_attn_k, cache_attn_v, state_ssm_re, state_ssm_im, norm_ffn1, ffn1_w_gate, ffn1_w_up, ffn1_w_down, norm_mix, w_in, q_norm, k_norm, rel_bias, ssm_lambda_re, ssm_lambda_im, ssm_log_dt, ssm_b_re, ssm_b_im, ssm_c_re, ssm_c_im, ssm_d, w_glu, b_glu, norm_att_out, norm_ssm_out, w_out, norm_ffn2, ffn2_w_gate, ffn2_w_up, ffn2_w_down, norm_final):
    depth = norm_ffn1.shape[0]
    bs, ls, _ = x_sample.shape
    yp, ys = x_prompt, x_sample
    outs = [[] for _ in range(8)]
    for l in range(depth):
        lagk, bm, cm, acoef = _ssm_matrices(ssm_lambda_re[l], ssm_lambda_im[l], ssm_log_dt[l], ssm_b_re[l],
                                          ssm_b_im[l], ssm_c_re[l], ssm_c_im[l], ssm_d[l])
        row = lambda a: a[l][None, :]
        bias_table = _bias_table(rel_bias[l])
        wg1, wu1 = _cast_pad([ffn1_w_gate[l], ffn1_w_up[l]], axis=1, mult=FF_PAD, block=256)
        wd1, = _cast_pad([ffn1_w_down[l]], axis=0, mult=FF_PAD, block=256, scale=0.5)
        p = dict(
            g_ffn1=row(norm_ffn1), g_mix=row(norm_mix), g_att=row(norm_att_out), g_ssm=row(norm_ssm_out),
            g_ffn2=row(norm_ffn2), g_final=row(norm_final), q_norm=row(q_norm), k_norm=row(k_norm),
            ffn1_wg=wg1, ffn1_wu=wu1, ffn1_wd=wd1,
            ffn2_gate_up_f32=(ffn2_w_gate[l], ffn2_w_up[l]), ffn2_down_f32=(ffn2_w_down[l],),
            w_in=w_in[l].astype(BF16), w_glu=w_glu[l].astype(BF16), b_glu=row(b_glu), w_out=w_out[l].astype(BF16),
            bias_table=bias_table, bias_prompt=jnp.where(_band_mask()[None], bias_table, NEG_INF),
            lagk=lagk, bm=bm, cm=cm, acoef=acoef,
        )
        yp, kp, vp, rp, ip = _stream(yp, p, tm=1024, tf=FF_PAD, tb=CHUNK, ssm_groups=4)
        ys, kd, vd, rd, idd = _stream(ys, p, tm=bs * ls, tf=FF_PAD, tb=ls, ssm_groups=8,
                                      cache=(cache_attn_k[l], cache_attn_v[l], state_ssm_re[l], state_ssm_im[l]))
        as_cache = lambda a: a.reshape(a.shape[0], -1, N_HEADS, HEAD_DIM)
        for lst, val in zip(outs, (as_cache(kp), as_cache(vp), rp, ip, as_cache(kd), as_cache(vd), rd, idd)):
            lst.append(val)
    return (yp, ys) + tuple(jnp.stack(o) for o in outs)
```

```python
import functools

import jax
import jax.numpy as jnp
from jax import lax
from jax.experimental import pallas as pl
from jax.experimental.pallas import tpu as pltpu

EPS = 1e-6
NEG_INF = -1e30
CHUNK = 64
LEFT_CHUNKS = 8
ATT_LEFT = LEFT_CHUNKS * CHUNK
REL_CLIP = 256
N_HEADS = 8
HEAD_DIM = 128
SSM_GROUP = 16
SSM_STATE = 64
LANES = 128
SSM_T = 16
SSM_W = SSM_T * SSM_GROUP
GROUPS_PER_TILE = LANES // SSM_GROUP
ATT_QB = 256
ATT_KW = ATT_QB + ATT_LEFT
ATT_HEADS_PER_STEP = 4
REGROUP_PAD = 8
FF_PAD = 512
VMEM_LIMIT_V7X = 60 * 1024 * 1024

BF16 = jnp.bfloat16
F32 = jnp.float32


def _dot(a, b):
    return jnp.dot(a, b, preferred_element_type=F32)


def _rms(x, g):
    return x * lax.rsqrt(jnp.mean(x * x, axis=-1, keepdims=True) + EPS) * g


def _ffn_kernel(x_ref, g_ref, wg_hbm, wu_hbm, wd_hbm, gf_ref, o_ref, xn_ref, wg_buf, wu_buf, wd_buf, sem,
                *, tf, n_chunks, final_norm):
    i = pl.program_id(0)
    first_slot = (i * n_chunks) % 2

    def chunk_copies(j, slot):
        cols = pl.ds(pl.multiple_of(j * tf, tf), tf)
        return (pltpu.make_async_copy(wg_hbm.at[:, cols], wg_buf.at[slot], sem.at[0, slot]),
                pltpu.make_async_copy(wu_hbm.at[:, cols], wu_buf.at[slot], sem.at[1, slot]),
                pltpu.make_async_copy(wd_hbm.at[cols, :], wd_buf.at[slot], sem.at[2, slot]))

    @pl.when(i == 0)
    def _():
        for c in chunk_copies(0, 0):
            c.start()

    x = x_ref[...]
    xn_ref[...] = _rms(x, g_ref[...]).astype(BF16)
    o_ref[...] = x

    def body(j, carry):
        slot = (first_slot + j) % 2
        nxt = jnp.where(j + 1 == n_chunks, 0, j + 1)
        for c in chunk_copies(nxt, 1 - slot):
            c.start()
        for c in chunk_copies(j, slot):
            c.wait()
        xn = xn_ref[...]
        a = _dot(xn, wg_buf[slot])
        b = _dot(xn, wu_buf[slot])
        h = (a * jax.nn.sigmoid(a) * b).astype(BF16)
        o_ref[...] += _dot(h, wd_buf[slot])
        return carry

    lax.fori_loop(0, n_chunks, body, 0)

    @pl.when(i == pl.num_programs(0) - 1)
    def _():
        for c in chunk_copies(0, (first_slot + n_chunks) % 2):
            c.wait()

    if final_norm:
        o_ref[...] = _rms(o_ref[...], gf_ref[...])


def _ffn(x, g, wg, wu, wd_half, gf, *, tm, tf, final_norm):
    n, d = x.shape
    fpad = wg.shape[1]
    hbm = pl.BlockSpec(memory_space=pl.ANY)
    return pl.pallas_call(
        functools.partial(_ffn_kernel, tf=tf, n_chunks=fpad // tf, final_norm=final_norm),
        grid=(n // tm,),
        in_specs=[
            pl.BlockSpec((tm, d), lambda i: (i, 0)),
            pl.BlockSpec((1, d), lambda i: (0, 0)),
            hbm, hbm, hbm,
            pl.BlockSpec((1, d), lambda i: (0, 0)),
        ],
        out_specs=pl.BlockSpec((tm, d), lambda i: (i, 0)),
        out_shape=jax.ShapeDtypeStruct((n, d), F32),
        scratch_shapes=[pltpu.VMEM((tm, d), BF16),
                        pltpu.VMEM((2, d, tf), BF16), pltpu.VMEM((2, d, tf), BF16), pltpu.VMEM((2, tf, d), BF16),
                        pltpu.SemaphoreType.DMA((3, 2))],
        compiler_params=pltpu.CompilerParams(
            dimension_semantics=("arbitrary",), vmem_limit_bytes=VMEM_LIMIT_V7X),
        name="ffn",
    )(x, g, wg, wu, wd_half, gf)


def _granule_transpose(vs):
    gran = lax.broadcasted_iota(jnp.int32, vs[0].shape, 1) // SSM_GROUP
    for s in (4, 2, 1):
        upper = (gran & s) != 0
        new = list(vs)
        for a in range(GROUPS_PER_TILE):
            if a & s == 0:
                lo, hi = vs[a], vs[a + s]
                new[a] = jnp.where(upper, pltpu.roll(hi, s * SSM_GROUP, 1), lo)
                new[a + s] = jnp.where(upper, hi, pltpu.roll(lo, LANES - s * SSM_GROUP, 1))
        vs = new
    return vs


def _regroup_geometry(nb, tb):
    return tb + REGROUP_PAD, tb // SSM_T


def _tokens_to_groups(slab_ref, ug_ref, *, nb, tb):
    pitch, ncl = _regroup_geometry(nb, tb)
    for j in range(slab_ref.shape[0]):
        xs = [jnp.concatenate([slab_ref[j, pl.ds(c * SSM_T + t, nb, stride=pitch), :] for c in range(ncl)], axis=0)
              for t in range(SSM_T)]
        lo = _granule_transpose(xs[:GROUPS_PER_TILE])
        hi = _granule_transpose(xs[GROUPS_PER_TILE:])
        for gl in range(GROUPS_PER_TILE):
            ug_ref[j * GROUPS_PER_TILE + gl] = jnp.concatenate([lo[gl], hi[gl]], axis=1).astype(ug_ref.dtype)


def _groups_to_tokens(yg_ref, slab_ref, *, nb, tb):
    pitch, ncl = _regroup_geometry(nb, tb)
    for j in range(slab_ref.shape[0]):
        ds = [yg_ref[j * GROUPS_PER_TILE + gl].astype(F32) for gl in range(GROUPS_PER_TILE)]
        lo = _granule_transpose([d[:, :LANES] for d in ds])
        hi = _granule_transpose([d[:, LANES:] for d in ds])
        for t, x in enumerate(lo + hi):
            for c in range(ncl):
                slab_ref[j, pl.ds(c * SSM_T + t, nb, stride=pitch), :] = x[c * nb:(c + 1) * nb]


def _proj_kernel(*refs, first_tail_block, n_side):
    x_ref, g_ref, w_ref, qn_ref, kn_ref = refs[:5]
    q_ref, k_ref, v_ref, ug_ref, kf_ref, vf_ref = refs[5 + n_side:11 + n_side]
    slab_ref = refs[-1]
    if n_side:
        _cast_pad_kernel(*refs[5:5 + n_side], *refs[11 + n_side:11 + 2 * n_side], scale=1.0)
    i = pl.program_id(0)
    nb, tb, d = x_ref.shape
    d_att = N_HEADS * HEAD_DIM
    pitch, _ = _regroup_geometry(nb, tb)
    h = _rms(x_ref[...].reshape(nb * tb, d), g_ref[...]).astype(BF16)
    u = _dot(h, w_ref[:, 3 * d_att:])
    for j in range(slab_ref.shape[0]):
        for b in range(nb):
            slab_ref[j, b * pitch:b * pitch + tb, :] = u[b * tb:(b + 1) * tb, j * LANES:(j + 1) * LANES]
    _tokens_to_groups(slab_ref, ug_ref, nb=nb, tb=tb)

    def head_norm(y, gain):
        return [_rms(y[:, hd * HEAD_DIM:(hd + 1) * HEAD_DIM], gain) for hd in range(N_HEADS)]

    def store_tokens(ref, y):
        for b in range(nb):
            ref[b] = y[b * tb:(b + 1) * tb].astype(ref.dtype)

    qs = head_norm(_dot(h, w_ref[:, 0:d_att]), qn_ref[...] * (HEAD_DIM ** -0.5))
    store_tokens(q_ref, jnp.concatenate(qs, axis=1))
    ks = head_norm(_dot(h, w_ref[:, d_att:2 * d_att]), kn_ref[...])
    store_tokens(k_ref, jnp.concatenate(ks, axis=1))
    v = _dot(h, w_ref[:, 2 * d_att:3 * d_att])
    store_tokens(v_ref, v)

    @pl.when(i >= first_tail_block)
    def _():
        for b in range(nb):
            rows = slice(b * tb, (b + 1) * tb)
            for hd in range(N_HEADS):
                head_rows = pl.ds(hd, tb, stride=N_HEADS)
                kf_ref[b, head_rows, :] = ks[hd][rows]
                vf_ref[b, head_rows, :] = v[rows, hd * HEAD_DIM:(hd + 1) * HEAD_DIM]


def _proj(x, g, w, qn, kn, *, tb, side=()):
    nb, seq, d = x.shape
    d_att = N_HEADS * HEAD_DIM
    d_ssm = w.shape[1] - 3 * d_att
    n_g = d_ssm // SSM_GROUP
    n_blocks = seq // tb
    tail = min(ATT_LEFT, seq)
    first_tail_block = n_blocks - tail // tb
    pitch, ncl = _regroup_geometry(nb, tb)
    tok = lambda i: (0, i, 0)
    const = lambda i: (0, 0)
    tail_map = lambda i: (0, jnp.maximum(i - first_tail_block, 0), 0)
    side_in, side_out, side_shape = [], [], []
    for a in side:
        rows, cols = a.shape[0] // n_blocks, -(-a.shape[1] // FF_PAD) * FF_PAD
        assert rows * n_blocks == a.shape[0] and rows % 16 == 0
        side_in.append(pl.BlockSpec((rows, a.shape[1]), lambda i: (i, 0)))
        side_out.append(pl.BlockSpec((rows, cols), lambda i: (i, 0)))
        side_shape.append(jax.ShapeDtypeStruct((a.shape[0], cols), BF16))
    return pl.pallas_call(
        functools.partial(_proj_kernel, first_tail_block=first_tail_block, n_side=len(side)),
        grid=(n_blocks,),
        in_specs=[
            pl.BlockSpec((nb, tb, d), tok),
            pl.BlockSpec((1, d), const),
            pl.BlockSpec(w.shape, const, pipeline_mode=pl.Buffered(1)),
            pl.BlockSpec((1, HEAD_DIM), const),
            pl.BlockSpec((1, HEAD_DIM), const),
        ] + side_in,
        out_specs=[
            pl.BlockSpec((nb, tb, d_att), tok),
            pl.BlockSpec((nb, tb, d_att), tok),
            pl.BlockSpec((nb, tb, d_att), tok),
            pl.BlockSpec((n_g, ncl * nb, SSM_W), tok),
            pl.BlockSpec((nb, tb * N_HEADS, HEAD_DIM), tail_map),
            pl.BlockSpec((nb, tb * N_HEADS, HEAD_DIM), tail_map),
        ] + side_out,
        out_shape=[
            jax.ShapeDtypeStruct((nb, seq, d_att), BF16),
            jax.ShapeDtypeStruct((nb, seq, d_att), BF16),
            jax.ShapeDtypeStruct((nb, seq, d_att), BF16),
            jax.ShapeDtypeStruct((n_g, (seq // SSM_T) * nb, SSM_W), BF16),
            jax.ShapeDtypeStruct((nb, tail * N_HEADS, HEAD_DIM), F32),
            jax.ShapeDtypeStruct((nb, tail * N_HEADS, HEAD_DIM), F32),
        ] + side_shape,
        scratch_shapes=[pltpu.VMEM((d_ssm // LANES, nb * pitch, LANES), F32)],
        compiler_params=pltpu.CompilerParams(
            dimension_semantics=("arbitrary",), vmem_limit_bytes=VMEM_LIMIT_V7X),
        name="proj",
    )(x, g, w, qn, kn, *side)


def _softmax_pv(s, v):
    m = jnp.max(s, axis=-1, keepdims=True)
    p = jnp.exp(s - m)
    l = jnp.sum(p, axis=-1, keepdims=True)
    return _dot(p.astype(BF16), v) / l


def _qk(q, k):
    return lax.dot_general(q, k, (((1,), (1,)), ((), ())), preferred_element_type=F32)


def _attn_prompt_kernel(*refs, n_blocks, side_scales):
    n_side = len(side_scales)
    q_ref, k_ref, v_ref, bias_ref = refs[:4]
    o_ref = refs[4 + n_side]
    s_ref, p_ref, linv_ref = refs[-3:]
    for w_ref, wb_ref, scale in zip(refs[4:4 + n_side], refs[5 + n_side:5 + 2 * n_side], side_scales):
        _cast_pad_kernel(w_ref, wb_ref, scale=scale)
    lead = ATT_LEFT // ATT_QB
    n_full = n_blocks - lead

    def lanes(h):
        return slice(h * HEAD_DIM, (h + 1) * HEAD_DIM)

    def start(i):
        return pl.multiple_of(i * ATT_QB, ATT_QB)

    for i in range(min(lead, n_blocks)):
        kw = (i + 1) * ATT_QB
        rows = slice(i * ATT_QB, (i + 1) * ATT_QB)
        ss = [_qk(q_ref[rows, lanes(h)], k_ref[0:kw, lanes(h)]) for h in range(ATT_HEADS_PER_STEP)]
        for h in range(ATT_HEADS_PER_STEP):
            s = ss[h] + bias_ref[h, :, ATT_KW - kw:]
            o_ref[rows, lanes(h)] = _softmax_pv(s, v_ref[0:kw, lanes(h)]).astype(BF16)
    if n_full <= 0:
        return

    def scores(h, i, slot):
        s_ref[slot] = _qk(q_ref[pl.ds(start(i), ATT_QB), lanes(h)],
                          k_ref[pl.ds(start(i) - ATT_LEFT, ATT_KW), lanes(h)])

    def softmax(h, slot):
        win = ATT_KW - LANES
        for c in range(ATT_QB // CHUNK):
            rows = slice(c * CHUNK, (c + 1) * CHUNK)
            c0 = (c * CHUNK) // LANES * LANES
            assert c0 + win <= ATT_KW and c0 <= c * CHUNK and (c + LEFT_CHUNKS + 1) * CHUNK <= c0 + win
            s = s_ref[slot, rows, c0:c0 + win] + bias_ref[h, rows, c0:c0 + win]
            p = jnp.exp(s - jnp.max(s, axis=-1, keepdims=True))
            p_ref[slot, rows, c0:c0 + win] = p.astype(BF16)
            dead = slice(win, ATT_KW) if c0 == 0 else slice(0, c0)
            p_ref[slot, rows, dead] = jnp.zeros((CHUNK, LANES), BF16)
            linv_ref[slot, rows] = jnp.broadcast_to(1.0 / jnp.sum(p, axis=-1, keepdims=True), (CHUNK, HEAD_DIM))

    def output(h, i, slot):
        v = v_ref[pl.ds(start(i) - ATT_LEFT, ATT_KW), lanes(h)]
        o_ref[pl.ds(start(i), ATT_QB), lanes(h)] = (_dot(p_ref[slot], v) * linv_ref[slot]).astype(BF16)

    for h0 in range(0, ATT_HEADS_PER_STEP, 2):
        h1 = h0 + 1
        scores(h0, lead, 0)
        scores(h1, lead, 1)
        softmax(h0, 0)

        def body(r, carry):
            i = lead + r
            scores(h0, i + 1, 0)
            output(h0, i, 0)
            softmax(h1, 1)
            scores(h1, i + 1, 1)
            output(h1, i, 1)
            softmax(h0, 0)
            return carry

        lax.fori_loop(0, n_full - 1, body, 0)
        output(h0, n_blocks - 1, 0)
        softmax(h1, 1)
        output(h1, n_blocks - 1, 1)


def _attn_prompt(q, k, v, bias, *, side=(), side_scales=()):
    batch, seq, d_att = q.shape
    hs = ATT_HEADS_PER_STEP
    n_hg = N_HEADS // hs
    blk = pl.BlockSpec((None, seq, hs * HEAD_DIM), lambda b, h: (b, 0, h))
    side_in, side_out, side_shape = [], [], []
    for a in side:
        cols, rows = a.shape[1] // (batch * n_hg), -(-a.shape[0] // FF_PAD) * FF_PAD
        assert cols * batch * n_hg == a.shape[1] and cols % LANES == 0
        side_in.append(pl.BlockSpec((a.shape[0], cols), lambda b, h: (0, b * n_hg + h)))
        side_out.append(pl.BlockSpec((rows, cols), lambda b, h: (0, b * n_hg + h)))
        side_shape.append(jax.ShapeDtypeStruct((rows, a.shape[1]), BF16))
    return pl.pallas_call(
        functools.partial(_attn_prompt_kernel, n_blocks=seq // ATT_QB, side_scales=tuple(side_scales)),
        grid=(batch, n_hg),
        in_specs=[blk, blk, blk, pl.BlockSpec((hs, ATT_QB, ATT_KW), lambda b, h: (h, 0, 0))] + side_in,
        out_specs=[blk] + side_out,
        out_shape=[jax.ShapeDtypeStruct((batch, seq, d_att), BF16)] + side_shape,
        scratch_shapes=[pltpu.VMEM((2, ATT_QB, ATT_KW), F32), pltpu.VMEM((2, ATT_QB, ATT_KW), BF16),
                        pltpu.VMEM((2, ATT_QB, HEAD_DIM), F32)],
        compiler_params=pltpu.CompilerParams(
            dimension_semantics=("parallel", "parallel"), vmem_limit_bytes=VMEM_LIMIT_V7X),
        name="attn_prompt",
    )(q, k, v, bias, *side)


def _attn_sample_kernel(q_ref, kn_ref, vn_ref, kc_ref, vc_ref, bias_ref, o_ref, ko_ref, vo_ref, *, w_cache, seq):
    def head_rows(ref, n, hd):
        return ref[pl.ds(hd, n, stride=N_HEADS), :].astype(BF16)

    for hd in range(N_HEADS):
        sl = slice(hd * HEAD_DIM, (hd + 1) * HEAD_DIM)
        q = q_ref[:, sl]
        s1 = _qk(q, head_rows(kc_ref, w_cache, hd)) + bias_ref[hd, :, :w_cache]
        s2 = _qk(q, head_rows(kn_ref, seq, hd)) + bias_ref[hd, :, w_cache:]
        m = jnp.maximum(jnp.max(s1, axis=-1, keepdims=True), jnp.max(s2, axis=-1, keepdims=True))
        p1 = jnp.exp(s1 - m)
        p2 = jnp.exp(s2 - m)
        l = jnp.sum(p1, axis=-1, keepdims=True) + jnp.sum(p2, axis=-1, keepdims=True)
        o = (_dot(p1.astype(BF16), head_rows(vc_ref, w_cache, hd))
             + _dot(p2.astype(BF16), head_rows(vn_ref, seq, hd)))
        o_ref[:, sl] = (o / l).astype(BF16)

    keep = (w_cache - seq) * N_HEADS
    for new_ref, old_ref, out_ref in ((kn_ref, kc_ref, ko_ref), (vn_ref, vc_ref, vo_ref)):
        out_ref[:keep, :] = old_ref[seq * N_HEADS:, :]
        out_ref[keep:, :] = new_ref[...]


def _attn_sample(q, kn, vn, kc, vc, bias):
    batch, seq, d_att = q.shape
    w_cache = kc.shape[1] // N_HEADS
    assert seq <= w_cache
    tok = pl.BlockSpec((None, seq, d_att), lambda b: (b, 0, 0))
    new = pl.BlockSpec((None, seq * N_HEADS, HEAD_DIM), lambda b: (b, 0, 0))
    cache = pl.BlockSpec((None, w_cache * N_HEADS, HEAD_DIM), lambda b: (b, 0, 0))
    return pl.pallas_call(
        functools.partial(_attn_sample_kernel, w_cache=w_cache, seq=seq),
        grid=(batch,),
        in_specs=[tok, new, new, cache, cache, pl.BlockSpec(bias.shape, lambda b: (0, 0, 0))],
        out_specs=[tok, cache, cache],
        out_shape=[jax.ShapeDtypeStruct((batch, seq, d_att), BF16),
                   jax.ShapeDtypeStruct(kc.shape, F32), jax.ShapeDtypeStruct(vc.shape, F32)],
        compiler_params=pltpu.CompilerParams(
            dimension_semantics=("parallel",), vmem_limit_bytes=VMEM_LIMIT_V7X),
        name="attn_sample",
    )(q, kn, vn, kc, vc, bias)


def _ssm_kernel(u_ref, lag_ref, bm_ref, cm_ref, a_ref, s0_ref, y_ref, sf_ref, sl_ref, sp_ref, km_ref,
                *, rows, n_chunks, groups):
    half = 2 * SSM_STATE
    lane = lax.broadcasted_iota(jnp.int32, (SSM_GROUP, SSM_W), 1)
    for gi in range(groups):
        lag = lag_ref[gi]
        for t_in in range(SSM_T):
            shifted = lag if t_in == 0 else pltpu.roll(lag, t_in * SSM_GROUP, 1)
            km_ref[gi, t_in * SSM_GROUP:(t_in + 1) * SSM_GROUP, :] = jnp.where(
                lane >= t_in * SSM_GROUP, shifted, 0.0).astype(BF16)
        sl_ref[gi] = _dot(u_ref[gi], bm_ref[gi])

    coef = []
    for gi in range(groups):
        coef.append(tuple(jnp.broadcast_to(a_ref[gi, r:r + 1, :], (rows, half)) for r in range(3)))

    def body(c, carry):
        r0 = pl.multiple_of(c * rows, rows)
        new = []
        for gi in range(groups):
            s, sw = carry[gi]
            a1, a2, a2w = coef[gi]
            sp_ref[gi, pl.ds(r0, rows), :] = s
            loc = sl_ref[gi, pl.ds(r0, rows), :]
            new.append((a1 * s + a2 * sw + loc[:, :half], a1 * sw + a2w * s + loc[:, half:]))
        return tuple(new)

    init = tuple((s0_ref[gi, 0], s0_ref[gi, 1]) for gi in range(groups))
    last = lax.fori_loop(0, n_chunks, body, init)
    for gi in range(groups):
        sf_ref[gi] = last[gi][0]
        y = _dot(u_ref[gi], km_ref[gi]) + _dot(sp_ref[gi].astype(BF16), cm_ref[gi])
        y_ref[gi] = jax.nn.gelu(y).astype(BF16)


def _ssm(ug, lagk, bm, cm, acoef, s0, *, rows, groups):
    n_g, n_rows, _ = ug.shape
    n_chunks = n_rows // rows
    half = 2 * SSM_STATE
    g3 = lambda g: (g, 0, 0)
    return pl.pallas_call(
        functools.partial(_ssm_kernel, rows=rows, n_chunks=n_chunks, groups=groups),
        grid=(n_g // groups,),
        in_specs=[
            pl.BlockSpec((groups, n_rows, SSM_W), g3),
            pl.BlockSpec((groups, SSM_GROUP, SSM_W), g3),
            pl.BlockSpec((groups, SSM_W, 2 * half), g3),
            pl.BlockSpec((groups, half, SSM_W), g3),
            pl.BlockSpec((groups, 3, half), g3),
            pl.BlockSpec((groups, 2, rows, half), lambda g: (g, 0, 0, 0)),
        ],
        out_specs=[
            pl.BlockSpec((groups, n_rows, SSM_W), g3),
            pl.BlockSpec((groups, rows, half), g3),
        ],
        out_shape=[
            jax.ShapeDtypeStruct((n_g, n_rows, SSM_W), BF16),
            jax.ShapeDtypeStruct((n_g, rows, half), F32),
        ],
        scratch_shapes=[
            pltpu.VMEM((groups, n_rows, 2 * half), F32),
            pltpu.VMEM((groups, n_rows, half), F32),
            pltpu.VMEM((groups, SSM_W, SSM_W), BF16),
        ],
        compiler_params=pltpu.CompilerParams(
            dimension_semantics=("parallel",), vmem_limit_bytes=VMEM_LIMIT_V7X),
        name="ssm",
    )(ug, lagk, bm, cm, acoef, s0)


def _mix_kernel(att_ref, yg_ref, x_ref, wglu_ref, bglu_ref, ga_ref, gs_ref, wout_ref, o_ref, slab_ref, y_ref):
    nb, tb, d = x_ref.shape
    d_att = att_ref.shape[2]
    d_ssm = y_ref.shape[1]
    pitch, _ = _regroup_geometry(nb, tb)
    att = att_ref[...].reshape(nb * tb, d_att).astype(F32)
    mix_a = _rms(att, ga_ref[...]).astype(BF16)
    o = x_ref[...].reshape(nb * tb, d) + _dot(mix_a, wout_ref[:d_att, :])
    _groups_to_tokens(yg_ref, slab_ref, nb=nb, tb=tb)
    for j in range(slab_ref.shape[0]):
        for b in range(nb):
            y_ref[b * tb:(b + 1) * tb, j * LANES:(j + 1) * LANES] = slab_ref[j, b * pitch:b * pitch + tb, :].astype(BF16)
    glu = _dot(y_ref[...], wglu_ref[...]) + bglu_ref[...]
    ssm_out = glu[:, :d_ssm] * jax.nn.sigmoid(glu[:, d_ssm:])
    mix_s = _rms(ssm_out, gs_ref[...]).astype(BF16)
    o = o + _dot(mix_s, wout_ref[d_att:, :])
    o_ref[...] = o.reshape(nb, tb, d)


def _mix(att, yg, x, wglu, bglu, ga, gs, wout, *, tb):
    nb, seq, d = x.shape
    d_att = att.shape[2]
    n_g = yg.shape[0]
    d_ssm = n_g * SSM_GROUP
    pitch, ncl = _regroup_geometry(nb, tb)
    tok = lambda i: (0, i, 0)
    const = lambda i: (0, 0)
    once = lambda a: pl.BlockSpec(a.shape, const, pipeline_mode=pl.Buffered(1))
    return pl.pallas_call(
        _mix_kernel,
        grid=(seq // tb,),
        in_specs=[
            pl.BlockSpec((nb, tb, d_att), tok),
            pl.BlockSpec((n_g, ncl * nb, SSM_W), tok),
            pl.BlockSpec((nb, tb, d), tok),
            once(wglu), once(bglu), once(ga), once(gs), once(wout),
        ],
        out_specs=pl.BlockSpec((nb, tb, d), tok),
        out_shape=jax.ShapeDtypeStruct((nb, seq, d), F32),
        scratch_shapes=[pltpu.VMEM((d_ssm // LANES, nb * pitch, LANES), F32),
                        pltpu.VMEM((nb * tb, d_ssm), BF16)],
        compiler_params=pltpu.CompilerParams(
            dimension_semantics=("parallel",), vmem_limit_bytes=VMEM_LIMIT_V7X),
        name="mix",
    )(att, yg, x, wglu, bglu, ga, gs, wout)


def _ssm_matrices(lam_re, lam_im, log_dt, b_re, b_im, c_re, c_im, d_skip):
    hp = lax.Precision.HIGHEST
    n_g = lam_re.shape[0]
    dt = jnp.exp(log_dt)[:, None]
    n = jnp.arange(SSM_T + 1, dtype=F32)
    mag = jnp.exp((lam_re * dt)[..., None] * n)
    ang = (lam_im * dt)[..., None] * n
    pw_re, pw_im = mag * jnp.cos(ang), mag * jnp.sin(ang)
    x, y = pw_re[..., 1] - 1.0, pw_im[..., 1]
    den = lam_re * lam_re + lam_im * lam_im
    z_re, z_im = ((x * lam_re + y * lam_im) / den)[:, None, :], ((y * lam_re - x * lam_im) / den)[:, None, :]
    bt_re, bt_im = b_re.transpose(0, 2, 1), b_im.transpose(0, 2, 1)
    bb_re, bb_im = z_re * bt_re - z_im * bt_im, z_re * bt_im + z_im * bt_re
    ct_re, ct_im = c_re.transpose(0, 2, 1), c_im.transpose(0, 2, 1)
    per_step = lambda a: jnp.repeat(a, SSM_GROUP, axis=-1)
    per_chan = lambda a: jnp.tile(a, (1, 1, SSM_T + 1))
    cp_re = per_chan(ct_re) * per_step(pw_re) - per_chan(ct_im) * per_step(pw_im)
    cp_im = per_chan(ct_re) * per_step(pw_im) + per_chan(ct_im) * per_step(pw_re)
    cm = jnp.concatenate([cp_re[..., SSM_GROUP:], -cp_im[..., SSM_GROUP:]], axis=1)
    lagk = (jnp.einsum('gip,gpc->gic', bb_re, cp_re[..., :SSM_W], precision=hp)
            - jnp.einsum('gip,gpc->gic', bb_im, cp_im[..., :SSM_W], precision=hp))
    lagk = lagk.at[:, :, :SSM_GROUP].add(jnp.eye(SSM_GROUP, dtype=F32)[None] * d_skip[:, :, None])
    back_re = pw_re[..., SSM_T - 1::-1].transpose(0, 2, 1)[:, :, None, :]
    back_im = pw_im[..., SSM_T - 1::-1].transpose(0, 2, 1)[:, :, None, :]
    inj_re = (back_re * bb_re[:, None] - back_im * bb_im[:, None]).reshape(n_g, SSM_W, SSM_STATE)
    inj_im = (back_re * bb_im[:, None] + back_im * bb_re[:, None]).reshape(n_g, SSM_W, SSM_STATE)
    bm = jnp.concatenate([inj_re, inj_im, inj_im, inj_re], axis=-1)
    ar, ai = pw_re[..., SSM_T], pw_im[..., SSM_T]
    acoef = jnp.stack([jnp.concatenate([ar, ar], -1), jnp.concatenate([-ai, ai], -1),
                       jnp.concatenate([ai, -ai], -1)], axis=1)
    return lagk, bm.astype(BF16), cm.astype(BF16), acoef


def _bias_table(rel_bias):
    n_heads = rel_bias.shape[0]
    ext = ATT_QB + ATT_KW
    n_edge = ATT_LEFT - REL_CLIP + 1
    assert ATT_KW - n_edge == 2 * REL_CLIP - 1 and ATT_LEFT >= REL_CLIP
    far = rel_bias[:, 2 * REL_CLIP:]
    row = jnp.concatenate([jnp.broadcast_to(far, (n_heads, n_edge)), rel_bias[:, 1:2 * REL_CLIP][:, ::-1],
                           jnp.broadcast_to(far, (n_heads, ATT_QB))], axis=1)
    skew = jnp.tile(row, (1, ATT_QB))[:, :ATT_QB * (ext - 1)].reshape(n_heads, ATT_QB, ext - 1)
    return skew[:, :, :ATT_KW].astype(F32)


def _band_mask():
    r = jnp.arange(ATT_QB)[:, None] // CHUNK
    j = jnp.arange(ATT_KW)[None, :] // CHUNK
    return (j >= r) & (j <= r + LEFT_CHUNKS)


def _state_rows(s_re, s_im):
    s = jnp.concatenate([s_re, s_im], -1).transpose(1, 0, 2)
    sw = jnp.concatenate([s_im, s_re], -1).transpose(1, 0, 2)
    return jnp.stack([s, sw], axis=1)


def _cast_pad_kernel(*refs, scale):
    n = len(refs) // 2
    for w_ref, o_ref in zip(refs[:n], refs[n:]):
        r, c = w_ref.shape
        w = w_ref[...]
        o_ref[:r, :c] = (w if scale == 1.0 else w * scale).astype(BF16)
        if o_ref.shape[0] > r:
            o_ref[r:, :] = jnp.zeros((o_ref.shape[0] - r, o_ref.shape[1]), BF16)
        if o_ref.shape[1] > c:
            o_ref[:, c:] = jnp.zeros((o_ref.shape[0], o_ref.shape[1] - c), BF16)


def _cast_pad(ws, *, axis, mult, block, scale=1.0):
    r, c = ws[0].shape
    padded = -(-ws[0].shape[axis] // mult) * mult
    if axis == 1:
        in_spec, out_spec = pl.BlockSpec((block, c), lambda i: (i, 0)), pl.BlockSpec((block, padded), lambda i: (i, 0))
        out_shape, steps = jax.ShapeDtypeStruct((r, padded), BF16), r // block
    else:
        in_spec, out_spec = pl.BlockSpec((r, block), lambda i: (0, i)), pl.BlockSpec((padded, block), lambda i: (0, i))
        out_shape, steps = jax.ShapeDtypeStruct((padded, c), BF16), c // block
    return pl.pallas_call(
        functools.partial(_cast_pad_kernel, scale=scale),
        grid=(steps,),
        in_specs=[in_spec] * len(ws),
        out_specs=[out_spec] * len(ws),
        out_shape=[out_shape] * len(ws),
        compiler_params=pltpu.CompilerParams(
            dimension_semantics=("parallel",), vmem_limit_bytes=VMEM_LIMIT_V7X),
        name="cast_pad",
    )(*ws)


def _side_jobs_fit(w_rows, w_cols, batch, proj_steps):
    att_steps = batch * (N_HEADS // ATT_HEADS_PER_STEP)
    return (w_rows.shape[0] % (16 * proj_steps) == 0) and (w_cols.shape[1] % (LANES * att_steps) == 0)


def _stream(x, p, *, tm, tf, tb, ssm_groups, cache=None):
    batch, seq, d = x.shape
    ffn = functools.partial(_ffn, tm=tm, tf=tf)
    x1 = ffn(x.reshape(batch * seq, d), p['g_ffn1'], p['ffn1_wg'], p['ffn1_wu'], p['ffn1_wd'], p['g_final'],
             final_norm=False).reshape(batch, seq, d)
    proj_side, att_side = p.pop('proj_side', ()), p.pop('att_side', ())
    if proj_side and (cache is not None or not _side_jobs_fit(proj_side[0][1], att_side[0][1], batch, seq // tb)):
        for name, w, scale in proj_side:
            p[name], = _cast_pad([w], axis=1, mult=FF_PAD, block=256, scale=scale)
        for name, w, scale in att_side:
            p[name], = _cast_pad([w], axis=0, mult=FF_PAD, block=256, scale=scale)
        proj_side, att_side = (), ()
    q, k, v, ug, kf, vf, *cast = _proj(x1, p['g_mix'], p['w_in'], p['q_norm'], p['k_norm'], tb=tb,
                                       side=[w for _, w, _ in proj_side])
    p.update({name: c for (name, _, _), c in zip(proj_side, cast)})
    n_g = ug.shape[0]
    if cache is None:
        att, *cast = _attn_prompt(q, k, v, p['bias_prompt'], side=[w for _, w, _ in att_side],
                                  side_scales=[scale for _, _, scale in att_side])
        p.update({name: c for (name, _, _), c in zip(att_side, cast)})
        s0 = jnp.zeros((n_g, 2, batch, 2 * SSM_STATE), F32)
    else:
        ck, cv, s_re, s_im = cache
        w_cache = ck.shape[1]
        assert w_cache == ATT_LEFT and seq <= ATT_QB
        att, kf, vf = _attn_sample(q, kf, vf, ck.reshape(batch, w_cache * N_HEADS, HEAD_DIM),
                                   cv.reshape(batch, w_cache * N_HEADS, HEAD_DIM),
                                   p['bias_table'][:, :seq, :w_cache + seq])
        s0 = _state_rows(s_re, s_im)
    yg, sf = _ssm(ug, p['lagk'], p['bm'], p['cm'], p['acoef'], s0, rows=batch, groups=ssm_groups)
    x2 = _mix(att, yg, x1, p['w_glu'], p['b_glu'], p['g_att'], p['g_ssm'], p['w_out'], tb=tb)
    y = ffn(x2.reshape(batch * seq, d), p['g_ffn2'], p['ffn2_wg'], p['ffn2_wu'], p['ffn2_wd'], p['g_final'],
            final_norm=True).reshape(batch, seq, d)
    sf = sf.transpose(1, 0, 2)
    return y, kf, vf, sf[..., :SSM_STATE], sf[..., SSM_STATE:]


def kernel(x_prompt, x_sample, cache_attn_k, cache_attn_v, state_ssm_re, state_ssm_im, norm_ffn1, ffn1_w_gate, ffn1_w_up, ffn1_w_down, norm_mix, w_in, q_norm, k_norm, rel_bias, ssm_lambda_re, ssm_lambda_im, ssm_log_dt, ssm_b_re, ssm_b_im, ssm_c_re, ssm_c_im, ssm_d, w_glu, b_glu, norm_att_out, norm_ssm_out, w_out, norm_ffn2, ffn2_w_gate, ffn2_w_up, ffn2_w_down, norm_final):
    depth = norm_ffn1.shape[0]
    bs, ls, _ = x_sample.shape
    yp, ys = x_prompt, x_sample
    outs = [[] for _ in range(8)]
    for l in range(depth):
        lagk, bm, cm, acoef = _ssm_matrices(ssm_lambda_re[l], ssm_lambda_im[l], ssm_log_dt[l], ssm_b_re[l],
                                          ssm_b_im[l], ssm_c_re[l], ssm_c_im[l], ssm_d[l])
        row = lambda a: a[l][None, :]
        bias_table = _bias_table(rel_bias[l])
        wg1, wu1 = _cast_pad([ffn1_w_gate[l], ffn1_w_up[l]], axis=1, mult=FF_PAD, block=256)
        wd1, = _cast_pad([ffn1_w_down[l]], axis=0, mult=FF_PAD, block=256, scale=0.5)
        p = dict(
            g_ffn1=row(norm_ffn1), g_mix=row(norm_mix), g_att=row(norm_att_out), g_ssm=row(norm_ssm_out),
            g_ffn2=row(norm_ffn2), g_final=row(norm_final), q_norm=row(q_norm), k_norm=row(k_norm),
            ffn1_wg=wg1, ffn1_wu=wu1, ffn1_wd=wd1,
            proj_side=(('ffn2_wg', ffn2_w_gate[l], 1.0), ('ffn2_wu', ffn2_w_up[l], 1.0)),
            att_side=(('ffn2_wd', ffn2_w_down[l], 0.5), ('w_glu', w_glu[l], 1.0), ('w_out', w_out[l], 1.0)),
            w_in=w_in[l].astype(BF16), b_glu=row(b_glu),
            bias_table=bias_table, bias_prompt=jnp.where(_band_mask()[None], bias_table, NEG_INF),
            lagk=lagk, bm=bm, cm=cm, acoef=acoef,
        )
        yp, kp, vp, rp, ip = _stream(yp, p, tm=1024, tf=FF_PAD, tb=CHUNK, ssm_groups=4)
        ys, kd, vd, rd, idd = _stream(ys, p, tm=bs * ls, tf=FF_PAD, tb=ls, ssm_groups=8,
                                      cache=(cache_attn_k[l], cache_attn_v[l], state_ssm_re[l], state_ssm_im[l]))
        as_cache = lambda a: a.reshape(a.shape[0], -1, N_HEADS, HEAD_DIM)
        for lst, val in zip(outs, (as_cache(kp), as_cache(vp), rp, ip, as_cache(kd), as_cache(vd), rd, idd)):
            lst.append(val)
    return (yp, ys) + tuple(jnp.stack(o) for o in outs)
```

```python
import functools

import jax
import jax.numpy as jnp
from jax import lax
from jax.experimental import pallas as pl
from jax.experimental.pallas import tpu as pltpu

EPS = 1e-6
NEG_INF = -1e30
CHUNK = 64
LEFT_CHUNKS = 8
ATT_LEFT = LEFT_CHUNKS * CHUNK
REL_CLIP = 256
N_HEADS = 8
HEAD_DIM = 128
SSM_GROUP = 16
SSM_STATE = 64
LANES = 128
SSM_T = 16
SSM_W = SSM_T * SSM_GROUP
GROUPS_PER_TILE = LANES // SSM_GROUP
ATT_QB = 256
ATT_KW = ATT_QB + ATT_LEFT
ATT_HEADS_PER_STEP = 4
REGROUP_PAD = 8
FF_PAD = 512
VMEM_LIMIT_V7X = 60 * 1024 * 1024

BF16 = jnp.bfloat16
F32 = jnp.float32


def _dot(a, b):
    return jnp.dot(a, b, preferred_element_type=F32)


def _rms(x, g):
    return x * lax.rsqrt(jnp.mean(x * x, axis=-1, keepdims=True) + EPS) * g


def _ffn_kernel(x_ref, g_ref, wg_hbm, wu_hbm, wd_hbm, gf_ref, o_ref, xn_ref, wg_buf, wu_buf, wd_buf, sem,
                *, tf, n_chunks, final_norm):
    i = pl.program_id(0)
    first_slot = (i * n_chunks) % 2

    def chunk_copies(j, slot):
        cols = pl.ds(pl.multiple_of(j * tf, tf), tf)
        return (pltpu.make_async_copy(wg_hbm.at[:, cols], wg_buf.at[slot], sem.at[0, slot]),
                pltpu.make_async_copy(wu_hbm.at[:, cols], wu_buf.at[slot], sem.at[1, slot]),
                pltpu.make_async_copy(wd_hbm.at[cols, :], wd_buf.at[slot], sem.at[2, slot]))

    @pl.when(i == 0)
    def _():
        for c in chunk_copies(0, 0):
            c.start()

    x = x_ref[...]
    xn_ref[...] = _rms(x, g_ref[...]).astype(BF16)
    o_ref[...] = x

    def body(j, carry):
        slot = (first_slot + j) % 2
        nxt = jnp.where(j + 1 == n_chunks, 0, j + 1)
        for c in chunk_copies(nxt, 1 - slot):
            c.start()
        for c in chunk_copies(j, slot):
            c.wait()
        xn = xn_ref[...]
        a = _dot(xn, wg_buf[slot])
        b = _dot(xn, wu_buf[slot])
        h = (a * jax.nn.sigmoid(a) * b).astype(BF16)
        o_ref[...] += _dot(h, wd_buf[slot])
        return carry

    lax.fori_loop(0, n_chunks, body, 0)

    @pl.when(i == pl.num_programs(0) - 1)
    def _():
        for c in chunk_copies(0, (first_slot + n_chunks) % 2):
            c.wait()

    if final_norm:
        o_ref[...] = _rms(o_ref[...], gf_ref[...])


def _ffn(x, g, wg, wu, wd_half, gf, *, tm, tf, final_norm):
    n, d = x.shape
    fpad = wg.shape[1]
    hbm = pl.BlockSpec(memory_space=pl.ANY)
    return pl.pallas_call(
        functools.partial(_ffn_kernel, tf=tf, n_chunks=fpad // tf, final_norm=final_norm),
        grid=(n // tm,),
        in_specs=[
            pl.BlockSpec((tm, d), lambda i: (i, 0)),
            pl.BlockSpec((1, d), lambda i: (0, 0)),
            hbm, hbm, hbm,
            pl.BlockSpec((1, d), lambda i: (0, 0)),
        ],
        out_specs=pl.BlockSpec((tm, d), lambda i: (i, 0)),
        out_shape=jax.ShapeDtypeStruct((n, d), F32),
        scratch_shapes=[pltpu.VMEM((tm, d), BF16),
                        pltpu.VMEM((2, d, tf), BF16), pltpu.VMEM((2, d, tf), BF16), pltpu.VMEM((2, tf, d), BF16),
                        pltpu.SemaphoreType.DMA((3, 2))],
        compiler_params=pltpu.CompilerParams(
            dimension_semantics=("arbitrary",), vmem_limit_bytes=VMEM_LIMIT_V7X),
        name="ffn",
    )(x, g, wg, wu, wd_half, gf)


def _granule_transpose(vs):
    gran = lax.broadcasted_iota(jnp.int32, vs[0].shape, 1) // SSM_GROUP
    for s in (4, 2, 1):
        upper = (gran & s) != 0
        new = list(vs)
        for a in range(GROUPS_PER_TILE):
            if a & s == 0:
                lo, hi = vs[a], vs[a + s]
                new[a] = jnp.where(upper, pltpu.roll(hi, s * SSM_GROUP, 1), lo)
                new[a + s] = jnp.where(upper, hi, pltpu.roll(lo, LANES - s * SSM_GROUP, 1))
        vs = new
    return vs


def _regroup_geometry(nb, tb):
    return tb + REGROUP_PAD, tb // SSM_T


def _tokens_to_groups(slab_ref, ug_ref, *, nb, tb):
    pitch, ncl = _regroup_geometry(nb, tb)
    for j in range(slab_ref.shape[0]):
        xs = [jnp.concatenate([slab_ref[j, pl.ds(c * SSM_T + t, nb, stride=pitch), :] for c in range(ncl)], axis=0)
              for t in range(SSM_T)]
        lo = _granule_transpose(xs[:GROUPS_PER_TILE])
        hi = _granule_transpose(xs[GROUPS_PER_TILE:])
        for gl in range(GROUPS_PER_TILE):
            ug_ref[j * GROUPS_PER_TILE + gl] = jnp.concatenate([lo[gl], hi[gl]], axis=1).astype(ug_ref.dtype)


def _groups_to_tokens(yg_ref, slab_ref, *, nb, tb):
    pitch, ncl = _regroup_geometry(nb, tb)
    for j in range(slab_ref.shape[0]):
        ds = [yg_ref[j * GROUPS_PER_TILE + gl].astype(F32) for gl in range(GROUPS_PER_TILE)]
        lo = _granule_transpose([d[:, :LANES] for d in ds])
        hi = _granule_transpose([d[:, LANES:] for d in ds])
        for t, x in enumerate(lo + hi):
            for c in range(ncl):
                slab_ref[j, pl.ds(c * SSM_T + t, nb, stride=pitch), :] = x[c * nb:(c + 1) * nb]


def _proj_kernel(*refs, first_tail_block, n_side):
    x_ref, g_ref, w_ref, qn_ref, kn_ref = refs[:5]
    q_ref, k_ref, v_ref, ug_ref, kf_ref, vf_ref = refs[5 + n_side:11 + n_side]
    slab_ref = refs[-1]
    if n_side:
        _cast_pad_kernel(*refs[5:5 + n_side], *refs[11 + n_side:11 + 2 * n_side], scale=1.0)
    i = pl.program_id(0)
    nb, tb, d = x_ref.shape
    d_att = N_HEADS * HEAD_DIM
    pitch, _ = _regroup_geometry(nb, tb)
    h = _rms(x_ref[...].reshape(nb * tb, d), g_ref[...]).astype(BF16)
    u = _dot(h, w_ref[:, 3 * d_att:])
    for j in range(slab_ref.shape[0]):
        for b in range(nb):
            slab_ref[j, b * pitch:b * pitch + tb, :] = u[b * tb:(b + 1) * tb, j * LANES:(j + 1) * LANES]
    _tokens_to_groups(slab_ref, ug_ref, nb=nb, tb=tb)

    def head_norm(y, gain):
        return [_rms(y[:, hd * HEAD_DIM:(hd + 1) * HEAD_DIM], gain) for hd in range(N_HEADS)]

    def store_tokens(ref, y):
        for b in range(nb):
            ref[b] = y[b * tb:(b + 1) * tb].astype(ref.dtype)

    qs = head_norm(_dot(h, w_ref[:, 0:d_att]), qn_ref[...] * (HEAD_DIM ** -0.5))
    store_tokens(q_ref, jnp.concatenate(qs, axis=1))
    ks = head_norm(_dot(h, w_ref[:, d_att:2 * d_att]), kn_ref[...])
    store_tokens(k_ref, jnp.concatenate(ks, axis=1))
    v = _dot(h, w_ref[:, 2 * d_att:3 * d_att])
    store_tokens(v_ref, v)

    @pl.when(i >= first_tail_block)
    def _():
        for b in range(nb):
            rows = slice(b * tb, (b + 1) * tb)
            for hd in range(N_HEADS):
                head_rows = pl.ds(hd, tb, stride=N_HEADS)
                kf_ref[b, head_rows, :] = ks[hd][rows]
                vf_ref[b, head_rows, :] = v[rows, hd * HEAD_DIM:(hd + 1) * HEAD_DIM]


def _proj(x, g, w, qn, kn, *, tb, side=()):
    nb, seq, d = x.shape
    d_att = N_HEADS * HEAD_DIM
    d_ssm = w.shape[1] - 3 * d_att
    n_g = d_ssm // SSM_GROUP
    n_blocks = seq // tb
    tail = min(ATT_LEFT, seq)
    first_tail_block = n_blocks - tail // tb
    pitch, ncl = _regroup_geometry(nb, tb)
    tok = lambda i: (0, i, 0)
    const = lambda i: (0, 0)
    tail_map = lambda i: (0, jnp.maximum(i - first_tail_block, 0), 0)
    side_in, side_out, side_shape = [], [], []
    for a in side:
        rows, cols = a.shape[0] // n_blocks, -(-a.shape[1] // FF_PAD) * FF_PAD
        assert rows * n_blocks == a.shape[0] and rows % 16 == 0
        side_in.append(pl.BlockSpec((rows, a.shape[1]), lambda i: (i, 0)))
        side_out.append(pl.BlockSpec((rows, cols), lambda i: (i, 0)))
        side_shape.append(jax.ShapeDtypeStruct((a.shape[0], cols), BF16))
    return pl.pallas_call(
        functools.partial(_proj_kernel, first_tail_block=first_tail_block, n_side=len(side)),
        grid=(n_blocks,),
        in_specs=[
            pl.BlockSpec((nb, tb, d), tok),
            pl.BlockSpec((1, d), const),
            pl.BlockSpec(w.shape, const, pipeline_mode=pl.Buffered(1)),
            pl.BlockSpec((1, HEAD_DIM), const),
            pl.BlockSpec((1, HEAD_DIM), const),
        ] + side_in,
        out_specs=[
            pl.BlockSpec((nb, tb, d_att), tok),
            pl.BlockSpec((nb, tb, d_att), tok),
            pl.BlockSpec((nb, tb, d_att), tok),
            pl.BlockSpec((n_g, ncl * nb, SSM_W), tok),
            pl.BlockSpec((nb, tb * N_HEADS, HEAD_DIM), tail_map),
            pl.BlockSpec((nb, tb * N_HEADS, HEAD_DIM), tail_map),
        ] + side_out,
        out_shape=[
            jax.ShapeDtypeStruct((nb, seq, d_att), BF16),
            jax.ShapeDtypeStruct((nb, seq, d_att), BF16),
            jax.ShapeDtypeStruct((nb, seq, d_att), BF16),
            jax.ShapeDtypeStruct((n_g, (seq // SSM_T) * nb, SSM_W), BF16),
            jax.ShapeDtypeStruct((nb, tail * N_HEADS, HEAD_DIM), F32),
            jax.ShapeDtypeStruct((nb, tail * N_HEADS, HEAD_DIM), F32),
        ] + side_shape,
        scratch_shapes=[pltpu.VMEM((d_ssm // LANES, nb * pitch, LANES), F32)],
        compiler_params=pltpu.CompilerParams(
            dimension_semantics=("arbitrary",), vmem_limit_bytes=VMEM_LIMIT_V7X),
        name="proj",
    )(x, g, w, qn, kn, *side)


def _softmax_pv(s, v):
    m = jnp.max(s, axis=-1, keepdims=True)
    p = jnp.exp(s - m)
    l = jnp.sum(p, axis=-1, keepdims=True)
    return _dot(p.astype(BF16), v) / l


def _qk(q, k):
    return lax.dot_general(q, k, (((1,), (1,)), ((), ())), preferred_element_type=F32)


def _attn_prompt_kernel(*refs, n_blocks, side_scales):
    n_side = len(side_scales)
    q_ref, k_ref, v_ref, bias_ref = refs[:4]
    o_ref = refs[4 + n_side]
    s_ref, p_ref, linv_ref = refs[-3:]
    for w_ref, wb_ref, scale in zip(refs[4:4 + n_side], refs[5 + n_side:5 + 2 * n_side], side_scales):
        _cast_pad_kernel(w_ref, wb_ref, scale=scale)
    lead = ATT_LEFT // ATT_QB
    n_full = n_blocks - lead

    def lanes(h):
        return slice(h * HEAD_DIM, (h + 1) * HEAD_DIM)

    def start(i):
        return pl.multiple_of(i * ATT_QB, ATT_QB)

    for i in range(min(lead, n_blocks)):
        kw = (i + 1) * ATT_QB
        rows = slice(i * ATT_QB, (i + 1) * ATT_QB)
        ss = [_qk(q_ref[rows, lanes(h)], k_ref[0:kw, lanes(h)]) for h in range(ATT_HEADS_PER_STEP)]
        for h in range(ATT_HEADS_PER_STEP):
            s = ss[h] + bias_ref[h, :, ATT_KW - kw:]
            o_ref[rows, lanes(h)] = _softmax_pv(s, v_ref[0:kw, lanes(h)]).astype(BF16)
    if n_full <= 0:
        return

    def scores(h, i, slot):
        s_ref[slot] = _qk(q_ref[pl.ds(start(i), ATT_QB), lanes(h)],
                          k_ref[pl.ds(start(i) - ATT_LEFT, ATT_KW), lanes(h)])

    def softmax(h, slot):
        win = ATT_KW - LANES
        for c in range(ATT_QB // CHUNK):
            rows = slice(c * CHUNK, (c + 1) * CHUNK)
            c0 = (c * CHUNK) // LANES * LANES
            assert c0 + win <= ATT_KW and c0 <= c * CHUNK and (c + LEFT_CHUNKS + 1) * CHUNK <= c0 + win
            s = s_ref[slot, rows, c0:c0 + win] + bias_ref[h, rows, c0:c0 + win]
            p = jnp.exp(s - jnp.max(s, axis=-1, keepdims=True))
            p_ref[slot, rows, c0:c0 + win] = p.astype(BF16)
            dead = slice(win, ATT_KW) if c0 == 0 else slice(0, c0)
            p_ref[slot, rows, dead] = jnp.zeros((CHUNK, LANES), BF16)
            linv_ref[slot, rows] = jnp.broadcast_to(1.0 / jnp.sum(p, axis=-1, keepdims=True), (CHUNK, HEAD_DIM))

    def output(h, i, slot):
        v = v_ref[pl.ds(start(i) - ATT_LEFT, ATT_KW), lanes(h)]
        o_ref[pl.ds(start(i), ATT_QB), lanes(h)] = (_dot(p_ref[slot], v) * linv_ref[slot]).astype(BF16)

    for h0 in range(0, ATT_HEADS_PER_STEP, 2):
        h1 = h0 + 1
        scores(h0, lead, 0)
        scores(h1, lead, 1)
        softmax(h0, 0)

        def body(r, carry):
            i = lead + r
            scores(h0, i + 1, 0)
            output(h0, i, 0)
            softmax(h1, 1)
            scores(h1, i + 1, 1)
            output(h1, i, 1)
            softmax(h0, 0)
            return carry

        lax.fori_loop(0, n_full - 1, body, 0)
        output(h0, n_blocks - 1, 0)
        softmax(h1, 1)
        output(h1, n_blocks - 1, 1)


def _attn_prompt(q, k, v, bias, *, side=(), side_scales=()):
    batch, seq, d_att = q.shape
    hs = ATT_HEADS_PER_STEP
    n_hg = N_HEADS // hs
    blk = pl.BlockSpec((None, seq, hs * HEAD_DIM), lambda b, h: (b, 0, h))
    side_in, side_out, side_shape = [], [], []
    for a in side:
        cols, rows = a.shape[1] // (batch * n_hg), -(-a.shape[0] // FF_PAD) * FF_PAD
        assert cols * batch * n_hg == a.shape[1] and cols % LANES == 0
        side_in.append(pl.BlockSpec((a.shape[0], cols), lambda b, h: (0, b * n_hg + h)))
        side_out.append(pl.BlockSpec((rows, cols), lambda b, h: (0, b * n_hg + h)))
        side_shape.append(jax.ShapeDtypeStruct((rows, a.shape[1]), BF16))
    return pl.pallas_call(
        functools.partial(_attn_prompt_kernel, n_blocks=seq // ATT_QB, side_scales=tuple(side_scales)),
        grid=(batch, n_hg),
        in_specs=[blk, blk, blk, pl.BlockSpec((hs, ATT_QB, ATT_KW), lambda b, h: (h, 0, 0))] + side_in,
        out_specs=[blk] + side_out,
        out_shape=[jax.ShapeDtypeStruct((batch, seq, d_att), BF16)] + side_shape,
        scratch_shapes=[pltpu.VMEM((2, ATT_QB, ATT_KW), F32), pltpu.VMEM((2, ATT_QB, ATT_KW), BF16),
                        pltpu.VMEM((2, ATT_QB, HEAD_DIM), F32)],
        compiler_params=pltpu.CompilerParams(
            dimension_semantics=("parallel", "parallel"), vmem_limit_bytes=VMEM_LIMIT_V7X),
        name="attn_prompt",
    )(q, k, v, bias, *side)


def _attn_sample_kernel(q_ref, kn_ref, vn_ref, kc_ref, vc_ref, bias_ref, o_ref, ko_ref, vo_ref, *, w_cache, seq):
    def head_rows(ref, n, hd):
        return ref[pl.ds(hd, n, stride=N_HEADS), :].astype(BF16)

    for hd in range(N_HEADS):
        sl = slice(hd * HEAD_DIM, (hd + 1) * HEAD_DIM)
        q = q_ref[:, sl]
        s1 = _qk(q, head_rows(kc_ref, w_cache, hd)) + bias_ref[hd, :, :w_cache]
        s2 = _qk(q, head_rows(kn_ref, seq, hd)) + bias_ref[hd, :, w_cache:]
        m = jnp.maximum(jnp.max(s1, axis=-1, keepdims=True), jnp.max(s2, axis=-1, keepdims=True))
        p1 = jnp.exp(s1 - m)
        p2 = jnp.exp(s2 - m)
        l = jnp.sum(p1, axis=-1, keepdims=True) + jnp.sum(p2, axis=-1, keepdims=True)
        o = (_dot(p1.astype(BF16), head_rows(vc_ref, w_cache, hd))
             + _dot(p2.astype(BF16), head_rows(vn_ref, seq, hd)))
        o_ref[:, sl] = (o / l).astype(BF16)

    keep = (w_cache - seq) * N_HEADS
    for new_ref, old_ref, out_ref in ((kn_ref, kc_ref, ko_ref), (vn_ref, vc_ref, vo_ref)):
        out_ref[:keep, :] = old_ref[seq * N_HEADS:, :]
        out_ref[keep:, :] = new_ref[...]


def _attn_sample(q, kn, vn, kc, vc, bias):
    batch, seq, d_att = q.shape
    w_cache = kc.shape[1] // N_HEADS
    assert seq <= w_cache
    tok = pl.BlockSpec((None, seq, d_att), lambda b: (b, 0, 0))
    new = pl.BlockSpec((None, seq * N_HEADS, HEAD_DIM), lambda b: (b, 0, 0))
    cache = pl.BlockSpec((None, w_cache * N_HEADS, HEAD_DIM), lambda b: (b, 0, 0))
    return pl.pallas_call(
        functools.partial(_attn_sample_kernel, w_cache=w_cache, seq=seq),
        grid=(batch,),
        in_specs=[tok, new, new, cache, cache, pl.BlockSpec(bias.shape, lambda b: (0, 0, 0))],
        out_specs=[tok, cache, cache],
        out_shape=[jax.ShapeDtypeStruct((batch, seq, d_att), BF16),
                   jax.ShapeDtypeStruct(kc.shape, F32), jax.ShapeDtypeStruct(vc.shape, F32)],
        compiler_params=pltpu.CompilerParams(
            dimension_semantics=("parallel",), vmem_limit_bytes=VMEM_LIMIT_V7X),
        name="attn_sample",
    )(q, kn, vn, kc, vc, bias)


def _ssm_kernel(u_ref, lag_ref, bm_ref, cm_ref, a_ref, s0_ref, y_ref, sre_ref, sim_ref, sl_ref, sp_ref, km_ref,
                *, rows, n_chunks, groups):
    half = 2 * SSM_STATE
    lane = lax.broadcasted_iota(jnp.int32, (SSM_GROUP, SSM_W), 1)
    for gi in range(groups):
        lag = lag_ref[gi]
        for t_in in range(SSM_T):
            shifted = lag if t_in == 0 else pltpu.roll(lag, t_in * SSM_GROUP, 1)
            km_ref[gi, t_in * SSM_GROUP:(t_in + 1) * SSM_GROUP, :] = jnp.where(
                lane >= t_in * SSM_GROUP, shifted, 0.0).astype(BF16)
        sl_ref[gi] = _dot(u_ref[gi], bm_ref[gi])

    coef = []
    for gi in range(groups):
        coef.append(tuple(jnp.broadcast_to(a_ref[gi, r:r + 1, :], (rows, half)) for r in range(3)))

    def body(c, carry):
        r0 = pl.multiple_of(c * rows, rows)
        new = []
        for gi in range(groups):
            s, sw = carry[gi]
            a1, a2, a2w = coef[gi]
            sp_ref[gi, pl.ds(r0, rows), :] = s
            loc = sl_ref[gi, pl.ds(r0, rows), :]
            new.append((a1 * s + a2 * sw + loc[:, :half], a1 * sw + a2w * s + loc[:, half:]))
        return tuple(new)

    init = tuple((s0_ref[gi, 0], s0_ref[gi, 1]) for gi in range(groups))
    last = lax.fori_loop(0, n_chunks, body, init)
    for gi in range(groups):
        sre_ref[:, gi * SSM_STATE:(gi + 1) * SSM_STATE] = last[gi][0][:, :SSM_STATE]
        sim_ref[:, gi * SSM_STATE:(gi + 1) * SSM_STATE] = last[gi][0][:, SSM_STATE:]
        y = _dot(u_ref[gi], km_ref[gi]) + _dot(sp_ref[gi].astype(BF16), cm_ref[gi])
        y_ref[gi] = jax.nn.gelu(y).astype(BF16)


def _ssm(ug, lagk, bm, cm, acoef, s0, *, rows, groups):
    n_g, n_rows, _ = ug.shape
    n_chunks = n_rows // rows
    half = 2 * SSM_STATE
    g3 = lambda g: (g, 0, 0)
    return pl.pallas_call(
        functools.partial(_ssm_kernel, rows=rows, n_chunks=n_chunks, groups=groups),
        grid=(n_g // groups,),
        in_specs=[
            pl.BlockSpec((groups, n_rows, SSM_W), g3),
            pl.BlockSpec((groups, SSM_GROUP, SSM_W), g3),
            pl.BlockSpec((groups, SSM_W, 2 * half), g3),
            pl.BlockSpec((groups, half, SSM_W), g3),
            pl.BlockSpec((groups, 3, half), g3),
            pl.BlockSpec((groups, 2, rows, half), lambda g: (g, 0, 0, 0)),
        ],
        out_specs=[
            pl.BlockSpec((groups, n_rows, SSM_W), g3),
            pl.BlockSpec((rows, groups * SSM_STATE), lambda g: (0, g)),
            pl.BlockSpec((rows, groups * SSM_STATE), lambda g: (0, g)),
        ],
        out_shape=[
            jax.ShapeDtypeStruct((n_g, n_rows, SSM_W), BF16),
            jax.ShapeDtypeStruct((rows, n_g * SSM_STATE), F32),
            jax.ShapeDtypeStruct((rows, n_g * SSM_STATE), F32),
        ],
        scratch_shapes=[
            pltpu.VMEM((groups, n_rows, 2 * half), F32),
            pltpu.VMEM((groups, n_rows, half), F32),
            pltpu.VMEM((groups, SSM_W, SSM_W), BF16),
        ],
        compiler_params=pltpu.CompilerParams(
            dimension_semantics=("parallel",), vmem_limit_bytes=VMEM_LIMIT_V7X),
        name="ssm",
    )(ug, lagk, bm, cm, acoef, s0)


def _mix_kernel(att_ref, yg_ref, x_ref, wglu_ref, bglu_ref, ga_ref, gs_ref, wout_ref, o_ref, slab_ref, y_ref):
    nb, tb, d = x_ref.shape
    d_att = att_ref.shape[2]
    d_ssm = y_ref.shape[1]
    pitch, _ = _regroup_geometry(nb, tb)
    att = att_ref[...].reshape(nb * tb, d_att).astype(F32)
    mix_a = _rms(att, ga_ref[...]).astype(BF16)
    o = x_ref[...].reshape(nb * tb, d) + _dot(mix_a, wout_ref[:d_att, :])
    _groups_to_tokens(yg_ref, slab_ref, nb=nb, tb=tb)
    for j in range(slab_ref.shape[0]):
        for b in range(nb):
            y_ref[b * tb:(b + 1) * tb, j * LANES:(j + 1) * LANES] = slab_ref[j, b * pitch:b * pitch + tb, :].astype(BF16)
    glu = _dot(y_ref[...], wglu_ref[...]) + bglu_ref[...]
    ssm_out = glu[:, :d_ssm] * jax.nn.sigmoid(glu[:, d_ssm:])
    mix_s = _rms(ssm_out, gs_ref[...]).astype(BF16)
    o = o + _dot(mix_s, wout_ref[d_att:, :])
    o_ref[...] = o.reshape(nb, tb, d)


def _mix(att, yg, x, wglu, bglu, ga, gs, wout, *, tb):
    nb, seq, d = x.shape
    d_att = att.shape[2]
    n_g = yg.shape[0]
    d_ssm = n_g * SSM_GROUP
    pitch, ncl = _regroup_geometry(nb, tb)
    tok = lambda i: (0, i, 0)
    const = lambda i: (0, 0)
    once = lambda a: pl.BlockSpec(a.shape, const, pipeline_mode=pl.Buffered(1))
    return pl.pallas_call(
        _mix_kernel,
        grid=(seq // tb,),
        in_specs=[
            pl.BlockSpec((nb, tb, d_att), tok),
            pl.BlockSpec((n_g, ncl * nb, SSM_W), tok),
            pl.BlockSpec((nb, tb, d), tok),
            once(wglu), once(bglu), once(ga), once(gs), once(wout),
        ],
        out_specs=pl.BlockSpec((nb, tb, d), tok),
        out_shape=jax.ShapeDtypeStruct((nb, seq, d), F32),
        scratch_shapes=[pltpu.VMEM((d_ssm // LANES, nb * pitch, LANES), F32),
                        pltpu.VMEM((nb * tb, d_ssm), BF16)],
        compiler_params=pltpu.CompilerParams(
            dimension_semantics=("parallel",), vmem_limit_bytes=VMEM_LIMIT_V7X),
        name="mix",
    )(att, yg, x, wglu, bglu, ga, gs, wout)


def _ssm_matrices(lam_re, lam_im, log_dt, b_re, b_im, c_re, c_im, d_skip):
    hp = lax.Precision.HIGHEST
    n_g = lam_re.shape[0]
    dt = jnp.exp(log_dt)[:, None]
    n = jnp.arange(SSM_T + 1, dtype=F32)
    mag = jnp.exp((lam_re * dt)[..., None] * n)
    ang = (lam_im * dt)[..., None] * n
    pw_re, pw_im = mag * jnp.cos(ang), mag * jnp.sin(ang)
    x, y = pw_re[..., 1] - 1.0, pw_im[..., 1]
    den = lam_re * lam_re + lam_im * lam_im
    z_re, z_im = ((x * lam_re + y * lam_im) / den)[:, None, :], ((y * lam_re - x * lam_im) / den)[:, None, :]
    bt_re, bt_im = b_re.transpose(0, 2, 1), b_im.transpose(0, 2, 1)
    bb_re, bb_im = z_re * bt_re - z_im * bt_im, z_re * bt_im + z_im * bt_re
    ct_re, ct_im = c_re.transpose(0, 2, 1), c_im.transpose(0, 2, 1)
    per_step = lambda a: jnp.repeat(a, SSM_GROUP, axis=-1)
    per_chan = lambda a: jnp.tile(a, (1, 1, SSM_T + 1))
    cp_re = per_chan(ct_re) * per_step(pw_re) - per_chan(ct_im) * per_step(pw_im)
    cp_im = per_chan(ct_re) * per_step(pw_im) + per_chan(ct_im) * per_step(pw_re)
    cm = jnp.concatenate([cp_re[..., SSM_GROUP:], -cp_im[..., SSM_GROUP:]], axis=1)
    lagk = (jnp.einsum('gip,gpc->gic', bb_re, cp_re[..., :SSM_W], precision=hp)
            - jnp.einsum('gip,gpc->gic', bb_im, cp_im[..., :SSM_W], precision=hp))
    lagk = lagk.at[:, :, :SSM_GROUP].add(jnp.eye(SSM_GROUP, dtype=F32)[None] * d_skip[:, :, None])
    back_re = pw_re[..., SSM_T - 1::-1].transpose(0, 2, 1)[:, :, None, :]
    back_im = pw_im[..., SSM_T - 1::-1].transpose(0, 2, 1)[:, :, None, :]
    inj_re = (back_re * bb_re[:, None] - back_im * bb_im[:, None]).reshape(n_g, SSM_W, SSM_STATE)
    inj_im = (back_re * bb_im[:, None] + back_im * bb_re[:, None]).reshape(n_g, SSM_W, SSM_STATE)
    bm = jnp.concatenate([inj_re, inj_im, inj_im, inj_re], axis=-1)
    ar, ai = pw_re[..., SSM_T], pw_im[..., SSM_T]
    acoef = jnp.stack([jnp.concatenate([ar, ar], -1), jnp.concatenate([-ai, ai], -1),
                       jnp.concatenate([ai, -ai], -1)], axis=1)
    return lagk, bm.astype(BF16), cm.astype(BF16), acoef


def _bias_table(rel_bias):
    n_heads = rel_bias.shape[0]
    ext = ATT_QB + ATT_KW
    n_edge = ATT_LEFT - REL_CLIP + 1
    assert ATT_KW - n_edge == 2 * REL_CLIP - 1 and ATT_LEFT >= REL_CLIP
    far = rel_bias[:, 2 * REL_CLIP:]
    row = jnp.concatenate([jnp.broadcast_to(far, (n_heads, n_edge)), rel_bias[:, 1:2 * REL_CLIP][:, ::-1],
                           jnp.broadcast_to(far, (n_heads, ATT_QB))], axis=1)
    plain = jax.ShapeDtypeStruct((n_heads, ATT_QB, ATT_KW), F32)
    out_spec = pl.BlockSpec((None, ATT_QB, ATT_KW), lambda h: (h, 0, 0))
    return pl.pallas_call(
        _bias_table_kernel,
        grid=(n_heads,),
        in_specs=[pl.BlockSpec((None, 1, ext), lambda h: (h, 0, 0))],
        out_specs=[out_spec, out_spec],
        out_shape=[plain, plain],
        name="bias_table",
    )(row[:, None, :].astype(F32))


def _bias_table_kernel(row_ref, table_ref, banded_ref):
    rows = jnp.broadcast_to(row_ref[...], (ATT_QB, row_ref.shape[1]))
    table = pltpu.roll(rows, 0, 1, stride=1, stride_axis=0)[:, :ATT_KW]
    table_ref[...] = table
    q_chunk = lax.broadcasted_iota(jnp.int32, table.shape, 0) // CHUNK
    k_chunk = lax.broadcasted_iota(jnp.int32, table.shape, 1) // CHUNK
    band = (k_chunk >= q_chunk) & (k_chunk <= q_chunk + LEFT_CHUNKS)
    banded_ref[...] = jnp.where(band, table, NEG_INF)


def _state_rows(s_re, s_im):
    s = jnp.concatenate([s_re, s_im], -1).transpose(1, 0, 2)
    sw = jnp.concatenate([s_im, s_re], -1).transpose(1, 0, 2)
    return jnp.stack([s, sw], axis=1)


def _cast_pad_kernel(*refs, scale):
    n = len(refs) // 2
    for w_ref, o_ref in zip(refs[:n], refs[n:]):
        r, c = w_ref.shape
        w = w_ref[...]
        o_ref[:r, :c] = (w if scale == 1.0 else w * scale).astype(BF16)
        if o_ref.shape[0] > r:
            o_ref[r:, :] = jnp.zeros((o_ref.shape[0] - r, o_ref.shape[1]), BF16)
        if o_ref.shape[1] > c:
            o_ref[:, c:] = jnp.zeros((o_ref.shape[0], o_ref.shape[1] - c), BF16)


def _cast_pad(ws, *, axis, mult, block, scale=1.0):
    r, c = ws[0].shape
    padded = -(-ws[0].shape[axis] // mult) * mult
    if axis == 1:
        in_spec, out_spec = pl.BlockSpec((block, c), lambda i: (i, 0)), pl.BlockSpec((block, padded), lambda i: (i, 0))
        out_shape, steps = jax.ShapeDtypeStruct((r, padded), BF16), r // block
    else:
        in_spec, out_spec = pl.BlockSpec((r, block), lambda i: (0, i)), pl.BlockSpec((padded, block), lambda i: (0, i))
        out_shape, steps = jax.ShapeDtypeStruct((padded, c), BF16), c // block
    return pl.pallas_call(
        functools.partial(_cast_pad_kernel, scale=scale),
        grid=(steps,),
        in_specs=[in_spec] * len(ws),
        out_specs=[out_spec] * len(ws),
        out_shape=[out_shape] * len(ws),
        compiler_params=pltpu.CompilerParams(
            dimension_semantics=("parallel",), vmem_limit_bytes=VMEM_LIMIT_V7X),
        name="cast_pad",
    )(*ws)


def _side_jobs_fit(w_rows, w_cols, batch, proj_steps):
    att_steps = batch * (N_HEADS // ATT_HEADS_PER_STEP)
    return (w_rows.shape[0] % (16 * proj_steps) == 0) and (w_cols.shape[1] % (LANES * att_steps) == 0)


def _stream(x, p, *, tm, tf, tb, ssm_groups, cache=None):
    batch, seq, d = x.shape
    ffn = functools.partial(_ffn, tm=tm, tf=tf)
    x1 = ffn(x.reshape(batch * seq, d), p['g_ffn1'], p['ffn1_wg'], p['ffn1_wu'], p['ffn1_wd'], p['g_final'],
             final_norm=False).reshape(batch, seq, d)
    proj_side, att_side = p.pop('proj_side', ()), p.pop('att_side', ())
    if proj_side and (cache is not None or not _side_jobs_fit(proj_side[0][1], att_side[0][1], batch, seq // tb)):
        for name, w, scale in proj_side:
            p[name], = _cast_pad([w], axis=1, mult=FF_PAD, block=256, scale=scale)
        for name, w, scale in att_side:
            p[name], = _cast_pad([w], axis=0, mult=FF_PAD, block=256, scale=scale)
        proj_side, att_side = (), ()
    q, k, v, ug, kf, vf, *cast = _proj(x1, p['g_mix'], p['w_in'], p['q_norm'], p['k_norm'], tb=tb,
                                       side=[w for _, w, _ in proj_side])
    p.update({name: c for (name, _, _), c in zip(proj_side, cast)})
    n_g = ug.shape[0]
    if cache is None:
        att, *cast = _attn_prompt(q, k, v, p['bias_prompt'], side=[w for _, w, _ in att_side],
                                  side_scales=[scale for _, _, scale in att_side])
        p.update({name: c for (name, _, _), c in zip(att_side, cast)})
        s0 = jnp.zeros((n_g, 2, batch, 2 * SSM_STATE), F32)
    else:
        ck, cv, s_re, s_im = cache
        w_cache = ck.shape[1]
        assert w_cache == ATT_LEFT and seq <= ATT_QB
        att, kf, vf = _attn_sample(q, kf, vf, ck.reshape(batch, w_cache * N_HEADS, HEAD_DIM),
                                   cv.reshape(batch, w_cache * N_HEADS, HEAD_DIM),
                                   p['bias_table'][:, :seq, :w_cache + seq])
        s0 = _state_rows(s_re, s_im)
    yg, s_re, s_im = _ssm(ug, p['lagk'], p['bm'], p['cm'], p['acoef'], s0, rows=batch, groups=ssm_groups)
    x2 = _mix(att, yg, x1, p['w_glu'], p['b_glu'], p['g_att'], p['g_ssm'], p['w_out'], tb=tb)
    y = ffn(x2.reshape(batch * seq, d), p['g_ffn2'], p['ffn2_wg'], p['ffn2_wu'], p['ffn2_wd'], p['g_final'],
            final_norm=True).reshape(batch, seq, d)
    return y, kf, vf, s_re.reshape(batch, n_g, SSM_STATE), s_im.reshape(batch, n_g, SSM_STATE)


def kernel(x_prompt, x_sample, cache_attn_k, cache_attn_v, state_ssm_re, state_ssm_im, norm_ffn1, ffn1_w_gate, ffn1_w_up, ffn1_w_down, norm_mix, w_in, q_norm, k_norm, rel_bias, ssm_lambda_re, ssm_lambda_im, ssm_log_dt, ssm_b_re, ssm_b_im, ssm_c_re, ssm_c_im, ssm_d, w_glu, b_glu, norm_att_out, norm_ssm_out, w_out, norm_ffn2, ffn2_w_gate, ffn2_w_up, ffn2_w_down, norm_final):
    depth = norm_ffn1.shape[0]
    bs, ls, _ = x_sample.shape
    yp, ys = x_prompt, x_sample
    outs = [[] for _ in range(8)]
    for l in range(depth):
        lagk, bm, cm, acoef = _ssm_matrices(ssm_lambda_re[l], ssm_lambda_im[l], ssm_log_dt[l], ssm_b_re[l],
                                          ssm_b_im[l], ssm_c_re[l], ssm_c_im[l], ssm_d[l])
        row = lambda a: a[l][None, :]
        bias_table, bias_prompt = _bias_table(rel_bias[l])
        wg1, wu1 = _cast_pad([ffn1_w_gate[l], ffn1_w_up[l]], axis=1, mult=FF_PAD, block=256)
        wd1, = _cast_pad([ffn1_w_down[l]], axis=0, mult=FF_PAD, block=256, scale=0.5)
        p = dict(
            g_ffn1=row(norm_ffn1), g_mix=row(norm_mix), g_att=row(norm_att_out), g_ssm=row(norm_ssm_out),
            g_ffn2=row(norm_ffn2), g_final=row(norm_final), q_norm=row(q_norm), k_norm=row(k_norm),
            ffn1_wg=wg1, ffn1_wu=wu1, ffn1_wd=wd1,
            proj_side=(('ffn2_wg', ffn2_w_gate[l], 1.0), ('ffn2_wu', ffn2_w_up[l], 1.0)),
            att_side=(('ffn2_wd', ffn2_w_down[l], 0.5), ('w_glu', w_glu[l], 1.0), ('w_out', w_out[l], 1.0)),
            w_in=w_in[l].astype(BF16), b_glu=row(b_glu),
            bias_table=bias_table, bias_prompt=bias_prompt,
            lagk=lagk, bm=bm, cm=cm, acoef=acoef,
        )
        yp, kp, vp, rp, ip = _stream(yp, p, tm=1024, tf=FF_PAD, tb=CHUNK, ssm_groups=4)
        ys, kd, vd, rd, idd = _stream(ys, p, tm=bs * ls, tf=FF_PAD, tb=ls, ssm_groups=8,
                                      cache=(cache_attn_k[l], cache_attn_v[l], state_ssm_re[l], state_ssm_im[l]))
        as_cache = lambda a: a.reshape(a.shape[0], -1, N_HEADS, HEAD_DIM)
        for lst, val in zip(outs, (as_cache(kp), as_cache(vp), rp, ip, as_cache(kd), as_cache(vd), rd, idd)):
            lst.append(val)
    return (yp, ys) + tuple(jnp.stack(o) for o in outs)
```

```python
import functools

import jax
import jax.numpy as jnp
from jax import lax
from jax.experimental import pallas as pl
from jax.experimental.pallas import tpu as pltpu

EPS = 1e-6
NEG_INF = -1e30
CHUNK = 64
LEFT_CHUNKS = 8
ATT_LEFT = LEFT_CHUNKS * CHUNK
REL_CLIP = 256
N_HEADS = 8
HEAD_DIM = 128
SSM_GROUP = 16
SSM_STATE = 64
LANES = 128
SSM_T = 16
SSM_W = SSM_T * SSM_GROUP
GROUPS_PER_TILE = LANES // SSM_GROUP
ATT_QB = 256
ATT_KW = ATT_QB + ATT_LEFT
ATT_HEADS_PER_STEP = 4
REGROUP_PAD = 8
FF_PAD = 512
VMEM_LIMIT_V7X = 60 * 1024 * 1024

BF16 = jnp.bfloat16
F32 = jnp.float32


def _dot(a, b):
    return jnp.dot(a, b, preferred_element_type=F32)


def _rms(x, g):
    return x * lax.rsqrt(jnp.mean(x * x, axis=-1, keepdims=True) + EPS) * g


def _ffn_kernel(x_ref, g_ref, wg_hbm, wu_hbm, wd_hbm, gf_ref, o_ref, xn_ref, wg_buf, wu_buf, wd_buf, sem,
                *, tf, n_chunks, final_norm):
    i = pl.program_id(0)
    first_slot = (i * n_chunks) % 2

    def chunk_copies(j, slot):
        cols = pl.ds(pl.multiple_of(j * tf, tf), tf)
        return (pltpu.make_async_copy(wg_hbm.at[:, cols], wg_buf.at[slot], sem.at[0, slot]),
                pltpu.make_async_copy(wu_hbm.at[:, cols], wu_buf.at[slot], sem.at[1, slot]),
                pltpu.make_async_copy(wd_hbm.at[cols, :], wd_buf.at[slot], sem.at[2, slot]))

    @pl.when(i == 0)
    def _():
        for c in chunk_copies(0, 0):
            c.start()

    x = x_ref[...]
    xn_ref[...] = _rms(x, g_ref[...]).astype(BF16)
    o_ref[...] = x

    def body(j, carry):
        slot = (first_slot + j) % 2
        nxt = jnp.where(j + 1 == n_chunks, 0, j + 1)
        for c in chunk_copies(nxt, 1 - slot):
            c.start()
        for c in chunk_copies(j, slot):
            c.wait()
        xn = xn_ref[...]
        a = _dot(xn, wg_buf[slot])
        b = _dot(xn, wu_buf[slot])
        h = (a * jax.nn.sigmoid(a) * b).astype(BF16)
        o_ref[...] += _dot(h, wd_buf[slot])
        return carry

    lax.fori_loop(0, n_chunks, body, 0)

    @pl.when(i == pl.num_programs(0) - 1)
    def _():
        for c in chunk_copies(0, (first_slot + n_chunks) % 2):
            c.wait()

    if final_norm:
        o_ref[...] = _rms(o_ref[...], gf_ref[...])


def _ffn(x, g, wg, wu, wd_half, gf, *, tm, tf, final_norm):
    n, d = x.shape
    fpad = wg.shape[1]
    hbm = pl.BlockSpec(memory_space=pl.ANY)
    return pl.pallas_call(
        functools.partial(_ffn_kernel, tf=tf, n_chunks=fpad // tf, final_norm=final_norm),
        grid=(n // tm,),
        in_specs=[
            pl.BlockSpec((tm, d), lambda i: (i, 0)),
            pl.BlockSpec((1, d), lambda i: (0, 0)),
            hbm, hbm, hbm,
            pl.BlockSpec((1, d), lambda i: (0, 0)),
        ],
        out_specs=pl.BlockSpec((tm, d), lambda i: (i, 0)),
        out_shape=jax.ShapeDtypeStruct((n, d), F32),
        scratch_shapes=[pltpu.VMEM((tm, d), BF16),
                        pltpu.VMEM((2, d, tf), BF16), pltpu.VMEM((2, d, tf), BF16), pltpu.VMEM((2, tf, d), BF16),
                        pltpu.SemaphoreType.DMA((3, 2))],
        compiler_params=pltpu.CompilerParams(
            dimension_semantics=("arbitrary",), vmem_limit_bytes=VMEM_LIMIT_V7X),
        name="ffn",
    )(x, g, wg, wu, wd_half, gf)


def _granule_transpose(vs):
    gran = lax.broadcasted_iota(jnp.int32, vs[0].shape, 1) // SSM_GROUP
    for s in (4, 2, 1):
        upper = (gran & s) != 0
        new = list(vs)
        for a in range(GROUPS_PER_TILE):
            if a & s == 0:
                lo, hi = vs[a], vs[a + s]
                new[a] = jnp.where(upper, pltpu.roll(hi, s * SSM_GROUP, 1), lo)
                new[a + s] = jnp.where(upper, hi, pltpu.roll(lo, LANES - s * SSM_GROUP, 1))
        vs = new
    return vs


def _regroup_geometry(nb, tb):
    return tb + REGROUP_PAD, tb // SSM_T


def _tokens_to_groups(slab_ref, ug_ref, *, nb, tb):
    pitch, ncl = _regroup_geometry(nb, tb)
    for j in range(slab_ref.shape[0]):
        xs = [jnp.concatenate([slab_ref[j, pl.ds(c * SSM_T + t, nb, stride=pitch), :] for c in range(ncl)], axis=0)
              for t in range(SSM_T)]
        lo = _granule_transpose(xs[:GROUPS_PER_TILE])
        hi = _granule_transpose(xs[GROUPS_PER_TILE:])
        for gl in range(GROUPS_PER_TILE):
            ug_ref[j * GROUPS_PER_TILE + gl] = jnp.concatenate([lo[gl], hi[gl]], axis=1).astype(ug_ref.dtype)


def _groups_to_tokens(yg_ref, slab_ref, *, nb, tb):
    pitch, ncl = _regroup_geometry(nb, tb)
    for j in range(slab_ref.shape[0]):
        ds = [yg_ref[j * GROUPS_PER_TILE + gl].astype(F32) for gl in range(GROUPS_PER_TILE)]
        lo = _granule_transpose([d[:, :LANES] for d in ds])
        hi = _granule_transpose([d[:, LANES:] for d in ds])
        for t, x in enumerate(lo + hi):
            for c in range(ncl):
                slab_ref[j, pl.ds(c * SSM_T + t, nb, stride=pitch), :] = x[c * nb:(c + 1) * nb]


def _proj_kernel(*refs, first_tail_block, n_side):
    x_ref, g_ref, w_ref, qn_ref, kn_ref = refs[:5]
    q_ref, k_ref, v_ref, ug_ref, kf_ref, vf_ref = refs[5 + n_side:11 + n_side]
    slab_ref = refs[-1]
    if n_side:
        _cast_pad_kernel(*refs[5:5 + n_side], *refs[11 + n_side:11 + 2 * n_side], scale=1.0)
    i = pl.program_id(0)
    nb, tb, d = x_ref.shape
    d_att = N_HEADS * HEAD_DIM
    pitch, _ = _regroup_geometry(nb, tb)
    h = _rms(x_ref[...].reshape(nb * tb, d), g_ref[...]).astype(BF16)
    u = _dot(h, w_ref[:, 3 * d_att:])
    for j in range(slab_ref.shape[0]):
        for b in range(nb):
            slab_ref[j, b * pitch:b * pitch + tb, :] = u[b * tb:(b + 1) * tb, j * LANES:(j + 1) * LANES]
    _tokens_to_groups(slab_ref, ug_ref, nb=nb, tb=tb)

    def head_norm(y, gain):
        return [_rms(y[:, hd * HEAD_DIM:(hd + 1) * HEAD_DIM], gain) for hd in range(N_HEADS)]

    def store_tokens(ref, y):
        for b in range(nb):
            ref[b] = y[b * tb:(b + 1) * tb].astype(ref.dtype)

    qs = head_norm(_dot(h, w_ref[:, 0:d_att]), qn_ref[...] * (HEAD_DIM ** -0.5))
    store_tokens(q_ref, jnp.concatenate(qs, axis=1))
    ks = head_norm(_dot(h, w_ref[:, d_att:2 * d_att]), kn_ref[...])
    store_tokens(k_ref, jnp.concatenate(ks, axis=1))
    v = _dot(h, w_ref[:, 2 * d_att:3 * d_att])
    store_tokens(v_ref, v)

    @pl.when(i >= first_tail_block)
    def _():
        for b in range(nb):
            rows = slice(b * tb, (b + 1) * tb)
            for hd in range(N_HEADS):
                head_rows = pl.ds(hd, tb, stride=N_HEADS)
                kf_ref[b, head_rows, :] = ks[hd][rows]
                vf_ref[b, head_rows, :] = v[rows, hd * HEAD_DIM:(hd + 1) * HEAD_DIM]


def _proj(x, g, w, qn, kn, *, tb, side=()):
    nb, seq, d = x.shape
    d_att = N_HEADS * HEAD_DIM
    d_ssm = w.shape[1] - 3 * d_att
    n_g = d_ssm // SSM_GROUP
    n_blocks = seq // tb
    tail = min(ATT_LEFT, seq)
    first_tail_block = n_blocks - tail // tb
    pitch, ncl = _regroup_geometry(nb, tb)
    tok = lambda i: (0, i, 0)
    const = lambda i: (0, 0)
    tail_map = lambda i: (0, jnp.maximum(i - first_tail_block, 0), 0)
    side_in, side_out, side_shape = [], [], []
    for a in side:
        rows, cols = a.shape[0] // n_blocks, -(-a.shape[1] // FF_PAD) * FF_PAD
        assert rows * n_blocks == a.shape[0] and rows % 16 == 0
        side_in.append(pl.BlockSpec((rows, a.shape[1]), lambda i: (i, 0)))
        side_out.append(pl.BlockSpec((rows, cols), lambda i: (i, 0)))
        side_shape.append(jax.ShapeDtypeStruct((a.shape[0], cols), BF16))
    return pl.pallas_call(
        functools.partial(_proj_kernel, first_tail_block=first_tail_block, n_side=len(side)),
        grid=(n_blocks,),
        in_specs=[
            pl.BlockSpec((nb, tb, d), tok),
            pl.BlockSpec((1, d), const),
            pl.BlockSpec(w.shape, const, pipeline_mode=pl.Buffered(1)),
            pl.BlockSpec((1, HEAD_DIM), const),
            pl.BlockSpec((1, HEAD_DIM), const),
        ] + side_in,
        out_specs=[
            pl.BlockSpec((nb, tb, d_att), tok),
            pl.BlockSpec((nb, tb, d_att), tok),
            pl.BlockSpec((nb, tb, d_att), tok),
            pl.BlockSpec((n_g, ncl * nb, SSM_W), tok),
            pl.BlockSpec((nb, tb * N_HEADS, HEAD_DIM), tail_map),
            pl.BlockSpec((nb, tb * N_HEADS, HEAD_DIM), tail_map),
        ] + side_out,
        out_shape=[
            jax.ShapeDtypeStruct((nb, seq, d_att), BF16),
            jax.ShapeDtypeStruct((nb, seq, d_att), BF16),
            jax.ShapeDtypeStruct((nb, seq, d_att), BF16),
            jax.ShapeDtypeStruct((n_g, (seq // SSM_T) * nb, SSM_W), BF16),
            jax.ShapeDtypeStruct((nb, tail * N_HEADS, HEAD_DIM), F32),
            jax.ShapeDtypeStruct((nb, tail * N_HEADS, HEAD_DIM), F32),
        ] + side_shape,
        scratch_shapes=[pltpu.VMEM((d_ssm // LANES, nb * pitch, LANES), F32)],
        compiler_params=pltpu.CompilerParams(
            dimension_semantics=("arbitrary",), vmem_limit_bytes=VMEM_LIMIT_V7X),
        name="proj",
    )(x, g, w, qn, kn, *side)


def _softmax_pv(s, v):
    m = jnp.max(s, axis=-1, keepdims=True)
    p = jnp.exp(s - m)
    l = jnp.sum(p, axis=-1, keepdims=True)
    return _dot(p.astype(BF16), v) / l


def _qk(q, k):
    return lax.dot_general(q, k, (((1,), (1,)), ((), ())), preferred_element_type=F32)


def _attn_prompt_kernel(*refs, n_blocks, side_scales):
    n_side = len(side_scales)
    q_ref, k_ref, v_ref, bias_ref = refs[:4]
    o_ref = refs[4 + n_side]
    s_ref, p_ref, linv_ref = refs[-3:]
    for w_ref, wb_ref, scale in zip(refs[4:4 + n_side], refs[5 + n_side:5 + 2 * n_side], side_scales):
        _cast_pad_kernel(w_ref, wb_ref, scale=scale)
    lead = ATT_LEFT // ATT_QB
    n_full = n_blocks - lead

    def lanes(h):
        return slice(h * HEAD_DIM, (h + 1) * HEAD_DIM)

    def start(i):
        return pl.multiple_of(i * ATT_QB, ATT_QB)

    for i in range(min(lead, n_blocks)):
        kw = (i + 1) * ATT_QB
        rows = slice(i * ATT_QB, (i + 1) * ATT_QB)
        ss = [_qk(q_ref[rows, lanes(h)], k_ref[0:kw, lanes(h)]) for h in range(ATT_HEADS_PER_STEP)]
        for h in range(ATT_HEADS_PER_STEP):
            s = ss[h] + bias_ref[h, :, ATT_KW - kw:]
            o_ref[rows, lanes(h)] = _softmax_pv(s, v_ref[0:kw, lanes(h)]).astype(BF16)
    if n_full <= 0:
        return

    def scores(h, i, slot):
        s_ref[slot] = _qk(q_ref[pl.ds(start(i), ATT_QB), lanes(h)],
                          k_ref[pl.ds(start(i) - ATT_LEFT, ATT_KW), lanes(h)])

    def softmax(h, slot):
        win = ATT_KW - LANES
        for c in range(ATT_QB // CHUNK):
            rows = slice(c * CHUNK, (c + 1) * CHUNK)
            c0 = (c * CHUNK) // LANES * LANES
            assert c0 + win <= ATT_KW and c0 <= c * CHUNK and (c + LEFT_CHUNKS + 1) * CHUNK <= c0 + win
            s = s_ref[slot, rows, c0:c0 + win] + bias_ref[h, rows, c0:c0 + win]
            p = jnp.exp(s - jnp.max(s, axis=-1, keepdims=True))
            p_ref[slot, rows, c0:c0 + win] = p.astype(BF16)
            dead = slice(win, ATT_KW) if c0 == 0 else slice(0, c0)
            p_ref[slot, rows, dead] = jnp.zeros((CHUNK, LANES), BF16)
            linv_ref[slot, rows] = jnp.broadcast_to(1.0 / jnp.sum(p, axis=-1, keepdims=True), (CHUNK, HEAD_DIM))

    def output(h, i, slot):
        v = v_ref[pl.ds(start(i) - ATT_LEFT, ATT_KW), lanes(h)]
        o_ref[pl.ds(start(i), ATT_QB), lanes(h)] = (_dot(p_ref[slot], v) * linv_ref[slot]).astype(BF16)

    for h0 in range(0, ATT_HEADS_PER_STEP, 2):
        h1 = h0 + 1
        scores(h0, lead, 0)
        scores(h1, lead, 1)
        softmax(h0, 0)

        def body(r, carry):
            i = lead + r
            scores(h0, i + 1, 0)
            output(h0, i, 0)
            softmax(h1, 1)
            scores(h1, i + 1, 1)
            output(h1, i, 1)
            softmax(h0, 0)
            return carry

        lax.fori_loop(0, n_full - 1, body, 0)
        output(h0, n_blocks - 1, 0)
        softmax(h1, 1)
        output(h1, n_blocks - 1, 1)


def _attn_prompt(q, k, v, bias, *, side=(), side_scales=()):
    batch, seq, d_att = q.shape
    hs = ATT_HEADS_PER_STEP
    n_hg = N_HEADS // hs
    blk = pl.BlockSpec((None, seq, hs * HEAD_DIM), lambda b, h: (b, 0, h))
    side_in, side_out, side_shape = [], [], []
    for a in side:
        cols, rows = a.shape[1] // (batch * n_hg), -(-a.shape[0] // FF_PAD) * FF_PAD
        assert cols * batch * n_hg == a.shape[1] and cols % LANES == 0
        side_in.append(pl.BlockSpec((a.shape[0], cols), lambda b, h: (0, b * n_hg + h)))
        side_out.append(pl.BlockSpec((rows, cols), lambda b, h: (0, b * n_hg + h)))
        side_shape.append(jax.ShapeDtypeStruct((rows, a.shape[1]), BF16))
    return pl.pallas_call(
        functools.partial(_attn_prompt_kernel, n_blocks=seq // ATT_QB, side_scales=tuple(side_scales)),
        grid=(batch, n_hg),
        in_specs=[blk, blk, blk, pl.BlockSpec((hs, ATT_QB, ATT_KW), lambda b, h: (h, 0, 0))] + side_in,
        out_specs=[blk] + side_out,
        out_shape=[jax.ShapeDtypeStruct((batch, seq, d_att), BF16)] + side_shape,
        scratch_shapes=[pltpu.VMEM((2, ATT_QB, ATT_KW), F32), pltpu.VMEM((2, ATT_QB, ATT_KW), BF16),
                        pltpu.VMEM((2, ATT_QB, HEAD_DIM), F32)],
        compiler_params=pltpu.CompilerParams(
            dimension_semantics=("parallel", "parallel"), vmem_limit_bytes=VMEM_LIMIT_V7X),
        name="attn_prompt",
    )(q, k, v, bias, *side)


def _attn_sample_kernel(q_ref, kn_ref, vn_ref, kc_ref, vc_ref, bias_ref, o_ref, ko_ref, vo_ref, *, w_cache, seq):
    def head_rows(ref, n, hd):
        return ref[pl.ds(hd, n, stride=N_HEADS), :].astype(BF16)

    for hd in range(N_HEADS):
        sl = slice(hd * HEAD_DIM, (hd + 1) * HEAD_DIM)
        q = q_ref[:, sl]
        s1 = _qk(q, head_rows(kc_ref, w_cache, hd)) + bias_ref[hd, :, :w_cache]
        s2 = _qk(q, head_rows(kn_ref, seq, hd)) + bias_ref[hd, :, w_cache:]
        m = jnp.maximum(jnp.max(s1, axis=-1, keepdims=True), jnp.max(s2, axis=-1, keepdims=True))
        p1 = jnp.exp(s1 - m)
        p2 = jnp.exp(s2 - m)
        l = jnp.sum(p1, axis=-1, keepdims=True) + jnp.sum(p2, axis=-1, keepdims=True)
        o = (_dot(p1.astype(BF16), head_rows(vc_ref, w_cache, hd))
             + _dot(p2.astype(BF16), head_rows(vn_ref, seq, hd)))
        o_ref[:, sl] = (o / l).astype(BF16)

    keep = (w_cache - seq) * N_HEADS
    for new_ref, old_ref, out_ref in ((kn_ref, kc_ref, ko_ref), (vn_ref, vc_ref, vo_ref)):
        out_ref[:keep, :] = old_ref[seq * N_HEADS:, :]
        out_ref[keep:, :] = new_ref[...]


def _attn_sample(q, kn, vn, kc, vc, bias):
    batch, seq, d_att = q.shape
    w_cache = kc.shape[1] // N_HEADS
    assert seq <= w_cache
    tok = pl.BlockSpec((None, seq, d_att), lambda b: (b, 0, 0))
    new = pl.BlockSpec((None, seq * N_HEADS, HEAD_DIM), lambda b: (b, 0, 0))
    cache = pl.BlockSpec((None, w_cache * N_HEADS, HEAD_DIM), lambda b: (b, 0, 0))
    return pl.pallas_call(
        functools.partial(_attn_sample_kernel, w_cache=w_cache, seq=seq),
        grid=(batch,),
        in_specs=[tok, new, new, cache, cache, pl.BlockSpec(bias.shape, lambda b: (0, 0, 0))],
        out_specs=[tok, cache, cache],
        out_shape=[jax.ShapeDtypeStruct((batch, seq, d_att), BF16),
                   jax.ShapeDtypeStruct(kc.shape, F32), jax.ShapeDtypeStruct(vc.shape, F32)],
        compiler_params=pltpu.CompilerParams(
            dimension_semantics=("parallel",), vmem_limit_bytes=VMEM_LIMIT_V7X),
        name="attn_sample",
    )(q, kn, vn, kc, vc, bias)


def _ssm_kernel(u_ref, lag_ref, bm_ref, cm_ref, a_ref, s0_ref, y_ref, sre_ref, sim_ref, sl_ref, sp_ref, km_ref,
                *, rows, n_chunks, groups):
    half = 2 * SSM_STATE
    lane = lax.broadcasted_iota(jnp.int32, (SSM_GROUP, SSM_W), 1)
    for gi in range(groups):
        lag = lag_ref[gi]
        for t_in in range(SSM_T):
            shifted = lag if t_in == 0 else pltpu.roll(lag, t_in * SSM_GROUP, 1)
            km_ref[gi, t_in * SSM_GROUP:(t_in + 1) * SSM_GROUP, :] = jnp.where(
                lane >= t_in * SSM_GROUP, shifted, 0.0).astype(BF16)
        sl_ref[gi] = _dot(u_ref[gi], bm_ref[gi])

    coef = []
    for gi in range(groups):
        coef.append(tuple(jnp.broadcast_to(a_ref[gi, r:r + 1, :], (rows, half)) for r in range(3)))

    def body(c, carry):
        r0 = pl.multiple_of(c * rows, rows)
        new = []
        for gi in range(groups):
            s, sw = carry[gi]
            a1, a2, a2w = coef[gi]
            sp_ref[gi, pl.ds(r0, rows), :] = s
            loc = sl_ref[gi, pl.ds(r0, rows), :]
            new.append((a1 * s + a2 * sw + loc[:, :half], a1 * sw + a2w * s + loc[:, half:]))
        return tuple(new)

    init = tuple((s0_ref[gi, 0], s0_ref[gi, 1]) for gi in range(groups))
    last = lax.fori_loop(0, n_chunks, body, init)
    for gi in range(groups):
        sre_ref[:, gi * SSM_STATE:(gi + 1) * SSM_STATE] = last[gi][0][:, :SSM_STATE]
        sim_ref[:, gi * SSM_STATE:(gi + 1) * SSM_STATE] = last[gi][0][:, SSM_STATE:]
        y = _dot(u_ref[gi], km_ref[gi]) + _dot(sp_ref[gi].astype(BF16), cm_ref[gi])
        y_ref[gi] = jax.nn.gelu(y).astype(BF16)


def _ssm(ug, lagk, bm, cm, acoef, s0, *, rows, groups):
    n_g, n_rows, _ = ug.shape
    n_chunks = n_rows // rows
    half = 2 * SSM_STATE
    g3 = lambda g: (g, 0, 0)
    return pl.pallas_call(
        functools.partial(_ssm_kernel, rows=rows, n_chunks=n_chunks, groups=groups),
        grid=(n_g // groups,),
        in_specs=[
            pl.BlockSpec((groups, n_rows, SSM_W), g3),
            pl.BlockSpec((groups, SSM_GROUP, SSM_W), g3),
            pl.BlockSpec((groups, SSM_W, 2 * half), g3),
            pl.BlockSpec((groups, half, SSM_W), g3),
            pl.BlockSpec((groups, 3, half), g3),
            pl.BlockSpec((groups, 2, rows, half), lambda g: (g, 0, 0, 0)),
        ],
        out_specs=[
            pl.BlockSpec((groups, n_rows, SSM_W), g3),
            pl.BlockSpec((rows, groups * SSM_STATE), lambda g: (0, g)),
            pl.BlockSpec((rows, groups * SSM_STATE), lambda g: (0, g)),
        ],
        out_shape=[
            jax.ShapeDtypeStruct((n_g, n_rows, SSM_W), BF16),
            jax.ShapeDtypeStruct((rows, n_g * SSM_STATE), F32),
            jax.ShapeDtypeStruct((rows, n_g * SSM_STATE), F32),
        ],
        scratch_shapes=[
            pltpu.VMEM((groups, n_rows, 2 * half), F32),
            pltpu.VMEM((groups, n_rows, half), F32),
            pltpu.VMEM((groups, SSM_W, SSM_W), BF16),
        ],
        compiler_params=pltpu.CompilerParams(
            dimension_semantics=("parallel",), vmem_limit_bytes=VMEM_LIMIT_V7X),
        name="ssm",
    )(ug, lagk, bm, cm, acoef, s0)


def _mix_kernel(att_ref, yg_ref, x_ref, wglu_ref, bglu_ref, ga_ref, gs_ref, wout_ref, o_ref, slab_ref, y_ref):
    nb, tb, d = x_ref.shape
    d_att = att_ref.shape[2]
    d_ssm = y_ref.shape[1]
    pitch, _ = _regroup_geometry(nb, tb)
    att = att_ref[...].reshape(nb * tb, d_att).astype(F32)
    mix_a = _rms(att, ga_ref[...]).astype(BF16)
    o = x_ref[...].reshape(nb * tb, d) + _dot(mix_a, wout_ref[:d_att, :])
    _groups_to_tokens(yg_ref, slab_ref, nb=nb, tb=tb)
    for j in range(slab_ref.shape[0]):
        for b in range(nb):
            y_ref[b * tb:(b + 1) * tb, j * LANES:(j + 1) * LANES] = slab_ref[j, b * pitch:b * pitch + tb, :].astype(BF16)
    glu = _dot(y_ref[...], wglu_ref[...]) + bglu_ref[...]
    ssm_out = glu[:, :d_ssm] * jax.nn.sigmoid(glu[:, d_ssm:])
    mix_s = _rms(ssm_out, gs_ref[...]).astype(BF16)
    o = o + _dot(mix_s, wout_ref[d_att:, :])
    o_ref[...] = o.reshape(nb, tb, d)


def _mix(att, yg, x, wglu, bglu, ga, gs, wout, *, tb):
    nb, seq, d = x.shape
    d_att = att.shape[2]
    n_g = yg.shape[0]
    d_ssm = n_g * SSM_GROUP
    pitch, ncl = _regroup_geometry(nb, tb)
    tok = lambda i: (0, i, 0)
    const = lambda i: (0, 0)
    once = lambda a: pl.BlockSpec(a.shape, const, pipeline_mode=pl.Buffered(1))
    return pl.pallas_call(
        _mix_kernel,
        grid=(seq // tb,),
        in_specs=[
            pl.BlockSpec((nb, tb, d_att), tok),
            pl.BlockSpec((n_g, ncl * nb, SSM_W), tok),
            pl.BlockSpec((nb, tb, d), tok),
            once(wglu), once(bglu), once(ga), once(gs), once(wout),
        ],
        out_specs=pl.BlockSpec((nb, tb, d), tok),
        out_shape=jax.ShapeDtypeStruct((nb, seq, d), F32),
        scratch_shapes=[pltpu.VMEM((d_ssm // LANES, nb * pitch, LANES), F32),
                        pltpu.VMEM((nb * tb, d_ssm), BF16)],
        compiler_params=pltpu.CompilerParams(
            dimension_semantics=("parallel",), vmem_limit_bytes=VMEM_LIMIT_V7X),
        name="mix",
    )(att, yg, x, wglu, bglu, ga, gs, wout)


def _ssm_matrices(lam_re, lam_im, log_dt, b_re, b_im, c_re, c_im, d_skip):
    hp = lax.Precision.HIGHEST
    n_g = lam_re.shape[0]
    dt = jnp.exp(log_dt)[:, None]
    n = jnp.arange(SSM_T + 1, dtype=F32)
    mag = jnp.exp((lam_re * dt)[..., None] * n)
    ang = (lam_im * dt)[..., None] * n
    pw_re, pw_im = mag * jnp.cos(ang), mag * jnp.sin(ang)
    x, y = pw_re[..., 1] - 1.0, pw_im[..., 1]
    den = lam_re * lam_re + lam_im * lam_im
    z_re, z_im = ((x * lam_re + y * lam_im) / den)[:, None, :], ((y * lam_re - x * lam_im) / den)[:, None, :]
    bt_re, bt_im = b_re.transpose(0, 2, 1), b_im.transpose(0, 2, 1)
    bb_re, bb_im = z_re * bt_re - z_im * bt_im, z_re * bt_im + z_im * bt_re
    ct_re, ct_im = c_re.transpose(0, 2, 1), c_im.transpose(0, 2, 1)
    per_step = lambda a: jnp.repeat(a, SSM_GROUP, axis=-1)
    per_chan = lambda a: jnp.tile(a, (1, 1, SSM_T + 1))
    cp_re = per_chan(ct_re) * per_step(pw_re) - per_chan(ct_im) * per_step(pw_im)
    cp_im = per_chan(ct_re) * per_step(pw_im) + per_chan(ct_im) * per_step(pw_re)
    cm = jnp.concatenate([cp_re[..., SSM_GROUP:], -cp_im[..., SSM_GROUP:]], axis=1)
    lagk = (jnp.einsum('gip,gpc->gic', bb_re, cp_re[..., :SSM_W], precision=hp)
            - jnp.einsum('gip,gpc->gic', bb_im, cp_im[..., :SSM_W], precision=hp))
    lagk = lagk.at[:, :, :SSM_GROUP].add(jnp.eye(SSM_GROUP, dtype=F32)[None] * d_skip[:, :, None])
    back_re = pw_re[..., SSM_T - 1::-1].transpose(0, 2, 1)[:, :, None, :]
    back_im = pw_im[..., SSM_T - 1::-1].transpose(0, 2, 1)[:, :, None, :]
    inj_re = (back_re * bb_re[:, None] - back_im * bb_im[:, None]).reshape(n_g, SSM_W, SSM_STATE)
    inj_im = (back_re * bb_im[:, None] + back_im * bb_re[:, None]).reshape(n_g, SSM_W, SSM_STATE)
    bm = jnp.concatenate([inj_re, inj_im, inj_im, inj_re], axis=-1)
    ar, ai = pw_re[..., SSM_T], pw_im[..., SSM_T]
    acoef = jnp.stack([jnp.concatenate([ar, ar], -1), jnp.concatenate([-ai, ai], -1),
                       jnp.concatenate([ai, -ai], -1)], axis=1)
    return lagk, bm.astype(BF16), cm.astype(BF16), acoef


def _bias_table(rel_bias):
    n_heads = rel_bias.shape[0]
    ext = ATT_QB + ATT_KW
    n_edge = ATT_LEFT - REL_CLIP + 1
    assert ATT_KW - n_edge == 2 * REL_CLIP - 1 and ATT_LEFT >= REL_CLIP
    far = rel_bias[:, 2 * REL_CLIP:]
    row = jnp.concatenate([jnp.broadcast_to(far, (n_heads, n_edge)), rel_bias[:, 1:2 * REL_CLIP][:, ::-1],
                           jnp.broadcast_to(far, (n_heads, ATT_QB))], axis=1)
    plain = jax.ShapeDtypeStruct((n_heads, ATT_QB, ATT_KW), F32)
    out_spec = pl.BlockSpec((None, ATT_QB, ATT_KW), lambda h: (h, 0, 0))
    return pl.pallas_call(
        _bias_table_kernel,
        grid=(n_heads,),
        in_specs=[pl.BlockSpec((None, 1, ext), lambda h: (h, 0, 0))],
        out_specs=[out_spec, out_spec],
        out_shape=[plain, plain],
        name="bias_table",
    )(row[:, None, :].astype(F32))


def _bias_table_kernel(row_ref, table_ref, banded_ref):
    rows = jnp.broadcast_to(row_ref[...], (ATT_QB, row_ref.shape[1]))
    table = pltpu.roll(rows, 0, 1, stride=1, stride_axis=0)[:, :ATT_KW]
    table_ref[...] = table
    q_chunk = lax.broadcasted_iota(jnp.int32, table.shape, 0) // CHUNK
    k_chunk = lax.broadcasted_iota(jnp.int32, table.shape, 1) // CHUNK
    band = (k_chunk >= q_chunk) & (k_chunk <= q_chunk + LEFT_CHUNKS)
    banded_ref[...] = jnp.where(band, table, NEG_INF)


def _state_rows(s_re, s_im):
    s = jnp.concatenate([s_re, s_im], -1).transpose(1, 0, 2)
    sw = jnp.concatenate([s_im, s_re], -1).transpose(1, 0, 2)
    return jnp.stack([s, sw], axis=1)


def _cast_pad_kernel(*refs, scale):
    n = len(refs) // 2
    for w_ref, o_ref in zip(refs[:n], refs[n:]):
        r, c = w_ref.shape
        w = w_ref[...]
        o_ref[:r, :c] = (w if scale == 1.0 else w * scale).astype(BF16)
        if o_ref.shape[0] > r:
            o_ref[r:, :] = jnp.zeros((o_ref.shape[0] - r, o_ref.shape[1]), BF16)
        if o_ref.shape[1] > c:
            o_ref[:, c:] = jnp.zeros((o_ref.shape[0], o_ref.shape[1] - c), BF16)


def _cast_pad(ws, *, axis, mult, block, scale=1.0):
    r, c = ws[0].shape
    padded = -(-ws[0].shape[axis] // mult) * mult
    if axis == 1:
        in_spec, out_spec = pl.BlockSpec((block, c), lambda i: (i, 0)), pl.BlockSpec((block, padded), lambda i: (i, 0))
        out_shape, steps = jax.ShapeDtypeStruct((r, padded), BF16), r // block
    else:
        in_spec, out_spec = pl.BlockSpec((r, block), lambda i: (0, i)), pl.BlockSpec((padded, block), lambda i: (0, i))
        out_shape, steps = jax.ShapeDtypeStruct((padded, c), BF16), c // block
    return pl.pallas_call(
        functools.partial(_cast_pad_kernel, scale=scale),
        grid=(steps,),
        in_specs=[in_spec] * len(ws),
        out_specs=[out_spec] * len(ws),
        out_shape=[out_shape] * len(ws),
        compiler_params=pltpu.CompilerParams(
            dimension_semantics=("parallel",), vmem_limit_bytes=VMEM_LIMIT_V7X),
        name="cast_pad",
    )(*ws)


def _side_jobs_fit(w_rows, w_cols, batch, proj_steps):
    att_steps = batch * (N_HEADS // ATT_HEADS_PER_STEP)
    return (w_rows.shape[0] % (16 * proj_steps) == 0) and (w_cols.shape[1] % (LANES * att_steps) == 0)


def _stream(x, p, *, tm, tf, tb, ssm_groups, cache=None):
    batch, seq, d = x.shape
    ffn = functools.partial(_ffn, tm=tm, tf=tf)
    x1 = ffn(x.reshape(batch * seq, d), p['g_ffn1'], p['ffn1_wg'], p['ffn1_wu'], p['ffn1_wd'], p['g_final'],
             final_norm=False).reshape(batch, seq, d)
    proj_side, att_side = p.pop('proj_side', ()), p.pop('att_side', ())
    if proj_side and (cache is not None or not _side_jobs_fit(proj_side[0][1], att_side[0][1], batch, seq // tb)):
        for name, w, scale in proj_side:
            p[name], = _cast_pad([w], axis=1, mult=FF_PAD, block=256, scale=scale)
        for name, w, scale in att_side:
            p[name], = _cast_pad([w], axis=0, mult=FF_PAD, block=256, scale=scale)
        proj_side, att_side = (), ()
    q, k, v, ug, kf, vf, *cast = _proj(x1, p['g_mix'], p['w_in'], p['q_norm'], p['k_norm'], tb=tb,
                                       side=[w for _, w, _ in proj_side])
    p.update({name: c for (name, _, _), c in zip(proj_side, cast)})
    n_g = ug.shape[0]
    if cache is None:
        att, *cast = _attn_prompt(q, k, v, p['bias_prompt'], side=[w for _, w, _ in att_side],
                                  side_scales=[scale for _, _, scale in att_side])
        p.update({name: c for (name, _, _), c in zip(att_side, cast)})
        s0 = jnp.zeros((n_g, 2, batch, 2 * SSM_STATE), F32)
    else:
        ck, cv, s_re, s_im = cache
        w_cache = ck.shape[1]
        assert w_cache == ATT_LEFT and seq <= ATT_QB
        att, kf, vf = _attn_sample(q, kf, vf, ck.reshape(batch, w_cache * N_HEADS, HEAD_DIM),
                                   cv.reshape(batch, w_cache * N_HEADS, HEAD_DIM),
                                   p['bias_table'][:, :seq, :w_cache + seq])
        s0 = _state_rows(s_re, s_im)
    yg, s_re, s_im = _ssm(ug, p['lagk'], p['bm'], p['cm'], p['acoef'], s0, rows=batch, groups=ssm_groups)
    x2 = _mix(att, yg, x1, p['w_glu'], p['b_glu'], p['g_att'], p['g_ssm'], p['w_out'], tb=tb)
    y = ffn(x2.reshape(batch * seq, d), p['g_ffn2'], p['ffn2_wg'], p['ffn2_wu'], p['ffn2_wd'], p['g_final'],
            final_norm=True).reshape(batch, seq, d)
    return y, kf, vf, s_re.reshape(batch, n_g, SSM_STATE), s_im.reshape(batch, n_g, SSM_STATE)


def kernel(x_prompt, x_sample, cache_attn_k, cache_attn_v, state_ssm_re, state_ssm_im, norm_ffn1, ffn1_w_gate, ffn1_w_up, ffn1_w_down, norm_mix, w_in, q_norm, k_norm, rel_bias, ssm_lambda_re, ssm_lambda_im, ssm_log_dt, ssm_b_re, ssm_b_im, ssm_c_re, ssm_c_im, ssm_d, w_glu, b_glu, norm_att_out, norm_ssm_out, w_out, norm_ffn2, ffn2_w_gate, ffn2_w_up, ffn2_w_down, norm_final):
    depth = norm_ffn1.shape[0]
    bs, ls, _ = x_sample.shape
    yp, ys = x_prompt, x_sample
    outs = [[] for _ in range(8)]
    for l in range(depth):
        lagk, bm, cm, acoef = _ssm_matrices(ssm_lambda_re[l], ssm_lambda_im[l], ssm_log_dt[l], ssm_b_re[l],
                                          ssm_b_im[l], ssm_c_re[l], ssm_c_im[l], ssm_d[l])
        row = lambda a: a[l][None, :]
        bias_table, bias_prompt = _bias_table(rel_bias[l])
        wg1, wu1 = _cast_pad([ffn1_w_gate[l], ffn1_w_up[l]], axis=1, mult=FF_PAD, block=256)
        wd1, = _cast_pad([ffn1_w_down[l]], axis=0, mult=FF_PAD, block=256, scale=0.5)
        p = dict(
            g_ffn1=row(norm_ffn1), g_mix=row(norm_mix), g_att=row(norm_att_out), g_ssm=row(norm_ssm_out),
            g_ffn2=row(norm_ffn2), g_final=row(norm_final), q_norm=row(q_norm), k_norm=row(k_norm),
            ffn1_wg=wg1, ffn1_wu=wu1, ffn1_wd=wd1,
            proj_side=(('ffn2_wg', ffn2_w_gate[l], 1.0), ('ffn2_wu', ffn2_w_up[l], 1.0)),
            att_side=(('ffn2_wd', ffn2_w_down[l], 0.5), ('w_glu', w_glu[l], 1.0), ('w_out', w_out[l], 1.0)),
            w_in=w_in[l].astype(BF16), b_glu=row(b_glu),
            bias_table=bias_table, bias_prompt=bias_prompt,
            lagk=lagk, bm=bm, cm=cm, acoef=acoef,
        )
        yp, kp, vp, rp, ip = _stream(yp, p, tm=1024, tf=FF_PAD, tb=CHUNK, ssm_groups=4)
        ff = wg1.shape[1]
        tf_sample = ff // 4 if ff % (4 * LANES) == 0 else FF_PAD
        ys, kd, vd, rd, idd = _stream(ys, p, tm=bs * ls, tf=tf_sample, tb=ls, ssm_groups=8,
                                      cache=(cache_attn_k[l], cache_attn_v[l], state_ssm_re[l], state_ssm_im[l]))
        as_cache = lambda a: a.reshape(a.shape[0], -1, N_HEADS, HEAD_DIM)
        for lst, val in zip(outs, (as_cache(kp), as_cache(vp), rp, ip, as_cache(kd), as_cache(vd), rd, idd)):
            lst.append(val)
    return (yp, ys) + tuple(jnp.stack(o) for o in outs)
```

```python
import functools

import jax
import jax.numpy as jnp
from jax import lax
from jax.experimental import pallas as pl
from jax.experimental.pallas import tpu as pltpu

EPS = 1e-6
NEG_INF = -1e30
CHUNK = 64
LEFT_CHUNKS = 8
ATT_LEFT = LEFT_CHUNKS * CHUNK
REL_CLIP = 256
N_HEADS = 8
HEAD_DIM = 128
SSM_GROUP = 16
SSM_STATE = 64
LANES = 128
SSM_T = 16
SSM_W = SSM_T * SSM_GROUP
GROUPS_PER_TILE = LANES // SSM_GROUP
ATT_QB = 256
ATT_KW = ATT_QB + ATT_LEFT
ATT_HEADS_PER_STEP = 4
REGROUP_PAD = 8
FF_PAD = 512
VMEM_LIMIT_V7X = 60 * 1024 * 1024

BF16 = jnp.bfloat16
F32 = jnp.float32


def _dot(a, b):
    return jnp.dot(a, b, preferred_element_type=F32)


def _rms(x, g):
    return x * lax.rsqrt(jnp.mean(x * x, axis=-1, keepdims=True) + EPS) * g


def _ffn_kernel(*refs, tf, n_chunks, final_norm, n_side):
    x_ref, g_ref, wg_hbm, wu_hbm, wd_hbm, gf_ref = refs[:6]
    o_ref = refs[6 + n_side]
    xn_ref, wg_buf, wu_buf, wd_buf, sem = refs[-5:]
    if n_side:
        _cast_pad_kernel(*refs[6:6 + n_side], *refs[7 + n_side:7 + 2 * n_side], scale=1.0)
    i = pl.program_id(0)
    first_slot = (i * n_chunks) % 2

    def chunk_copies(j, slot):
        cols = pl.ds(pl.multiple_of(j * tf, tf), tf)
        return (pltpu.make_async_copy(wg_hbm.at[:, cols], wg_buf.at[slot], sem.at[0, slot]),
                pltpu.make_async_copy(wu_hbm.at[:, cols], wu_buf.at[slot], sem.at[1, slot]),
                pltpu.make_async_copy(wd_hbm.at[cols, :], wd_buf.at[slot], sem.at[2, slot]))

    @pl.when(i == 0)
    def _():
        for c in chunk_copies(0, 0):
            c.start()

    x = x_ref[...]
    xn_ref[...] = _rms(x, g_ref[...]).astype(BF16)
    o_ref[...] = x

    def body(j, carry):
        slot = (first_slot + j) % 2
        nxt = jnp.where(j + 1 == n_chunks, 0, j + 1)
        for c in chunk_copies(nxt, 1 - slot):
            c.start()
        for c in chunk_copies(j, slot):
            c.wait()
        xn = xn_ref[...]
        a = _dot(xn, wg_buf[slot])
        b = _dot(xn, wu_buf[slot])
        h = (a * jax.nn.sigmoid(a) * b).astype(BF16)
        o_ref[...] += _dot(h, wd_buf[slot])
        return carry

    lax.fori_loop(0, n_chunks, body, 0)

    @pl.when(i == pl.num_programs(0) - 1)
    def _():
        for c in chunk_copies(0, (first_slot + n_chunks) % 2):
            c.wait()

    if final_norm:
        o_ref[...] = _rms(o_ref[...], gf_ref[...])


def _ffn(x, g, wg, wu, wd_half, gf, *, tm, tf, final_norm, side=()):
    n, d = x.shape
    fpad = wg.shape[1]
    hbm = pl.BlockSpec(memory_space=pl.ANY)
    side_in, side_out, side_shape = [], [], []
    for a in side:
        rows = a.shape[0] // (n // tm)
        assert rows * (n // tm) == a.shape[0] and rows % 16 == 0
        side_in.append(pl.BlockSpec((rows, a.shape[1]), lambda i: (i, 0)))
        side_out.append(pl.BlockSpec((rows, a.shape[1]), lambda i: (i, 0)))
        side_shape.append(jax.ShapeDtypeStruct(a.shape, BF16))
    return pl.pallas_call(
        functools.partial(_ffn_kernel, tf=tf, n_chunks=fpad // tf, final_norm=final_norm, n_side=len(side)),
        grid=(n // tm,),
        in_specs=[
            pl.BlockSpec((tm, d), lambda i: (i, 0)),
            pl.BlockSpec((1, d), lambda i: (0, 0)),
            hbm, hbm, hbm,
            pl.BlockSpec((1, d), lambda i: (0, 0)),
        ] + side_in,
        out_specs=[pl.BlockSpec((tm, d), lambda i: (i, 0))] + side_out,
        out_shape=[jax.ShapeDtypeStruct((n, d), F32)] + side_shape,
        scratch_shapes=[pltpu.VMEM((tm, d), BF16),
                        pltpu.VMEM((2, d, tf), BF16), pltpu.VMEM((2, d, tf), BF16), pltpu.VMEM((2, tf, d), BF16),
                        pltpu.SemaphoreType.DMA((3, 2))],
        compiler_params=pltpu.CompilerParams(
            dimension_semantics=("arbitrary",), vmem_limit_bytes=VMEM_LIMIT_V7X),
        name="ffn",
    )(x, g, wg, wu, wd_half, gf, *side)


def _granule_transpose(vs):
    gran = lax.broadcasted_iota(jnp.int32, vs[0].shape, 1) // SSM_GROUP
    for s in (4, 2, 1):
        upper = (gran & s) != 0
        new = list(vs)
        for a in range(GROUPS_PER_TILE):
            if a & s == 0:
                lo, hi = vs[a], vs[a + s]
                new[a] = jnp.where(upper, pltpu.roll(hi, s * SSM_GROUP, 1), lo)
                new[a + s] = jnp.where(upper, hi, pltpu.roll(lo, LANES - s * SSM_GROUP, 1))
        vs = new
    return vs


def _regroup_geometry(nb, tb):
    return tb + REGROUP_PAD, tb // SSM_T


def _tokens_to_groups(slab_ref, ug_ref, *, nb, tb):
    pitch, ncl = _regroup_geometry(nb, tb)
    for j in range(slab_ref.shape[0]):
        xs = [jnp.concatenate([slab_ref[j, pl.ds(c * SSM_T + t, nb, stride=pitch), :] for c in range(ncl)], axis=0)
              for t in range(SSM_T)]
        lo = _granule_transpose(xs[:GROUPS_PER_TILE])
        hi = _granule_transpose(xs[GROUPS_PER_TILE:])
        for gl in range(GROUPS_PER_TILE):
            ug_ref[j * GROUPS_PER_TILE + gl] = jnp.concatenate([lo[gl], hi[gl]], axis=1).astype(ug_ref.dtype)


def _groups_to_tokens(yg_ref, slab_ref, *, nb, tb):
    pitch, ncl = _regroup_geometry(nb, tb)
    for j in range(slab_ref.shape[0]):
        ds = [yg_ref[j * GROUPS_PER_TILE + gl].astype(F32) for gl in range(GROUPS_PER_TILE)]
        lo = _granule_transpose([d[:, :LANES] for d in ds])
        hi = _granule_transpose([d[:, LANES:] for d in ds])
        for t, x in enumerate(lo + hi):
            for c in range(ncl):
                slab_ref[j, pl.ds(c * SSM_T + t, nb, stride=pitch), :] = x[c * nb:(c + 1) * nb]


def _proj_kernel(*refs, first_tail_block, n_side):
    x_ref, g_ref, w_ref, qn_ref, kn_ref = refs[:5]
    q_ref, k_ref, v_ref, ug_ref, kf_ref, vf_ref = refs[5 + n_side:11 + n_side]
    slab_ref = refs[-1]
    if n_side:
        _cast_pad_kernel(*refs[5:5 + n_side], *refs[11 + n_side:11 + 2 * n_side], scale=1.0)
    i = pl.program_id(0)
    nb, tb, d = x_ref.shape
    d_att = N_HEADS * HEAD_DIM
    pitch, _ = _regroup_geometry(nb, tb)
    h = _rms(x_ref[...].reshape(nb * tb, d), g_ref[...]).astype(BF16)
    u = _dot(h, w_ref[:, 3 * d_att:])
    for j in range(slab_ref.shape[0]):
        for b in range(nb):
            slab_ref[j, b * pitch:b * pitch + tb, :] = u[b * tb:(b + 1) * tb, j * LANES:(j + 1) * LANES]
    _tokens_to_groups(slab_ref, ug_ref, nb=nb, tb=tb)

    def head_norm(y, gain):
        return [_rms(y[:, hd * HEAD_DIM:(hd + 1) * HEAD_DIM], gain) for hd in range(N_HEADS)]

    def store_tokens(ref, y):
        for b in range(nb):
            ref[b] = y[b * tb:(b + 1) * tb].astype(ref.dtype)

    qs = head_norm(_dot(h, w_ref[:, 0:d_att]), qn_ref[...] * (HEAD_DIM ** -0.5))
    store_tokens(q_ref, jnp.concatenate(qs, axis=1))
    ks = head_norm(_dot(h, w_ref[:, d_att:2 * d_att]), kn_ref[...])
    store_tokens(k_ref, jnp.concatenate(ks, axis=1))
    v = _dot(h, w_ref[:, 2 * d_att:3 * d_att])
    store_tokens(v_ref, v)

    @pl.when(i >= first_tail_block)
    def _():
        for b in range(nb):
            rows = slice(b * tb, (b + 1) * tb)
            for hd in range(N_HEADS):
                head_rows = pl.ds(hd, tb, stride=N_HEADS)
                kf_ref[b, head_rows, :] = ks[hd][rows]
                vf_ref[b, head_rows, :] = v[rows, hd * HEAD_DIM:(hd + 1) * HEAD_DIM]


def _proj(x, g, w, qn, kn, *, tb, side=()):
    nb, seq, d = x.shape
    d_att = N_HEADS * HEAD_DIM
    d_ssm = w.shape[1] - 3 * d_att
    n_g = d_ssm // SSM_GROUP
    n_blocks = seq // tb
    tail = min(ATT_LEFT, seq)
    first_tail_block = n_blocks - tail // tb
    pitch, ncl = _regroup_geometry(nb, tb)
    tok = lambda i: (0, i, 0)
    const = lambda i: (0, 0)
    tail_map = lambda i: (0, jnp.maximum(i - first_tail_block, 0), 0)
    side_in, side_out, side_shape = [], [], []
    for a in side:
        rows, cols = a.shape[0] // n_blocks, -(-a.shape[1] // FF_PAD) * FF_PAD
        assert rows * n_blocks == a.shape[0] and rows % 16 == 0
        side_in.append(pl.BlockSpec((rows, a.shape[1]), lambda i: (i, 0)))
        side_out.append(pl.BlockSpec((rows, cols), lambda i: (i, 0)))
        side_shape.append(jax.ShapeDtypeStruct((a.shape[0], cols), BF16))
    return pl.pallas_call(
        functools.partial(_proj_kernel, first_tail_block=first_tail_block, n_side=len(side)),
        grid=(n_blocks,),
        in_specs=[
            pl.BlockSpec((nb, tb, d), tok),
            pl.BlockSpec((1, d), const),
            pl.BlockSpec(w.shape, const, pipeline_mode=pl.Buffered(1)),
            pl.BlockSpec((1, HEAD_DIM), const),
            pl.BlockSpec((1, HEAD_DIM), const),
        ] + side_in,
        out_specs=[
            pl.BlockSpec((nb, tb, d_att), tok),
            pl.BlockSpec((nb, tb, d_att), tok),
            pl.BlockSpec((nb, tb, d_att), tok),
            pl.BlockSpec((n_g, ncl * nb, SSM_W), tok),
            pl.BlockSpec((nb, tb * N_HEADS, HEAD_DIM), tail_map),
            pl.BlockSpec((nb, tb * N_HEADS, HEAD_DIM), tail_map),
        ] + side_out,
        out_shape=[
            jax.ShapeDtypeStruct((nb, seq, d_att), BF16),
            jax.ShapeDtypeStruct((nb, seq, d_att), BF16),
            jax.ShapeDtypeStruct((nb, seq, d_att), BF16),
            jax.ShapeDtypeStruct((n_g, (seq // SSM_T) * nb, SSM_W), BF16),
            jax.ShapeDtypeStruct((nb, tail * N_HEADS, HEAD_DIM), F32),
            jax.ShapeDtypeStruct((nb, tail * N_HEADS, HEAD_DIM), F32),
        ] + side_shape,
        scratch_shapes=[pltpu.VMEM((d_ssm // LANES, nb * pitch, LANES), F32)],
        compiler_params=pltpu.CompilerParams(
            dimension_semantics=("arbitrary",), vmem_limit_bytes=VMEM_LIMIT_V7X),
        name="proj",
    )(x, g, w, qn, kn, *side)


def _softmax_pv(s, v):
    m = jnp.max(s, axis=-1, keepdims=True)
    p = jnp.exp(s - m)
    l = jnp.sum(p, axis=-1, keepdims=True)
    return _dot(p.astype(BF16), v) / l


def _qk(q, k):
    return lax.dot_general(q, k, (((1,), (1,)), ((), ())), preferred_element_type=F32)


def _attn_prompt_kernel(*refs, n_blocks, side_scales):
    n_side = len(side_scales)
    q_ref, k_ref, v_ref, bias_ref = refs[:4]
    o_ref = refs[4 + n_side]
    s_ref, p_ref, linv_ref = refs[-3:]
    for w_ref, wb_ref, scale in zip(refs[4:4 + n_side], refs[5 + n_side:5 + 2 * n_side], side_scales):
        _cast_pad_kernel(w_ref, wb_ref, scale=scale)
    lead = ATT_LEFT // ATT_QB
    n_full = n_blocks - lead

    def lanes(h):
        return slice(h * HEAD_DIM, (h + 1) * HEAD_DIM)

    def start(i):
        return pl.multiple_of(i * ATT_QB, ATT_QB)

    for i in range(min(lead, n_blocks)):
        kw = (i + 1) * ATT_QB
        rows = slice(i * ATT_QB, (i + 1) * ATT_QB)
        ss = [_qk(q_ref[rows, lanes(h)], k_ref[0:kw, lanes(h)]) for h in range(ATT_HEADS_PER_STEP)]
        for h in range(ATT_HEADS_PER_STEP):
            s = ss[h] + bias_ref[h, :, ATT_KW - kw:]
            o_ref[rows, lanes(h)] = _softmax_pv(s, v_ref[0:kw, lanes(h)]).astype(BF16)
    if n_full <= 0:
        return

    def scores(h, i, slot):
        s_ref[slot] = _qk(q_ref[pl.ds(start(i), ATT_QB), lanes(h)],
                          k_ref[pl.ds(start(i) - ATT_LEFT, ATT_KW), lanes(h)])

    def softmax(h, slot):
        win = ATT_KW - LANES
        for c in range(ATT_QB // CHUNK):
            rows = slice(c * CHUNK, (c + 1) * CHUNK)
            c0 = (c * CHUNK) // LANES * LANES
            assert c0 + win <= ATT_KW and c0 <= c * CHUNK and (c + LEFT_CHUNKS + 1) * CHUNK <= c0 + win
            s = s_ref[slot, rows, c0:c0 + win] + bias_ref[h, rows, c0:c0 + win]
            p = jnp.exp(s - jnp.max(s, axis=-1, keepdims=True))
            p_ref[slot, rows, c0:c0 + win] = p.astype(BF16)
            dead = slice(win, ATT_KW) if c0 == 0 else slice(0, c0)
            p_ref[slot, rows, dead] = jnp.zeros((CHUNK, LANES), BF16)
            linv_ref[slot, rows] = jnp.broadcast_to(1.0 / jnp.sum(p, axis=-1, keepdims=True), (CHUNK, HEAD_DIM))

    def output(h, i, slot):
        v = v_ref[pl.ds(start(i) - ATT_LEFT, ATT_KW), lanes(h)]
        o_ref[pl.ds(start(i), ATT_QB), lanes(h)] = (_dot(p_ref[slot], v) * linv_ref[slot]).astype(BF16)

    for h0 in range(0, ATT_HEADS_PER_STEP, 2):
        h1 = h0 + 1
        scores(h0, lead, 0)
        scores(h1, lead, 1)
        softmax(h0, 0)

        def body(r, carry):
            i = lead + r
            scores(h0, i + 1, 0)
            output(h0, i, 0)
            softmax(h1, 1)
            scores(h1, i + 1, 1)
            output(h1, i, 1)
            softmax(h0, 0)
            return carry

        lax.fori_loop(0, n_full - 1, body, 0)
        output(h0, n_blocks - 1, 0)
        softmax(h1, 1)
        output(h1, n_blocks - 1, 1)


def _attn_prompt(q, k, v, bias, *, side=(), side_scales=()):
    batch, seq, d_att = q.shape
    hs = ATT_HEADS_PER_STEP
    n_hg = N_HEADS // hs
    blk = pl.BlockSpec((None, seq, hs * HEAD_DIM), lambda b, h: (b, 0, h))
    side_in, side_out, side_shape = [], [], []
    for a in side:
        cols, rows = a.shape[1] // (batch * n_hg), -(-a.shape[0] // FF_PAD) * FF_PAD
        assert cols * batch * n_hg == a.shape[1] and cols % LANES == 0
        side_in.append(pl.BlockSpec((a.shape[0], cols), lambda b, h: (0, b * n_hg + h)))
        side_out.append(pl.BlockSpec((rows, cols), lambda b, h: (0, b * n_hg + h)))
        side_shape.append(jax.ShapeDtypeStruct((rows, a.shape[1]), BF16))
    return pl.pallas_call(
        functools.partial(_attn_prompt_kernel, n_blocks=seq // ATT_QB, side_scales=tuple(side_scales)),
        grid=(batch, n_hg),
        in_specs=[blk, blk, blk, pl.BlockSpec((hs, ATT_QB, ATT_KW), lambda b, h: (h, 0, 0))] + side_in,
        out_specs=[blk] + side_out,
        out_shape=[jax.ShapeDtypeStruct((batch, seq, d_att), BF16)] + side_shape,
        scratch_shapes=[pltpu.VMEM((2, ATT_QB, ATT_KW), F32), pltpu.VMEM((2, ATT_QB, ATT_KW), BF16),
                        pltpu.VMEM((2, ATT_QB, HEAD_DIM), F32)],
        compiler_params=pltpu.CompilerParams(
            dimension_semantics=("parallel", "parallel"), vmem_limit_bytes=VMEM_LIMIT_V7X),
        name="attn_prompt",
    )(q, k, v, bias, *side)


def _attn_sample_kernel(q_ref, kn_ref, vn_ref, kc_ref, vc_ref, bias_ref, o_ref, ko_ref, vo_ref, *, w_cache, seq):
    def head_rows(ref, n, hd):
        return ref[pl.ds(hd, n, stride=N_HEADS), :].astype(BF16)

    for hd in range(N_HEADS):
        sl = slice(hd * HEAD_DIM, (hd + 1) * HEAD_DIM)
        q = q_ref[:, sl]
        s1 = _qk(q, head_rows(kc_ref, w_cache, hd)) + bias_ref[hd, :, :w_cache]
        s2 = _qk(q, head_rows(kn_ref, seq, hd)) + bias_ref[hd, :, w_cache:]
        m = jnp.maximum(jnp.max(s1, axis=-1, keepdims=True), jnp.max(s2, axis=-1, keepdims=True))
        p1 = jnp.exp(s1 - m)
        p2 = jnp.exp(s2 - m)
        l = jnp.sum(p1, axis=-1, keepdims=True) + jnp.sum(p2, axis=-1, keepdims=True)
        o = (_dot(p1.astype(BF16), head_rows(vc_ref, w_cache, hd))
             + _dot(p2.astype(BF16), head_rows(vn_ref, seq, hd)))
        o_ref[:, sl] = (o / l).astype(BF16)

    keep = (w_cache - seq) * N_HEADS
    for new_ref, old_ref, out_ref in ((kn_ref, kc_ref, ko_ref), (vn_ref, vc_ref, vo_ref)):
        out_ref[:keep, :] = old_ref[seq * N_HEADS:, :]
        out_ref[keep:, :] = new_ref[...]


def _attn_sample(q, kn, vn, kc, vc, bias):
    batch, seq, d_att = q.shape
    w_cache = kc.shape[1] // N_HEADS
    assert seq <= w_cache
    tok = pl.BlockSpec((None, seq, d_att), lambda b: (b, 0, 0))
    new = pl.BlockSpec((None, seq * N_HEADS, HEAD_DIM), lambda b: (b, 0, 0))
    cache = pl.BlockSpec((None, w_cache * N_HEADS, HEAD_DIM), lambda b: (b, 0, 0))
    return pl.pallas_call(
        functools.partial(_attn_sample_kernel, w_cache=w_cache, seq=seq),
        grid=(batch,),
        in_specs=[tok, new, new, cache, cache, pl.BlockSpec(bias.shape, lambda b: (0, 0, 0))],
        out_specs=[tok, cache, cache],
        out_shape=[jax.ShapeDtypeStruct((batch, seq, d_att), BF16),
                   jax.ShapeDtypeStruct(kc.shape, F32), jax.ShapeDtypeStruct(vc.shape, F32)],
        compiler_params=pltpu.CompilerParams(
            dimension_semantics=("parallel",), vmem_limit_bytes=VMEM_LIMIT_V7X),
        name="attn_sample",
    )(q, kn, vn, kc, vc, bias)


def _ssm_kernel(u_ref, lag_ref, bm_ref, cm_ref, a_ref, s0_ref, y_ref, sre_ref, sim_ref, sl_ref, sp_ref, km_ref,
                *, rows, n_chunks, groups):
    half = 2 * SSM_STATE
    lane = lax.broadcasted_iota(jnp.int32, (SSM_GROUP, SSM_W), 1)
    for gi in range(groups):
        lag = lag_ref[gi]
        for t_in in range(SSM_T):
            shifted = lag if t_in == 0 else pltpu.roll(lag, t_in * SSM_GROUP, 1)
            km_ref[gi, t_in * SSM_GROUP:(t_in + 1) * SSM_GROUP, :] = jnp.where(
                lane >= t_in * SSM_GROUP, shifted, 0.0).astype(BF16)
        sl_ref[gi] = _dot(u_ref[gi], bm_ref[gi])

    coef = []
    for gi in range(groups):
        coef.append(tuple(jnp.broadcast_to(a_ref[gi, r:r + 1, :], (rows, half)) for r in range(3)))

    def body(c, carry):
        r0 = pl.multiple_of(c * rows, rows)
        new = []
        for gi in range(groups):
            s, sw = carry[gi]
            a1, a2, a2w = coef[gi]
            sp_ref[gi, pl.ds(r0, rows), :] = s
            loc = sl_ref[gi, pl.ds(r0, rows), :]
            new.append((a1 * s + a2 * sw + loc[:, :half], a1 * sw + a2w * s + loc[:, half:]))
        return tuple(new)

    init = tuple((s0_ref[gi, 0], s0_ref[gi, 1]) for gi in range(groups))
    last = lax.fori_loop(0, n_chunks, body, init)
    for gi in range(groups):
        sre_ref[:, gi * SSM_STATE:(gi + 1) * SSM_STATE] = last[gi][0][:, :SSM_STATE]
        sim_ref[:, gi * SSM_STATE:(gi + 1) * SSM_STATE] = last[gi][0][:, SSM_STATE:]
        y = _dot(u_ref[gi], km_ref[gi]) + _dot(sp_ref[gi].astype(BF16), cm_ref[gi])
        y_ref[gi] = jax.nn.gelu(y).astype(BF16)


def _ssm(ug, lagk, bm, cm, acoef, s0, *, rows, groups):
    n_g, n_rows, _ = ug.shape
    n_chunks = n_rows // rows
    half = 2 * SSM_STATE
    g3 = lambda g: (g, 0, 0)
    return pl.pallas_call(
        functools.partial(_ssm_kernel, rows=rows, n_chunks=n_chunks, groups=groups),
        grid=(n_g // groups,),
        in_specs=[
            pl.BlockSpec((groups, n_rows, SSM_W), g3),
            pl.BlockSpec((groups, SSM_GROUP, SSM_W), g3),
            pl.BlockSpec((groups, SSM_W, 2 * half), g3),
            pl.BlockSpec((groups, half, SSM_W), g3),
            pl.BlockSpec((groups, 3, half), g3),
            pl.BlockSpec((groups, 2, rows, half), lambda g: (g, 0, 0, 0)),
        ],
        out_specs=[
            pl.BlockSpec((groups, n_rows, SSM_W), g3),
            pl.BlockSpec((rows, groups * SSM_STATE), lambda g: (0, g)),
            pl.BlockSpec((rows, groups * SSM_STATE), lambda g: (0, g)),
        ],
        out_shape=[
            jax.ShapeDtypeStruct((n_g, n_rows, SSM_W), BF16),
            jax.ShapeDtypeStruct((rows, n_g * SSM_STATE), F32),
            jax.ShapeDtypeStruct((rows, n_g * SSM_STATE), F32),
        ],
        scratch_shapes=[
            pltpu.VMEM((groups, n_rows, 2 * half), F32),
            pltpu.VMEM((groups, n_rows, half), F32),
            pltpu.VMEM((groups, SSM_W, SSM_W), BF16),
        ],
        compiler_params=pltpu.CompilerParams(
            dimension_semantics=("parallel",), vmem_limit_bytes=VMEM_LIMIT_V7X),
        name="ssm",
    )(ug, lagk, bm, cm, acoef, s0)


def _mix_kernel(att_ref, yg_ref, x_ref, wglu_ref, bglu_ref, ga_ref, gs_ref, wout_ref, o_ref, slab_ref, y_ref):
    nb, tb, d = x_ref.shape
    d_att = att_ref.shape[2]
    d_ssm = y_ref.shape[1]
    pitch, _ = _regroup_geometry(nb, tb)
    att = att_ref[...].reshape(nb * tb, d_att).astype(F32)
    mix_a = _rms(att, ga_ref[...]).astype(BF16)
    o = x_ref[...].reshape(nb * tb, d) + _dot(mix_a, wout_ref[:d_att, :])
    _groups_to_tokens(yg_ref, slab_ref, nb=nb, tb=tb)
    for j in range(slab_ref.shape[0]):
        for b in range(nb):
            y_ref[b * tb:(b + 1) * tb, j * LANES:(j + 1) * LANES] = slab_ref[j, b * pitch:b * pitch + tb, :].astype(BF16)
    glu = _dot(y_ref[...], wglu_ref[...]) + bglu_ref[...]
    ssm_out = glu[:, :d_ssm] * jax.nn.sigmoid(glu[:, d_ssm:])
    mix_s = _rms(ssm_out, gs_ref[...]).astype(BF16)
    o = o + _dot(mix_s, wout_ref[d_att:, :])
    o_ref[...] = o.reshape(nb, tb, d)


def _mix(att, yg, x, wglu, bglu, ga, gs, wout, *, tb):
    nb, seq, d = x.shape
    d_att = att.shape[2]
    n_g = yg.shape[0]
    d_ssm = n_g * SSM_GROUP
    pitch, ncl = _regroup_geometry(nb, tb)
    tok = lambda i: (0, i, 0)
    const = lambda i: (0, 0)
    once = lambda a: pl.BlockSpec(a.shape, const, pipeline_mode=pl.Buffered(1))
    return pl.pallas_call(
        _mix_kernel,
        grid=(seq // tb,),
        in_specs=[
            pl.BlockSpec((nb, tb, d_att), tok),
            pl.BlockSpec((n_g, ncl * nb, SSM_W), tok),
            pl.BlockSpec((nb, tb, d), tok),
            once(wglu), once(bglu), once(ga), once(gs), once(wout),
        ],
        out_specs=pl.BlockSpec((nb, tb, d), tok),
        out_shape=jax.ShapeDtypeStruct((nb, seq, d), F32),
        scratch_shapes=[pltpu.VMEM((d_ssm // LANES, nb * pitch, LANES), F32),
                        pltpu.VMEM((nb * tb, d_ssm), BF16)],
        compiler_params=pltpu.CompilerParams(
            dimension_semantics=("parallel",), vmem_limit_bytes=VMEM_LIMIT_V7X),
        name="mix",
    )(att, yg, x, wglu, bglu, ga, gs, wout)


def _ssm_matrices(lam_re, lam_im, log_dt, b_re, b_im, c_re, c_im, d_skip):
    hp = lax.Precision.HIGHEST
    n_g = lam_re.shape[0]
    dt = jnp.exp(log_dt)[:, None]
    n = jnp.arange(SSM_T + 1, dtype=F32)
    mag = jnp.exp((lam_re * dt)[..., None] * n)
    ang = (lam_im * dt)[..., None] * n
    pw_re, pw_im = mag * jnp.cos(ang), mag * jnp.sin(ang)
    x, y = pw_re[..., 1] - 1.0, pw_im[..., 1]
    den = lam_re * lam_re + lam_im * lam_im
    z_re, z_im = ((x * lam_re + y * lam_im) / den)[:, None, :], ((y * lam_re - x * lam_im) / den)[:, None, :]
    bt_re, bt_im = b_re.transpose(0, 2, 1), b_im.transpose(0, 2, 1)
    bb_re, bb_im = z_re * bt_re - z_im * bt_im, z_re * bt_im + z_im * bt_re
    ct_re, ct_im = c_re.transpose(0, 2, 1), c_im.transpose(0, 2, 1)
    per_step = lambda a: jnp.repeat(a, SSM_GROUP, axis=-1)
    per_chan = lambda a: jnp.tile(a, (1, 1, SSM_T + 1))
    cp_re = per_chan(ct_re) * per_step(pw_re) - per_chan(ct_im) * per_step(pw_im)
    cp_im = per_chan(ct_re) * per_step(pw_im) + per_chan(ct_im) * per_step(pw_re)
    cm = jnp.concatenate([cp_re[..., SSM_GROUP:], -cp_im[..., SSM_GROUP:]], axis=1)
    lagk = (jnp.einsum('gip,gpc->gic', bb_re, cp_re[..., :SSM_W], precision=hp)
            - jnp.einsum('gip,gpc->gic', bb_im, cp_im[..., :SSM_W], precision=hp))
    lagk = lagk.at[:, :, :SSM_GROUP].add(jnp.eye(SSM_GROUP, dtype=F32)[None] * d_skip[:, :, None])
    back_re = pw_re[..., SSM_T - 1::-1].transpose(0, 2, 1)[:, :, None, :]
    back_im = pw_im[..., SSM_T - 1::-1].transpose(0, 2, 1)[:, :, None, :]
    inj_re = (back_re * bb_re[:, None] - back_im * bb_im[:, None]).reshape(n_g, SSM_W, SSM_STATE)
    inj_im = (back_re * bb_im[:, None] + back_im * bb_re[:, None]).reshape(n_g, SSM_W, SSM_STATE)
    bm = jnp.concatenate([inj_re, inj_im, inj_im, inj_re], axis=-1)
    ar, ai = pw_re[..., SSM_T], pw_im[..., SSM_T]
    acoef = jnp.stack([jnp.concatenate([ar, ar], -1), jnp.concatenate([-ai, ai], -1),
                       jnp.concatenate([ai, -ai], -1)], axis=1)
    return lagk, bm.astype(BF16), cm.astype(BF16), acoef


def _bias_table(rel_bias):
    n_heads = rel_bias.shape[0]
    ext = ATT_QB + ATT_KW
    n_edge = ATT_LEFT - REL_CLIP + 1
    assert ATT_KW - n_edge == 2 * REL_CLIP - 1 and ATT_LEFT >= REL_CLIP
    far = rel_bias[:, 2 * REL_CLIP:]
    row = jnp.concatenate([jnp.broadcast_to(far, (n_heads, n_edge)), rel_bias[:, 1:2 * REL_CLIP][:, ::-1],
                           jnp.broadcast_to(far, (n_heads, ATT_QB))], axis=1)
    plain = jax.ShapeDtypeStruct((n_heads, ATT_QB, ATT_KW), F32)
    out_spec = pl.BlockSpec((None, ATT_QB, ATT_KW), lambda h: (h, 0, 0))
    return pl.pallas_call(
        _bias_table_kernel,
        grid=(n_heads,),
        in_specs=[pl.BlockSpec((None, 1, ext), lambda h: (h, 0, 0))],
        out_specs=[out_spec, out_spec],
        out_shape=[plain, plain],
        name="bias_table",
    )(row[:, None, :].astype(F32))


def _bias_table_kernel(row_ref, table_ref, banded_ref):
    rows = jnp.broadcast_to(row_ref[...], (ATT_QB, row_ref.shape[1]))
    table = pltpu.roll(rows, 0, 1, stride=1, stride_axis=0)[:, :ATT_KW]
    table_ref[...] = table
    q_chunk = lax.broadcasted_iota(jnp.int32, table.shape, 0) // CHUNK
    k_chunk = lax.broadcasted_iota(jnp.int32, table.shape, 1) // CHUNK
    band = (k_chunk >= q_chunk) & (k_chunk <= q_chunk + LEFT_CHUNKS)
    banded_ref[...] = jnp.where(band, table, NEG_INF)


def _state_rows(s_re, s_im):
    s = jnp.concatenate([s_re, s_im], -1).transpose(1, 0, 2)
    sw = jnp.concatenate([s_im, s_re], -1).transpose(1, 0, 2)
    return jnp.stack([s, sw], axis=1)


def _cast_pad_kernel(*refs, scale):
    n = len(refs) // 2
    for w_ref, o_ref in zip(refs[:n], refs[n:]):
        r, c = w_ref.shape
        w = w_ref[...]
        o_ref[:r, :c] = (w if scale == 1.0 else w * scale).astype(BF16)
        if o_ref.shape[0] > r:
            o_ref[r:, :] = jnp.zeros((o_ref.shape[0] - r, o_ref.shape[1]), BF16)
        if o_ref.shape[1] > c:
            o_ref[:, c:] = jnp.zeros((o_ref.shape[0], o_ref.shape[1] - c), BF16)


def _cast_pad(ws, *, axis, mult, block, scale=1.0):
    r, c = ws[0].shape
    padded = -(-ws[0].shape[axis] // mult) * mult
    if axis == 1:
        in_spec, out_spec = pl.BlockSpec((block, c), lambda i: (i, 0)), pl.BlockSpec((block, padded), lambda i: (i, 0))
        out_shape, steps = jax.ShapeDtypeStruct((r, padded), BF16), r // block
    else:
        in_spec, out_spec = pl.BlockSpec((r, block), lambda i: (0, i)), pl.BlockSpec((padded, block), lambda i: (0, i))
        out_shape, steps = jax.ShapeDtypeStruct((padded, c), BF16), c // block
    return pl.pallas_call(
        functools.partial(_cast_pad_kernel, scale=scale),
        grid=(steps,),
        in_specs=[in_spec] * len(ws),
        out_specs=[out_spec] * len(ws),
        out_shape=[out_shape] * len(ws),
        compiler_params=pltpu.CompilerParams(
            dimension_semantics=("parallel",), vmem_limit_bytes=VMEM_LIMIT_V7X),
        name="cast_pad",
    )(*ws)


def _side_jobs_fit(w_rows, w_cols, batch, proj_steps):
    att_steps = batch * (N_HEADS // ATT_HEADS_PER_STEP)
    return (w_rows.shape[0] % (16 * proj_steps) == 0) and (w_cols.shape[1] % (LANES * att_steps) == 0)


def _stream(x, p, *, tm, tf, tb, ssm_groups, cache=None):
    batch, seq, d = x.shape
    ffn = functools.partial(_ffn, tm=tm, tf=tf)
    ffn_side = p.pop('ffn_side', ())
    steps = (batch * seq) // tm
    if ffn_side and (cache is not None or any(w.shape[0] % (16 * steps) for _, w, _ in ffn_side)):
        for name, w, _ in ffn_side:
            p[name] = w.astype(BF16)
        ffn_side = ()
    x1, *cast = ffn(x.reshape(batch * seq, d), p['g_ffn1'], p['ffn1_wg'], p['ffn1_wu'], p['ffn1_wd'], p['g_final'],
                    final_norm=False, side=[w for _, w, _ in ffn_side])
    p.update({name: c for (name, _, _), c in zip(ffn_side, cast)})
    x1 = x1.reshape(batch, seq, d)
    proj_side, att_side = p.pop('proj_side', ()), p.pop('att_side', ())
    if proj_side and (cache is not None or not _side_jobs_fit(proj_side[0][1], att_side[0][1], batch, seq // tb)):
        for name, w, scale in proj_side:
            p[name], = _cast_pad([w], axis=1, mult=FF_PAD, block=256, scale=scale)
        for name, w, scale in att_side:
            p[name], = _cast_pad([w], axis=0, mult=FF_PAD, block=256, scale=scale)
        proj_side, att_side = (), ()
    q, k, v, ug, kf, vf, *cast = _proj(x1, p['g_mix'], p['w_in'], p['q_norm'], p['k_norm'], tb=tb,
                                       side=[w for _, w, _ in proj_side])
    p.update({name: c for (name, _, _), c in zip(proj_side, cast)})
    n_g = ug.shape[0]
    if cache is None:
        att, *cast = _attn_prompt(q, k, v, p['bias_prompt'], side=[w for _, w, _ in att_side],
                                  side_scales=[scale for _, _, scale in att_side])
        p.update({name: c for (name, _, _), c in zip(att_side, cast)})
        s0 = jnp.zeros((n_g, 2, batch, 2 * SSM_STATE), F32)
    else:
        ck, cv, s_re, s_im = cache
        w_cache = ck.shape[1]
        assert w_cache == ATT_LEFT and seq <= ATT_QB
        att, kf, vf = _attn_sample(q, kf, vf, ck.reshape(batch, w_cache * N_HEADS, HEAD_DIM),
                                   cv.reshape(batch, w_cache * N_HEADS, HEAD_DIM),
                                   p['bias_table'][:, :seq, :w_cache + seq])
        s0 = _state_rows(s_re, s_im)
    yg, s_re, s_im = _ssm(ug, p['lagk'], p['bm'], p['cm'], p['acoef'], s0, rows=batch, groups=ssm_groups)
    x2 = _mix(att, yg, x1, p['w_glu'], p['b_glu'], p['g_att'], p['g_ssm'], p['w_out'], tb=tb)
    y = ffn(x2.reshape(batch * seq, d), p['g_ffn2'], p['ffn2_wg'], p['ffn2_wu'], p['ffn2_wd'], p['g_final'],
            final_norm=True)[0].reshape(batch, seq, d)
    return y, kf, vf, s_re.reshape(batch, n_g, SSM_STATE), s_im.reshape(batch, n_g, SSM_STATE)


def kernel(x_prompt, x_sample, cache_attn_k, cache_attn_v, state_ssm_re, state_ssm_im, norm_ffn1, ffn1_w_gate, ffn1_w_up, ffn1_w_down, norm_mix, w_in, q_norm, k_norm, rel_bias, ssm_lambda_re, ssm_lambda_im, ssm_log_dt, ssm_b_re, ssm_b_im, ssm_c_re, ssm_c_im, ssm_d, w_glu, b_glu, norm_att_out, norm_ssm_out, w_out, norm_ffn2, ffn2_w_gate, ffn2_w_up, ffn2_w_down, norm_final):
    depth = norm_ffn1.shape[0]
    bs, ls, _ = x_sample.shape
    yp, ys = x_prompt, x_sample
    outs = [[] for _ in range(8)]
    for l in range(depth):
        lagk, bm, cm, acoef = _ssm_matrices(ssm_lambda_re[l], ssm_lambda_im[l], ssm_log_dt[l], ssm_b_re[l],
                                          ssm_b_im[l], ssm_c_re[l], ssm_c_im[l], ssm_d[l])
        row = lambda a: a[l][None, :]
        bias_table, bias_prompt = _bias_table(rel_bias[l])
        wg1, wu1 = _cast_pad([ffn1_w_gate[l], ffn1_w_up[l]], axis=1, mult=FF_PAD, block=256)
        wd1, = _cast_pad([ffn1_w_down[l]], axis=0, mult=FF_PAD, block=256, scale=0.5)
        p = dict(
            g_ffn1=row(norm_ffn1), g_mix=row(norm_mix), g_att=row(norm_att_out), g_ssm=row(norm_ssm_out),
            g_ffn2=row(norm_ffn2), g_final=row(norm_final), q_norm=row(q_norm), k_norm=row(k_norm),
            ffn1_wg=wg1, ffn1_wu=wu1, ffn1_wd=wd1,
            proj_side=(('ffn2_wg', ffn2_w_gate[l], 1.0), ('ffn2_wu', ffn2_w_up[l], 1.0)),
            att_side=(('ffn2_wd', ffn2_w_down[l], 0.5), ('w_glu', w_glu[l], 1.0), ('w_out', w_out[l], 1.0)),
            ffn_side=(('w_in', w_in[l], 1.0),), b_glu=row(b_glu),
            bias_table=bias_table, bias_prompt=bias_prompt,
            lagk=lagk, bm=bm, cm=cm, acoef=acoef,
        )
        yp, kp, vp, rp, ip = _stream(yp, p, tm=1024, tf=FF_PAD, tb=CHUNK, ssm_groups=4)
        ff = wg1.shape[1]
        tf_sample = ff // 4 if ff % (4 * LANES) == 0 else FF_PAD
        ys, kd, vd, rd, idd = _stream(ys, p, tm=bs * ls, tf=tf_sample, tb=ls, ssm_groups=8,
                                      cache=(cache_attn_k[l], cache_attn_v[l], state_ssm_re[l], state_ssm_im[l]))
        as_cache = lambda a: a.reshape(a.shape[0], -1, N_HEADS, HEAD_DIM)
        for lst, val in zip(outs, (as_cache(kp), as_cache(vp), rp, ip, as_cache(kd), as_cache(vd), rd, idd)):
            lst.append(val)
    return (yp, ys) + tuple(jnp.stack(o) for o in outs)
```
